```python
import jax, jax.numpy as jnp
from jax import lax
import numpy as np

D_MODEL = 1024
BATCH = 2
SEQ = 8192
DEPTH = 2
DEC_BATCH = 32
DEC_SEQ = 4
PAST_LEN = 8192
PAGE_SIZE = 128

HEAD_DIM = 64
N_HEADS = D_MODEL // HEAD_DIM
N_KV = 4
HPG = N_HEADS // N_KV
Q_W = N_HEADS * HEAD_DIM
KV_W = N_KV * HEAD_DIM
NSA_SPLITS = tuple(Q_W + i * KV_W for i in range(7))
D_FF = ((8 * D_MODEL // 3 + 127) // 128) * 128
PLE_DIM = 256
CONV_W = 3
CMP_BLK = 64
SEL_BLK = 64
TOP_N = 16
WINDOW = 512
CMP_HID = 2 * HEAD_DIM
QBLK = 128
ROPE_THETA = 10000.0
RMS_EPS = 1e-6
FORCE_BONUS = 1e4
N_CONV = (DEPTH + 1) // 2
N_NSA = DEPTH // 2

kernel_name = 'hybrid_conv_nsa_macaron_step'


def rms_norm(x, g):
    x32 = x.astype(jnp.float32)
    y = x32 * lax.rsqrt(jnp.mean(x32 * x32, axis=-1, keepdims=True) + RMS_EPS)
    return (y * g.astype(jnp.float32)).astype(x.dtype)


def swiglu(h, w_gu, w_down):
    g, u = jnp.split(h @ w_gu, 2, axis=-1)
    return (jax.nn.silu(g) * u) @ w_down


def rope(x, pos):
    half = HEAD_DIM // 2
    inv = ROPE_THETA ** (-jnp.arange(half, dtype=jnp.float32) / half)
    ang = pos.astype(jnp.float32)[:, None] * inv
    ang = ang.reshape((pos.shape[0],) + (1,) * (x.ndim - 3) + (half,))
    cos, sin = jnp.cos(ang), jnp.sin(ang)
    x1 = x[..., :half].astype(jnp.float32)
    x2 = x[..., half:].astype(jnp.float32)
    return jnp.concatenate([x1 * cos - x2 * sin, x2 * cos + x1 * sin], axis=-1).astype(x.dtype)


def masked_softmax(s, mask):
    s = jnp.where(mask, s.astype(jnp.float32), -jnp.inf)
    m = jnp.max(s, axis=-1, keepdims=True)
    m = jnp.where(jnp.isfinite(m), m, 0.0)
    e = jnp.exp(s - m)
    return e / jnp.maximum(jnp.sum(e, axis=-1, keepdims=True), 1e-30)


def conv_mixer(h, prev, w_in, w_conv, w_out):
    t = h.shape[1]
    b_gate, c_gate, v = jnp.split(h @ w_in, 3, axis=-1)
    u = jnp.concatenate([prev.astype(h.dtype), c_gate * v], axis=1)
    y = sum(w_conv[j] * u[:, j:j + t] for j in range(CONV_W))
    return (b_gate * y) @ w_out, u[:, -(CONV_W - 1):]


def nsa_project(h, pos, w_in, qk_g):
    b, t, _ = h.shape
    q, kc, vc, ks, vs, kw, vw, gt = jnp.split(h @ w_in, NSA_SPLITS, axis=-1)
    heads = lambda a: a.reshape(b, t, N_KV, HEAD_DIM)
    q = rms_norm(q.reshape(b, t, N_KV, HPG, HEAD_DIM), qk_g[0])
    k_sel = rope(rms_norm(heads(ks), qk_g[2]), pos)
    k_win = rope(rms_norm(heads(kw), qk_g[3]), pos)
    gates = jax.nn.sigmoid(gt.reshape(b, t, N_KV, HPG, 3))
    return q, rope(q, pos), heads(kc), heads(vc), k_sel, heads(vs), k_win, heads(vw), gates


def compress(rows, pos_emb, w1, w2):
    b, l = rows.shape[:2]
    nb = l // CMP_BLK
    blk = rows[:, :nb * CMP_BLK].reshape(b, nb, CMP_BLK, N_KV, HEAD_DIM) + pos_emb[:, None, :]
    blk = blk.transpose(0, 1, 3, 2, 4).reshape(b, nb, N_KV, CMP_BLK * HEAD_DIM)
    return jax.nn.gelu(blk @ w1) @ w2


def to_blocks(rows):
    b, l = rows.shape[:2]
    nsb = -(-l // SEL_BLK)
    rows = jnp.pad(rows, ((0, 0), (0, nsb * SEL_BLK - l), (0, 0), (0, 0)))
    return rows.reshape(b, nsb, SEL_BLK, N_KV, HEAD_DIM).transpose(0, 3, 1, 2, 4)


def nsa_branches(q_nope, q_rope, gates, q_pos, kc, vc, ks_blk, vs_blk, kw, vw, kw_pos):
    scale = HEAD_DIM ** -0.5
    nb = kc.shape[1]
    nsb = ks_blk.shape[2]
    n_sel = min(TOP_N, nsb)
    s = jnp.einsum('bqghd,bngd->bghqn', q_nope, kc) * scale
    cmp_end = jnp.arange(nb) * CMP_BLK + (CMP_BLK - 1)
    p_cmp = masked_softmax(s, cmp_end[None, :] <= q_pos[:, None])
    o_cmp = jnp.einsum('bghqn,bngd->bqghd', p_cmp.astype(vc.dtype), vc)
    imp = jnp.pad(p_cmp.sum(axis=2), ((0, 0), (0, 0), (0, 0), (0, nsb - nb)))
    blk = jnp.arange(nsb)[None, :]
    cur = (q_pos // SEL_BLK)[:, None]
    forced = (blk == 0) | (blk == cur) | (blk == cur - 1)
    score = jnp.where(blk > cur, -jnp.inf, imp + jnp.where(forced, FORCE_BONUS, 0.0))
    top_s, idx = lax.top_k(score, n_sel)
    gather = jax.vmap(jax.vmap(lambda blocks, ix: blocks[ix]))
    lead = idx.shape[:3]
    k_g = gather(ks_blk, idx).reshape(lead + (n_sel * SEL_BLK, HEAD_DIM))
    v_g = gather(vs_blk, idx).reshape(lead + (n_sel * SEL_BLK, HEAD_DIM))
    k_pos = (idx[..., None] * SEL_BLK + jnp.arange(SEL_BLK)).reshape(lead + (n_sel * SEL_BLK,))
    sel_mask = (k_pos <= q_pos[:, None]) & jnp.repeat(jnp.isfinite(top_s), SEL_BLK, axis=-1)
    s = jnp.einsum('bqghd,bgqkd->bghqk', q_rope, k_g) * scale
    p_sel = masked_softmax(s, sel_mask[:, :, None])
    o_sel = jnp.einsum('bghqk,bgqkd->bqghd', p_sel.astype(v_g.dtype), v_g)
    diff = q_pos[:, None] - kw_pos[None, :]
    win_mask = (diff >= 0) & (diff < WINDOW) & (kw_pos >= 0)[None, :]
    s = jnp.einsum('bqghd,bkgd->bghqk', q_rope, kw) * scale
    p_win = masked_softmax(s, win_mask)
    o_win = jnp.einsum('bghqk,bkgd->bqghd', p_win.astype(vw.dtype), vw)
    return gates[..., 0:1] * o_cmp + gates[..., 1:2] * o_sel + gates[..., 2:3] * o_win


def nsa_prompt(h, w_in, qk_g, cmp_pos, cmp_w1, cmp_w2, w_out):
    b, t, _ = h.shape
    pos = jnp.arange(t)
    q_nope, q_rope, kc_r, vc_r, ks, vs, kw, vw, gates = nsa_project(h, pos, w_in, qk_g)
    kc = rms_norm(compress(kc_r, cmp_pos[0], cmp_w1[0], cmp_w2[0]), qk_g[1])
    vc = compress(vc_r, cmp_pos[1], cmp_w1[1], cmp_w2[1])
    ks_blk, vs_blk = to_blocks(ks), to_blocks(vs)
    pad = ((0, 0), (WINDOW, 0), (0, 0), (0, 0))
    kw_pad, vw_pad = jnp.pad(kw, pad), jnp.pad(vw, pad)

    def query_block(i):
        start = i * QBLK
        sl = lambda a, n: lax.dynamic_slice_in_dim(a, start, n, axis=1)
        q_pos = start + jnp.arange(QBLK)
        kw_pos = start - WINDOW + jnp.arange(QBLK + WINDOW)
        return nsa_branches(sl(q_nope, QBLK), sl(q_rope, QBLK), sl(gates, QBLK), q_pos, kc, vc,
                            ks_blk, vs_blk, sl(kw_pad, QBLK + WINDOW), sl(vw_pad, QBLK + WINDOW), kw_pos)

    o = lax.map(query_block, jnp.arange(t // QBLK))
    o = o.transpose(1, 0, 2, 3, 4, 5).reshape(b, t, Q_W)
    keep = min(WINDOW, t)
    return o @ w_out, (kc_r, vc_r, ks, vs, kw[:, -keep:], vw[:, -keep:])


def nsa_sample(h, k_cmp_pool, v_cmp_pool, k_sel_pool, v_sel_pool, k_win_buf, v_win_buf, page_table,
               w_in, qk_g, cmp_pos, cmp_w1, cmp_w2, w_out):
    b, t, _ = h.shape
    pos = PAST_LEN + jnp.arange(t)
    q_nope, q_rope, kc_r, vc_r, ks, vs, kw, vw, gates = nsa_project(h, pos, w_in, qk_g)
    gather_past = lambda pool: pool[page_table].reshape(b, -1, N_KV, HEAD_DIM)
    cat = lambda pool, new: jnp.concatenate([gather_past(pool).astype(new.dtype), new], axis=1)
    kc = rms_norm(compress(cat(k_cmp_pool, kc_r), cmp_pos[0], cmp_w1[0], cmp_w2[0]), qk_g[1])
    vc = compress(cat(v_cmp_pool, vc_r), cmp_pos[1], cmp_w1[1], cmp_w2[1])
    ks_blk, vs_blk = to_blocks(cat(k_sel_pool, ks)), to_blocks(cat(v_sel_pool, vs))
    w_buf = k_win_buf.shape[1]
    kw_all = jnp.concatenate([k_win_buf.astype(kw.dtype), kw], axis=1)
    vw_all = jnp.concatenate([v_win_buf.astype(vw.dtype), vw], axis=1)
    kw_pos = PAST_LEN - w_buf + jnp.arange(w_buf + t)
    o = nsa_branches(q_nope, q_rope, gates, pos, kc, vc, ks_blk, vs_blk, kw_all, vw_all, kw_pos)
    return o.reshape(b, t, Q_W) @ w_out, (kc_r, vc_r, ks, vs, kw_all[:, -w_buf:], vw_all[:, -w_buf:])


def macaron_layer(x, p, norm_g, w_gu, w_down, w_ple_proj, w_ple_gate, mixer):
    x = x + 0.5 * swiglu(rms_norm(x, norm_g[0]), w_gu[0], w_down[0])
    m, state = mixer(rms_norm(x, norm_g[1]))
    x = x + m
    x = x + 0.5 * swiglu(rms_norm(x, norm_g[2]), w_gu[1], w_down[1])
    x = x + jax.nn.sigmoid(rms_norm(x, norm_g[3]) @ w_ple_gate) * (p @ w_ple_proj)
    return x, state


def setup_inputs(seed: int = 0) -> dict:
    key = jax.random.key(seed)
    k = jax.random.split(key, 26)
    f32 = jnp.float32

    def nrm(kk, shape, scale=1.0):
        return scale * jax.random.normal(kk, shape, f32)

    n_pages = PAST_LEN // PAGE_SIZE
    n_used = DEC_BATCH * n_pages
    n_pool = n_used + max(1, n_used // 4)
    w_buf = min(WINDOW, PAST_LEN)
    pool = (N_NSA, n_pool, PAGE_SIZE, N_KV, HEAD_DIM)
    buf = (N_NSA, DEC_BATCH, w_buf, N_KV, HEAD_DIM)
    page_table = jax.random.permutation(k[9], n_pool)[:n_used].reshape(DEC_BATCH, n_pages).astype(jnp.int32)
    return {
        'x_prompt': nrm(k[0], (BATCH, SEQ, D_MODEL)),
        'x_sample': nrm(k[1], (DEC_BATCH, DEC_SEQ, D_MODEL)),
        'state_conv': nrm(k[2], (N_CONV, DEC_BATCH, CONV_W - 1, D_MODEL)),
        'cache_k_cmp': nrm(k[3], pool),
        'cache_v_cmp': nrm(k[4], pool),
        'cache_k_sel': nrm(k[5], pool),
        'cache_v_sel': nrm(k[6], pool),
        'cache_k_win': nrm(k[7], buf),
        'cache_v_win': nrm(k[8], buf),
        'page_table': page_table,
        'p_prompt': nrm(k[10], (DEPTH, BATCH, SEQ, PLE_DIM)),
        'p_sample': nrm(k[11], (DEPTH, DEC_BATCH, DEC_SEQ, PLE_DIM)),
        'norm_g': 1.0 + nrm(k[12], (DEPTH, 4, D_MODEL), 0.05),
        'ffn_w_gu': nrm(k[13], (DEPTH, 2, D_MODEL, 2 * D_FF), D_MODEL ** -0.5),
        'ffn_w_down': nrm(k[14], (DEPTH, 2, D_FF, D_MODEL), D_FF ** -0.5),
        'ple_w_proj': nrm(k[15], (DEPTH, PLE_DIM, D_MODEL), PLE_DIM ** -0.5),
        'ple_w_gate': nrm(k[16], (DEPTH, D_MODEL, D_MODEL), D_MODEL ** -0.5),
        'conv_w_in': nrm(k[17], (N_CONV, D_MODEL, 3 * D_MODEL), D_MODEL ** -0.5),
        'conv_w': nrm(k[18], (N_CONV, CONV_W, D_MODEL), CONV_W ** -0.5),
        'conv_w_out': nrm(k[19], (N_CONV, D_MODEL, D_MODEL), D_MODEL ** -0.5),
        'nsa_w_in': nrm(k[20], (N_NSA, D_MODEL, Q_W + 6 * KV_W + 3 * N_HEADS), D_MODEL ** -0.5),
        'nsa_qk_g': 1.0 + nrm(k[21], (N_NSA, 4, HEAD_DIM), 0.05),
        'nsa_cmp_pos': nrm(k[22], (N_NSA, 2, CMP_BLK, HEAD_DIM), 0.1),
        'nsa_cmp_w1': nrm(k[23], (N_NSA, 2, CMP_BLK * HEAD_DIM, CMP_HID), (CMP_BLK * HEAD_DIM) ** -0.5),
        'nsa_cmp_w2': nrm(k[24], (N_NSA, 2, CMP_HID, HEAD_DIM), CMP_HID ** -0.5),
        'nsa_w_out': nrm(k[25], (N_NSA, Q_W, D_MODEL), Q_W ** -0.5),
    }


def reference(x_prompt, x_sample, state_conv, cache_k_cmp, cache_v_cmp, cache_k_sel, cache_v_sel,
              cache_k_win, cache_v_win, page_table, p_prompt, p_sample, norm_g, ffn_w_gu, ffn_w_down,
              ple_w_proj, ple_w_gate, conv_w_in, conv_w, conv_w_out, nsa_w_in, nsa_qk_g, nsa_cmp_pos,
              nsa_cmp_w1, nsa_cmp_w2, nsa_w_out):
    conv_p_list, conv_s_list, nsa_p_list, nsa_s_list = [], [], [], []
    for i in range(DEPTH):
        j = i // 2
        lw = (norm_g[i], ffn_w_gu[i], ffn_w_down[i], ple_w_proj[i], ple_w_gate[i])
        if i % 2 == 0:
            cw = (conv_w_in[j], conv_w[j], conv_w_out[j])
            zero_prev = jnp.zeros((x_prompt.shape[0], CONV_W - 1, D_MODEL), x_prompt.dtype)
            x_prompt, st_p = macaron_layer(x_prompt, p_prompt[i], *lw,
                                           lambda h: conv_mixer(h, zero_prev, *cw))
            x_sample, st_s = macaron_layer(x_sample, p_sample[i], *lw,
                                           lambda h: conv_mixer(h, state_conv[j], *cw))
            conv_p_list.append(st_p)
            conv_s_list.append(st_s)
        else:
            nw = (nsa_w_in[j], nsa_qk_g[j], nsa_cmp_pos[j], nsa_cmp_w1[j], nsa_cmp_w2[j], nsa_w_out[j])
            x_prompt, st_p = macaron_layer(x_prompt, p_prompt[i], *lw, lambda h: nsa_prompt(h, *nw))
            x_sample, st_s = macaron_layer(
                x_sample, p_sample[i], *lw,
                lambda h: nsa_sample(h, cache_k_cmp[j], cache_v_cmp[j], cache_k_sel[j], cache_v_sel[j],
                                     cache_k_win[j], cache_v_win[j], page_table, *nw))
            nsa_p_list.append(st_p)
            nsa_s_list.append(st_s)
    conv_p = jnp.stack(conv_p_list)
    conv_s = jnp.stack(conv_s_list)
    k_cmp_p, v_cmp_p, k_sel_p, v_sel_p, k_win_p, v_win_p = [jnp.stack(a) for a in zip(*nsa_p_list)]
    k_cmp_s, v_cmp_s, k_sel_s, v_sel_s, k_win_s, v_win_s = [jnp.stack(a) for a in zip(*nsa_s_list)]
    return (x_prompt, x_sample, conv_p, conv_s, k_cmp_p, v_cmp_p, k_sel_p, v_sel_p, k_win_p, v_win_p,
            k_cmp_s, v_cmp_s, k_sel_s, v_sel_s, k_win_s, v_win_s)
```

```python
import functools

import jax
import jax.numpy as jnp
from jax import lax
from jax.experimental import pallas as pl
from jax.experimental.pallas import tpu as pltpu

F32 = jnp.float32
BF16 = jnp.bfloat16

HEAD_DIM = 64
N_KV = 4
HPG = 4
CMP_BLK = 64
SEL_BLK = 64
TOP_N = 16
N_FORCED = 3
WINDOW = 512
PAGE_SIZE = 128
CONV_W = 3
QBLK = 128
KTILE = 512
ROPE_THETA = 10000.0
RMS_EPS = 1e-6
NEG_INF = float("-inf")
KV_W = N_KV * HEAD_DIM
LANES = 128
MIB = 1024 * 1024


def _cparams(n_grid, vmem_mib):
    return pltpu.CompilerParams(
        dimension_semantics=("arbitrary",) * n_grid,
        vmem_limit_bytes=vmem_mib * MIB,
    )


def _rms(x, g):
    return x * lax.rsqrt(jnp.mean(x * x, axis=-1, keepdims=True) + RMS_EPS) * g


def _dot(a, b):
    return jnp.dot(a, b, preferred_element_type=F32)


def _ffn_body(x_ref, g_ref, wg_ref, wu_ref, wd_ref, o_ref, h_ref, acc_ref, *, n_f):
    j = pl.program_id(1)

    @pl.when(j == 0)
    def _():
        h_ref[...] = _rms(x_ref[...], g_ref[...]).astype(BF16)
        acc_ref[...] = jnp.zeros_like(acc_ref)

    h = h_ref[...]
    gate = _dot(h, wg_ref[...])
    up = _dot(h, wu_ref[...])
    act = (jax.nn.silu(gate) * up).astype(BF16)
    acc_ref[...] += _dot(act, wd_ref[...])

    @pl.when(j == n_f - 1)
    def _():
        o_ref[...] = x_ref[...] + 0.5 * acc_ref[...]


def _ffn(x, g, w_gu, w_down):
    m, d = x.shape
    f = w_down.shape[0]
    tm = min(m, 512)
    tf = f // 2
    n_f = f // tf
    return pl.pallas_call(
        functools.partial(_ffn_body, n_f=n_f),
        grid=(m // tm, n_f),
        in_specs=[
            pl.BlockSpec((tm, d), lambda i, j: (i, 0)),
            pl.BlockSpec((1, d), lambda i, j: (0, 0)),
            pl.BlockSpec((d, tf), lambda i, j: (0, j)),
            pl.BlockSpec((d, tf), lambda i, j: (0, n_f + j)),
            pl.BlockSpec((tf, d), lambda i, j: (j, 0)),
        ],
        out_specs=pl.BlockSpec((tm, d), lambda i, j: (i, 0)),
        out_shape=jax.ShapeDtypeStruct((m, d), F32),
        scratch_shapes=[pltpu.VMEM((tm, d), BF16), pltpu.VMEM((tm, d), F32)],
        compiler_params=_cparams(2, 48),
        name="ffn",
    )(x, g.reshape(1, d), w_gu, w_gu, w_down)


def _ple_body(x_ref, g_ref, p_ref, wg_ref, wp_ref, o_ref):
    x = x_ref[...]
    h = _rms(x, g_ref[...]).astype(BF16)
    gate = jax.nn.sigmoid(_dot(h, wg_ref[...]))
    o_ref[...] = x + gate * _dot(p_ref[...].astype(BF16), wp_ref[...])


def _ple(x, g, p, w_gate, w_proj):
    m, d = x.shape
    pd = p.shape[1]
    tm = min(m, 512)
    return pl.pallas_call(
        _ple_body,
        grid=(m // tm,),
        in_specs=[
            pl.BlockSpec((tm, d), lambda i: (i, 0)),
            pl.BlockSpec((1, d), lambda i: (0, 0)),
            pl.BlockSpec((tm, pd), lambda i: (i, 0)),
            pl.BlockSpec((d, d), lambda i: (0, 0)),
            pl.BlockSpec((pd, d), lambda i: (0, 0)),
        ],
        out_specs=pl.BlockSpec((tm, d), lambda i: (i, 0)),
        out_shape=jax.ShapeDtypeStruct((m, d), F32),
        compiler_params=_cparams(1, 32),
        name="ple",
    )(x, g.reshape(1, d), p, w_gate, w_proj)


def _proj_res_body(x_ref, o_ref, w_ref, y_ref):
    y_ref[...] = x_ref[...] + _dot(o_ref[...], w_ref[...])


def _proj_res(x, o, w):
    m, d = x.shape
    k = o.shape[1]
    tm = min(m, 512)
    return pl.pallas_call(
        _proj_res_body,
        grid=(m // tm,),
        in_specs=[
            pl.BlockSpec((tm, d), lambda i: (i, 0)),
            pl.BlockSpec((tm, k), lambda i: (i, 0)),
            pl.BlockSpec((k, d), lambda i: (0, 0)),
        ],
        out_specs=pl.BlockSpec((tm, d), lambda i: (i, 0)),
        out_shape=jax.ShapeDtypeStruct((m, d), F32),
        compiler_params=_cparams(1, 32),
        name="proj_res",
    )(x, o, w)


def _conv_body(x_ref, prev_ref, g_ref, win_ref, wc_ref, wout_ref, y_ref, st_ref, u_ref,
               *, shift, halo, tm, d):
    t = pl.program_id(1)
    lo = halo - 2 * shift

    @pl.when(t == 0)
    def _():
        u_ref[lo:halo, :] = prev_ref[0]

    x = x_ref[0]
    h = _rms(x, g_ref[...]).astype(BF16)
    proj = _dot(h, win_ref[...])
    b_gate = proj[:, :d]
    u = proj[:, d:2 * d] * proj[:, 2 * d:]
    u_ref[halo:halo + tm, :] = u
    wc = wc_ref[...]
    y = (wc[0:1, :] * u_ref[lo:lo + tm, :]
         + wc[1:2, :] * u_ref[lo + shift:lo + shift + tm, :]
         + wc[2:3, :] * u)
    y_ref[0] = x + _dot((b_gate * y).astype(BF16), wout_ref[...])
    tail = u_ref[lo + tm:halo + tm, :]
    st_ref[0] = tail
    u_ref[lo:halo, :] = tail


def _conv_mixer(x, prev, g, w_in, w_conv, w_out, shift):
    nb, t, d = x.shape
    tm = min(t, 512)
    halo = -(-2 * shift // 8) * 8
    return pl.pallas_call(
        functools.partial(_conv_body, shift=shift, halo=halo, tm=tm, d=d),
        grid=(nb, t // tm),
        in_specs=[
            pl.BlockSpec((1, tm, d), lambda b, i: (b, i, 0)),
            pl.BlockSpec((1, 2 * shift, d), lambda b, i: (b, 0, 0)),
            pl.BlockSpec((1, d), lambda b, i: (0, 0)),
            pl.BlockSpec((d, 3 * d), lambda b, i: (0, 0)),
            pl.BlockSpec((CONV_W, d), lambda b, i: (0, 0)),
            pl.BlockSpec((d, d), lambda b, i: (0, 0)),
        ],
        out_specs=[
            pl.BlockSpec((1, tm, d), lambda b, i: (b, i, 0)),
            pl.BlockSpec((1, 2 * shift, d), lambda b, i: (b, 0, 0)),
        ],
        out_shape=[
            jax.ShapeDtypeStruct((nb, t, d), F32),
            jax.ShapeDtypeStruct((nb, 2 * shift, d), F32),
        ],
        scratch_shapes=[pltpu.VMEM((halo + tm, d), F32)],
        compiler_params=_cparams(2, 48),
        name="conv_mixer",
    )(x, prev, g.reshape(1, d), w_in, w_conv, w_out)


def _head_norm(xc, gain, gg):
    sq = xc * xc
    hi = sq.astype(BF16)
    lo = (sq - hi.astype(F32)).astype(BF16)
    ss = _dot(jnp.concatenate([hi, lo], axis=1), gg)
    return xc * lax.rsqrt(ss * (1.0 / HEAD_DIM) + RMS_EPS) * gain


def _nsa_proj_body(x_ref, g_ref, w_ref, qkg_ref, cos_ref, sin_ref, gg_ref,
                   qn_ref, qr_ref, kc_ref, vc_ref, ks_ref, vs_ref, kw_ref, vw_ref, gt_ref,
                   *, tm, d):
    h = _rms(x_ref[...], g_ref[...]).astype(BF16)
    proj = _dot(h, w_ref[...])
    cos = cos_ref[...]
    sin = sin_ref[...]
    gg = gg_ref[...]
    lane = lax.broadcasted_iota(jnp.int32, (tm, LANES), 1)
    first_half = (lane % HEAD_DIM) < (HEAD_DIM // 2)
    scale = HEAD_DIM ** -0.5

    def rope(xc):
        swapped = jnp.where(first_half, pltpu.roll(xc, LANES - HEAD_DIM // 2, 1),
                            pltpu.roll(xc, HEAD_DIM // 2, 1))
        return xc * cos + swapped * sin

    def slab(c):
        return proj[:, c * LANES:(c + 1) * LANES]

    n_q = d // LANES
    for c in range(n_q):
        qc = _head_norm(slab(c), qkg_ref[0:1, :], gg)
        qn_ref[:, c * LANES:(c + 1) * LANES] = (qc * scale).astype(BF16)
        qr_ref[:, c * LANES:(c + 1) * LANES] = (rope(qc) * scale).astype(BF16)
    for c in range(2):
        kc_ref[:, c * LANES:(c + 1) * LANES] = slab(n_q + c)
        vc_ref[:, c * LANES:(c + 1) * LANES] = slab(n_q + 2 + c)
        ks_ref[:, c * LANES:(c + 1) * LANES] = rope(_head_norm(slab(n_q + 4 + c), qkg_ref[2:3, :], gg))
        vs_ref[:, c * LANES:(c + 1) * LANES] = slab(n_q + 6 + c)
        kw_ref[:, c * LANES:(c + 1) * LANES] = rope(_head_norm(slab(n_q + 8 + c), qkg_ref[3:4, :], gg))
        vw_ref[:, c * LANES:(c + 1) * LANES] = slab(n_q + 10 + c)
    gt_ref[...] = jax.nn.sigmoid(slab(n_q + 12))


def _nsa_proj(x, g, w_in_pad, qkg128, cos128, sin128, gg):
    m, d = x.shape
    n = w_in_pad.shape[1]
    tm = min(m, 512)
    row = lambda w: pl.BlockSpec((tm, w), lambda i: (i, 0))
    full = lambda a: pl.BlockSpec(a.shape, lambda i: (0,) * a.ndim)
    return pl.pallas_call(
        functools.partial(_nsa_proj_body, tm=tm, d=d),
        grid=(m // tm,),
        in_specs=[row(d), pl.BlockSpec((1, d), lambda i: (0, 0)), pl.BlockSpec((d, n), lambda i: (0, 0)),
                  full(qkg128), row(LANES), row(LANES), full(gg)],
        out_specs=[row(d), row(d)] + [row(KV_W)] * 6 + [row(LANES)],
        out_shape=[jax.ShapeDtypeStruct((m, d), BF16)] * 2
        + [jax.ShapeDtypeStruct((m, KV_W), F32)] * 6
        + [jax.ShapeDtypeStruct((m, LANES), F32)],
        compiler_params=_cparams(1, 48),
        name="nsa_proj",
    )(x, g.reshape(1, d), w_in_pad, qkg128, cos128, sin128, gg)


def _page_copies(pool_ref, buf_ref, sem, slot, page, p, split):
    rows = pl.ds(p * PAGE_SIZE, PAGE_SIZE)
    if not split:
        return [pltpu.make_async_copy(pool_ref.at[page], buf_ref.at[slot, rows], sem.at[slot])]
    return [pltpu.make_async_copy(pool_ref.at[page, :, pl.ds(hh * LANES, LANES)],
                                  buf_ref.at[slot, hh, rows], sem.at[slot])
            for hh in range(KV_W // LANES)]


def _gather_start(pt_ref, pool_ref, buf_ref, sem, b, slot, n_pages, split=False):
    for p in range(n_pages):
        for copy in _page_copies(pool_ref, buf_ref, sem, slot, pt_ref[b, p], p, split):
            copy.start()


def _gather_wait(pool_ref, buf_ref, sem, slot, n_pages, split=False):
    for p in range(n_pages):
        for copy in _page_copies(pool_ref, buf_ref, sem, slot, 0, p, split):
            copy.wait()


def _compress_body(pt_ref, pool_ref, pos_ref, bd1_ref, bd2_ref, gain_ref, gg_ref, o_ref,
                   buf_ref, sem, acc_ref, *, n_pages, normalize):
    b = pl.program_id(0)
    nb = pl.num_programs(0)
    slot = b % 2
    n_blk = n_pages * PAGE_SIZE // CMP_BLK
    n_half = KV_W // LANES
    unroll = 8

    @pl.when(b == 0)
    def _():
        _gather_start(pt_ref, pool_ref, buf_ref, sem, 0, 0, n_pages, split=True)

    @pl.when(b + 1 < nb)
    def _():
        _gather_start(pt_ref, pool_ref, buf_ref, sem, b + 1, 1 - slot, n_pages, split=True)

    _gather_wait(pool_ref, buf_ref, sem, slot, n_pages, split=True)
    acc_ref[...] = jnp.zeros_like(acc_ref)

    def chunk(c, carry):
        accs = [acc_ref[hh] for hh in range(n_half)]
        for rr in range(unroll):
            r = c * unroll + rr
            w = bd1_ref[r]
            pos = pos_ref[pl.ds(r, 1), :]
            for hh in range(n_half):
                rows = buf_ref[slot, hh, pl.ds(r, n_blk, stride=CMP_BLK), :] + pos
                accs[hh] = accs[hh] + _dot(rows.astype(BF16), w)
        for hh in range(n_half):
            acc_ref[hh] = accs[hh]
        return carry

    lax.fori_loop(0, CMP_BLK // unroll, chunk, 0)

    for hh in range(n_half):
        y = _dot(jax.nn.gelu(acc_ref[hh]).astype(BF16), bd2_ref[...])
        if normalize:
            y = _head_norm(y, gain_ref[...], gg_ref[...])
        o_ref[0, :, hh * LANES:(hh + 1) * LANES] = y.astype(BF16)


def _compress(page_table, pool, pos2, bd1, bd2, gain128, gg, normalize):
    nb, n_pages = page_table.shape
    n_blk = n_pages * PAGE_SIZE // CMP_BLK
    n_half = KV_W // LANES
    hid2 = bd1.shape[2]
    full = lambda a: pl.BlockSpec(a.shape, lambda b, pt: (0,) * a.ndim)
    grid_spec = pltpu.PrefetchScalarGridSpec(
        num_scalar_prefetch=1,
        grid=(nb,),
        in_specs=[pl.BlockSpec(memory_space=pl.ANY), full(pos2), full(bd1), full(bd2), full(gain128), full(gg)],
        out_specs=pl.BlockSpec((1, n_blk, KV_W), lambda b, pt: (b, 0, 0)),
        scratch_shapes=[
            pltpu.VMEM((2, n_half, n_pages * PAGE_SIZE, LANES), F32),
            pltpu.SemaphoreType.DMA((2,)),
            pltpu.VMEM((n_half, n_blk, hid2), F32),
        ],
    )
    return pl.pallas_call(
        functools.partial(_compress_body, n_pages=n_pages, normalize=normalize),
        grid_spec=grid_spec,
        out_shape=jax.ShapeDtypeStruct((nb, n_blk, KV_W), BF16),
        compiler_params=_cparams(1, 40),
        name="compress",
    )(page_table, pool, pos2, bd1, bd2, gain128, gg)


def _softmax_cols(s):
    m = jnp.max(s, axis=0, keepdims=True)
    m = jnp.where(m == NEG_INF, 0.0, m)
    e = jnp.exp(s - m)
    return e, jnp.sum(e, axis=0, keepdims=True)


def _select_bias(imp, cur):
    jj = lax.broadcasted_iota(jnp.int32, imp.shape, 0)
    forced = (jj == 0) | (jj == cur) | (jj == cur - 1)
    cand = (jj <= cur) & jnp.logical_not(forced)
    v = jnp.where(cand, imp, -1.0)
    bias = jnp.where(forced, 0.0, NEG_INF)
    for _ in range(TOP_N - N_FORCED):
        m = jnp.max(v, axis=0, keepdims=True)
        idx = jnp.min(jnp.where(v == m, jj, imp.shape[0]), axis=0, keepdims=True)
        idx = jnp.where(m >= 0.0, idx, -1)
        pick = jj == idx
        bias = jnp.where(pick, 0.0, bias)
        v = jnp.where(pick, -1.0, v)
    return bias


def _block_bias(selb_ref, first_block, n_blocks, reps):
    rows = [jnp.broadcast_to(selb_ref[pl.ds(first_block + r, 1), :], (SEL_BLK, LANES))
            for r in range(n_blocks)]
    bias = jnp.concatenate(rows, axis=0)
    return jnp.concatenate([bias] * reps, axis=1) if reps > 1 else bias


def _attn_prompt_body(qn_ref, qr_ref, gt_ref, kc_ref, vct_ref, ks_ref, vst_ref, kw_ref, vwt_ref,
                      o_ref, selb_ref):
    i = pl.program_id(2)
    q_start = i * QBLK
    n_col = HPG * QBLK
    qn = qn_ref[...]
    qr = qr_ref[...]
    q_pos = q_start + lax.broadcasted_iota(jnp.int32, (1, n_col), 1) % QBLK

    n_cmp = kc_ref.shape[0]
    s = _dot(kc_ref[...], qn)
    cmp_end = lax.broadcasted_iota(jnp.int32, (n_cmp, 1), 0) * CMP_BLK + (CMP_BLK - 1)
    s = jnp.where(cmp_end <= q_pos, s, NEG_INF)
    e, l = _softmax_cols(s)
    p_cmp = e / jnp.maximum(l, 1e-30)
    o_cmp = _dot(vct_ref[...], p_cmp.astype(BF16))

    imp = p_cmp[:, 0:QBLK]
    for hh in range(1, HPG):
        imp = imp + p_cmp[:, hh * QBLK:(hh + 1) * QBLK]
    selb_ref[...] = _select_bias(imp, q_pos[:, 0:QBLK] // SEL_BLK)

    blocks_per_tile = KTILE // SEL_BLK

    def sel_tile(t, carry, diagonal):
        m, l, acc = carry
        k0 = pl.multiple_of(t * KTILE, KTILE)
        s = _dot(ks_ref[pl.ds(k0, KTILE), :], qr)
        s = s + _block_bias(selb_ref, t * blocks_per_tile, blocks_per_tile, HPG)
        if diagonal:
            k_pos = k0 + lax.broadcasted_iota(jnp.int32, (KTILE, 1), 0)
            s = jnp.where(k_pos <= q_pos, s, NEG_INF)
        m_new = jnp.maximum(m, jnp.max(s, axis=0, keepdims=True))
        m_safe = jnp.where(m_new == NEG_INF, 0.0, m_new)
        alpha = jnp.exp(m - m_safe)
        p = jnp.exp(s - m_safe)
        l = alpha * l + jnp.sum(p, axis=0, keepdims=True)
        acc = alpha * acc + _dot(vst_ref[:, pl.ds(k0, KTILE)], p.astype(BF16))
        return m_new, l, acc

    last = (q_start + QBLK - 1) // KTILE
    init = (jnp.full((1, n_col), NEG_INF, F32), jnp.zeros((1, n_col), F32),
            jnp.zeros((HEAD_DIM, n_col), F32))
    carry = lax.fori_loop(0, last, lambda t, c: sel_tile(t, c, False), init)
    _, l_sel, acc_sel = sel_tile(last, carry, True)
    o_sel = acc_sel / jnp.maximum(l_sel, 1e-30)

    n_win = WINDOW + QBLK
    w0 = pl.multiple_of(q_start, QBLK)
    s = _dot(kw_ref[pl.ds(w0, n_win), :], qr)
    r = lax.broadcasted_iota(jnp.int32, (n_win, QBLK), 0)
    q = lax.broadcasted_iota(jnp.int32, (n_win, QBLK), 1)
    valid = (r > q) & (r <= q + WINDOW) & (r + q_start >= WINDOW)
    wbias = jnp.where(valid, 0.0, NEG_INF)
    s = s + jnp.concatenate([wbias] * HPG, axis=1)
    e, l = _softmax_cols(s)
    o_win = _dot(vwt_ref[:, pl.ds(w0, n_win)], e.astype(BF16)) / jnp.maximum(l, 1e-30)

    gt = gt_ref[...]
    o_ref[...] = (gt[0:1, :] * o_cmp + gt[1:2, :] * o_sel + gt[2:3, :] * o_win).astype(BF16)


def _attn_prompt(qn_t, qr_t, gt_t, kc, vc_t, ks, vs_t, kw_pad, vw_pad_t):
    nb, ng, nqb, hd, n_col = qn_t.shape
    t = ks.shape[2]
    n_cmp = kc.shape[2]
    per_q = lambda r: pl.BlockSpec((None, None, None, r, n_col), lambda b, g, i: (b, g, i, 0, 0))
    per_bg = lambda a: pl.BlockSpec((None, None) + a.shape[2:], lambda b, g, i: (b, g, 0, 0))
    return pl.pallas_call(
        _attn_prompt_body,
        grid=(nb, ng, nqb),
        in_specs=[per_q(hd), per_q(hd), per_q(8), per_bg(kc), per_bg(vc_t), per_bg(ks), per_bg(vs_t),
                  per_bg(kw_pad), per_bg(vw_pad_t)],
        out_specs=per_q(hd),
        out_shape=jax.ShapeDtypeStruct((nb, ng, nqb, hd, n_col), BF16),
        scratch_shapes=[pltpu.VMEM((n_cmp, QBLK), F32)],
        compiler_params=_cparams(3, 48),
        name="attn_prompt",
    )(qn_t, qr_t, gt_t, kc, vc_t, ks, vs_t, kw_pad, vw_pad_t)


def _attn_sample_body(pt_ref, qn_ref, qr_ref, gt_ref, kc_ref, vc_ref, kwin_ref, vwin_ref,
                      ksn_ref, vsn_ref, kwn_ref, vwn_ref, kpool_ref, vpool_ref, o_ref,
                      kbuf_ref, vbuf_ref, ksem, vsem, s_ref, selb_ref,
                      *, n_pages, n_new, past_len):
    b = pl.program_id(0)
    nb = pl.num_programs(0)
    slot = b % 2

    @pl.when(b == 0)
    def _():
        _gather_start(pt_ref, kpool_ref, kbuf_ref, ksem, 0, 0, n_pages)
        _gather_start(pt_ref, vpool_ref, vbuf_ref, vsem, 0, 0, n_pages)

    @pl.when(b + 1 < nb)
    def _():
        _gather_start(pt_ref, kpool_ref, kbuf_ref, ksem, b + 1, 1 - slot, n_pages)
        _gather_start(pt_ref, vpool_ref, vbuf_ref, vsem, b + 1, 1 - slot, n_pages)

    qn = qn_ref[0]
    qr = qr_ref[0]
    n_past = n_pages * PAGE_SIZE
    n_blk = n_past // SEL_BLK
    col = lax.broadcasted_iota(jnp.int32, (1, LANES), 1)
    q_idx = col % n_new
    q_pos = past_len + q_idx
    n_pad = ksn_ref.shape[1]
    new_row = lax.broadcasted_iota(jnp.int32, (n_pad, 1), 0)
    new_ok = (new_row <= q_idx) & (new_row < n_new)

    def weighted(p, v):
        return _dot(p.T.astype(BF16), v.astype(BF16))

    s = _dot(kc_ref[0], qn)
    cmp_end = lax.broadcasted_iota(jnp.int32, (n_blk, 1), 0) * CMP_BLK + (CMP_BLK - 1)
    s = jnp.where(cmp_end <= q_pos, s, NEG_INF)
    e, l = _softmax_cols(s)
    p_cmp = e / jnp.maximum(l, 1e-30)
    o_cmp = weighted(p_cmp, vc_ref[0])

    n_gq = N_KV * n_new
    imp = p_cmp
    for hh in range(1, HPG):
        imp = imp + pltpu.roll(p_cmp, LANES - hh * n_gq, 1)
    sel = _select_bias(imp, q_pos // SEL_BLK)
    sel01 = jnp.where((col < n_gq) & (sel == 0.0), 1.0, 0.0)
    spread = sel01
    for hh in range(1, HPG):
        spread = spread + pltpu.roll(sel01, hh * n_gq, 1)
    selb_ref[...] = jnp.where(spread > 0.5, 0.0, NEG_INF)

    _gather_wait(kpool_ref, kbuf_ref, ksem, slot, n_pages)
    _gather_wait(vpool_ref, vbuf_ref, vsem, slot, n_pages)
    chunk = 1024
    n_chunks = n_past // chunk
    blocks_per_chunk = chunk // SEL_BLK

    def score_chunk(c, m):
        k0 = pl.multiple_of(c * chunk, chunk)
        s = _dot(kbuf_ref[slot, pl.ds(k0, chunk), :].astype(BF16), qr)
        s = s + _block_bias(selb_ref, c * blocks_per_chunk, blocks_per_chunk, 1)
        s_ref[pl.ds(k0, chunk), :] = s
        return jnp.maximum(m, jnp.max(s, axis=0, keepdims=True))

    m = lax.fori_loop(0, n_chunks, score_chunk, jnp.full((1, LANES), NEG_INF, F32))
    s_new = jnp.where(new_ok, _dot(ksn_ref[0].astype(BF16), qr), NEG_INF)
    m = jnp.maximum(m, jnp.max(s_new, axis=0, keepdims=True))
    m = jnp.where(m == NEG_INF, 0.0, m)

    def value_chunk(c, carry):
        l, acc = carry
        k0 = pl.multiple_of(c * chunk, chunk)
        p = jnp.exp(s_ref[pl.ds(k0, chunk), :] - m)
        l = l + jnp.sum(p, axis=0, keepdims=True)
        return l, acc + weighted(p, vbuf_ref[slot, pl.ds(k0, chunk), :])

    p_new = jnp.exp(s_new - m)
    l_sel, acc = lax.fori_loop(
        0, n_chunks, value_chunk,
        (jnp.sum(p_new, axis=0, keepdims=True), weighted(p_new, vsn_ref[0])))
    o_sel = acc / jnp.maximum(l_sel, 1e-30).T

    n_buf = kwin_ref.shape[1]
    s_buf = _dot(kwin_ref[0].astype(BF16), qr)
    buf_pos = past_len - n_buf + lax.broadcasted_iota(jnp.int32, (n_buf, 1), 0)
    diff = q_pos - buf_pos
    s_buf = jnp.where((diff >= 0) & (diff < WINDOW) & (buf_pos >= 0), s_buf, NEG_INF)
    s_nw = jnp.where(new_ok, _dot(kwn_ref[0].astype(BF16), qr), NEG_INF)
    m = jnp.maximum(jnp.max(s_buf, axis=0, keepdims=True), jnp.max(s_nw, axis=0, keepdims=True))
    m = jnp.where(m == NEG_INF, 0.0, m)
    p_buf = jnp.exp(s_buf - m)
    p_nw = jnp.exp(s_nw - m)
    l_win = jnp.sum(p_buf, axis=0, keepdims=True) + jnp.sum(p_nw, axis=0, keepdims=True)
    o_win = (weighted(p_buf, vwin_ref[0]) + weighted(p_nw, vwn_ref[0])) / jnp.maximum(l_win, 1e-30).T

    gt = gt_ref[0]
    o = gt[:, 0:1] * o_cmp + gt[:, 1:2] * o_sel + gt[:, 2:3] * o_win
    row_g = (lax.broadcasted_iota(jnp.int32, (LANES, KV_W), 0) // n_new) % N_KV
    lane_g = lax.broadcasted_iota(jnp.int32, (LANES, KV_W), 1) // HEAD_DIM
    o = jnp.where(row_g == lane_g, o, 0.0)
    z = o[:, :LANES] + o[:, LANES:]
    o_ref[0] = z + pltpu.roll(z, HEAD_DIM, 1)


def _attn_sample(page_table, qn_bd, qr_bd, gt_col, kc, vc, kwin, vwin, ks_new, vs_new, kw_new, vw_new,
                 k_pool, v_pool, n_new, past_len):
    nb, n_pages = page_table.shape
    n_past = n_pages * PAGE_SIZE
    per_b = lambda a: pl.BlockSpec((1,) + a.shape[1:], lambda b, pt: (b,) + (0,) * (a.ndim - 1))
    any_spec = pl.BlockSpec(memory_space=pl.ANY)
    ins = (qn_bd, qr_bd, gt_col, kc, vc, kwin, vwin, ks_new, vs_new, kw_new, vw_new)
    grid_spec = pltpu.PrefetchScalarGridSpec(
        num_scalar_prefetch=1,
        grid=(nb,),
        in_specs=[per_b(a) for a in ins] + [any_spec, any_spec],
        out_specs=pl.BlockSpec((1, LANES, LANES), lambda b, pt: (b, 0, 0)),
        scratch_shapes=[
            pltpu.VMEM((2, n_past, KV_W), F32),
            pltpu.VMEM((2, n_past, KV_W), F32),
            pltpu.SemaphoreType.DMA((2,)),
            pltpu.SemaphoreType.DMA((2,)),
            pltpu.VMEM((n_past, LANES), F32),
            pltpu.VMEM((n_past // SEL_BLK, LANES), F32),
        ],
    )
    return pl.pallas_call(
        functools.partial(_attn_sample_body, n_pages=n_pages, n_new=n_new, past_len=past_len),
        grid_spec=grid_spec,
        out_shape=jax.ShapeDtypeStruct((nb, LANES, LANES), F32),
        compiler_params=_cparams(1, 56),
        name="attn_sample",
    )(page_table, *ins, k_pool, v_pool)


def _rope_tables(pos):
    half = HEAD_DIM // 2
    inv = ROPE_THETA ** (-jnp.arange(half, dtype=F32) / half)
    ang = pos.astype(F32)[:, None] * inv
    cos, sin = jnp.cos(ang), jnp.sin(ang)
    reps = LANES // HEAD_DIM
    return (jnp.tile(jnp.concatenate([cos, cos], axis=1), (1, reps)),
            jnp.tile(jnp.concatenate([-sin, sin], axis=1), (1, reps)))


def _group_sum_matrix():
    lane = jnp.arange(LANES)
    g = (lane[:, None] // HEAD_DIM == lane[None, :] // HEAD_DIM).astype(BF16)
    return jnp.concatenate([g, g], axis=0)


def _compress_weights(cmp_pos, cmp_w1, cmp_w2):
    n_pair = LANES // HEAD_DIM
    eye = jnp.eye(n_pair, dtype=F32)
    hid = cmp_w1.shape[1]
    w1 = cmp_w1.reshape(CMP_BLK, HEAD_DIM, hid)
    bd1 = jnp.einsum("gk,rdh->rgdkh", eye, w1).reshape(CMP_BLK, LANES, n_pair * hid).astype(BF16)
    bd2 = jnp.einsum("gk,hd->ghkd", eye, cmp_w2).reshape(n_pair * hid, LANES).astype(BF16)
    return jnp.tile(cmp_pos, (1, n_pair)), bd1, bd2


def _layer_tail(x, p, norm_g, w_gu, w_down, w_ple_proj, w_ple_gate):
    x = _ffn(x, norm_g[2], w_gu[1], w_down[1])
    return _ple(x, norm_g[3], p, w_ple_gate, w_ple_proj)


def kernel(x_prompt, x_sample, state_conv, cache_k_cmp, cache_v_cmp, cache_k_sel, cache_v_sel,
           cache_k_win, cache_v_win, page_table, p_prompt, p_sample, norm_g, ffn_w_gu, ffn_w_down,
           ple_w_proj, ple_w_gate, conv_w_in, conv_w, conv_w_out, nsa_w_in, nsa_qk_g, nsa_cmp_pos,
           nsa_cmp_w1, nsa_cmp_w2, nsa_w_out):
    nb_p, t_p, d = x_prompt.shape
    nb_s, t_s, _ = x_sample.shape
    ple_dim = p_prompt.shape[-1]
    past_len = page_table.shape[1] * PAGE_SIZE
    assert t_p % KTILE == 0 and t_p % PAGE_SIZE == 0 and d % LANES == 0

    w_gu = ffn_w_gu.astype(BF16)
    w_down = ffn_w_down.astype(BF16)
    w_pp = ple_w_proj.astype(BF16)
    w_pg = ple_w_gate.astype(BF16)

    xp = x_prompt.reshape(nb_p * t_p, d)
    xs = x_sample.reshape(nb_s * t_s, d)
    pp = p_prompt.reshape(-1, nb_p * t_p, ple_dim)
    ps = p_sample.reshape(-1, nb_s * t_s, ple_dim)

    lw = (norm_g[0], w_gu[0], w_down[0], w_pp[0], w_pg[0])
    cw = (conv_w_in[0].astype(BF16), conv_w[0], conv_w_out[0].astype(BF16))
    xp = _ffn(xp, norm_g[0, 0], w_gu[0, 0], w_down[0, 0])
    xs = _ffn(xs, norm_g[0, 0], w_gu[0, 0], w_down[0, 0])
    xp3, conv_p = _conv_mixer(xp.reshape(nb_p, t_p, d), jnp.zeros((nb_p, CONV_W - 1, d), F32),
                              norm_g[0, 1], *cw, shift=1)
    xs_tb = xs.reshape(nb_s, t_s, d).transpose(1, 0, 2).reshape(1, t_s * nb_s, d)
    prev_tb = state_conv[0].transpose(1, 0, 2).reshape(1, (CONV_W - 1) * nb_s, d)
    xs3, conv_s = _conv_mixer(xs_tb, prev_tb, norm_g[0, 1], *cw, shift=nb_s)
    xs = xs3.reshape(t_s, nb_s, d).transpose(1, 0, 2).reshape(nb_s * t_s, d)
    conv_s = conv_s.reshape(CONV_W - 1, nb_s, d).transpose(1, 0, 2)
    xp = _layer_tail(xp3.reshape(nb_p * t_p, d), pp[0], *lw)
    xs = _layer_tail(xs, ps[0], *lw)

    lw = (norm_g[1], w_gu[1], w_down[1], w_pp[1], w_pg[1])
    xp = _ffn(xp, norm_g[1, 0], w_gu[1, 0], w_down[1, 0])
    xs = _ffn(xs, norm_g[1, 0], w_gu[1, 0], w_down[1, 0])

    n_in = nsa_w_in.shape[-1]
    n_in_pad = -(-n_in // LANES) * LANES
    w_in = jnp.pad(nsa_w_in[0], ((0, 0), (0, n_in_pad - n_in))).astype(BF16)
    w_out = nsa_w_out[0].astype(BF16)
    qkg128 = jnp.tile(nsa_qk_g[0], (1, LANES // HEAD_DIM))
    gg = _group_sum_matrix()
    cos_p, sin_p = _rope_tables(jnp.arange(t_p))
    cos_s, sin_s = _rope_tables(past_len + jnp.arange(t_s))
    pos_k, bd1_k, bd2_k = _compress_weights(nsa_cmp_pos[0, 0], nsa_cmp_w1[0, 0], nsa_cmp_w2[0, 0])
    pos_v, bd1_v, bd2_v = _compress_weights(nsa_cmp_pos[0, 1], nsa_cmp_w1[0, 1], nsa_cmp_w2[0, 1])
    k_gain = qkg128[1:2]

    qn, qr, kc_r, vc_r, ks, vs, kw, vw, gt = _nsa_proj(
        xp, norm_g[1, 1], w_in, qkg128, jnp.tile(cos_p, (nb_p, 1)), jnp.tile(sin_p, (nb_p, 1)), gg)
    n_pages_p = t_p // PAGE_SIZE
    pt_p = jnp.arange(nb_p * n_pages_p, dtype=jnp.int32).reshape(nb_p, n_pages_p)
    as_pool = lambda a: a.reshape(nb_p * n_pages_p, PAGE_SIZE, KV_W)
    kc = _compress(pt_p, as_pool(kc_r), pos_k, bd1_k, bd2_k, k_gain, gg, True)
    vc = _compress(pt_p, as_pool(vc_r), pos_v, bd1_v, bd2_v, k_gain, gg, False)

    nqb = t_p // QBLK
    n_col = HPG * QBLK

    def q_cols(a):
        a = a.reshape(nb_p, nqb, QBLK, N_KV, HPG, HEAD_DIM).transpose(0, 3, 1, 5, 4, 2)
        return a.reshape(nb_p, N_KV, nqb, HEAD_DIM, n_col)

    def by_group(a):
        return a.reshape(nb_p, -1, N_KV, HEAD_DIM).transpose(0, 2, 1, 3)

    gt_t = gt[:, :N_KV * HPG * 3].reshape(nb_p, nqb, QBLK, N_KV, HPG, 3).transpose(0, 3, 1, 5, 4, 2)
    gt_t = jnp.pad(gt_t.reshape(nb_p, N_KV, nqb, 3, n_col), ((0, 0),) * 3 + ((0, 5), (0, 0)))
    front = ((0, 0), (0, 0), (WINDOW, 0), (0, 0))
    o_t = _attn_prompt(
        q_cols(qn), q_cols(qr), gt_t,
        by_group(kc), by_group(vc).transpose(0, 1, 3, 2),
        by_group(ks).astype(BF16), by_group(vs).astype(BF16).transpose(0, 1, 3, 2),
        jnp.pad(by_group(kw).astype(BF16), front),
        jnp.pad(by_group(vw).astype(BF16), front).transpose(0, 1, 3, 2))
    o = o_t.reshape(nb_p, N_KV, nqb, HEAD_DIM, HPG, QBLK).transpose(0, 2, 5, 1, 4, 3)
    xp = _proj_res(xp, o.reshape(nb_p * t_p, d), w_out)
    keep = min(WINDOW, t_p)
    st = lambda a: a.reshape(1, nb_p, t_p, N_KV, HEAD_DIM)
    nsa_p = (st(kc_r), st(vc_r), st(ks), st(vs), st(kw)[:, :, -keep:], st(vw)[:, :, -keep:])

    qn, qr, kc_r, vc_r, ks, vs, kw, vw, gt = _nsa_proj(
        xs, norm_g[1, 1], w_in, qkg128, jnp.tile(cos_s, (nb_s, 1)), jnp.tile(sin_s, (nb_s, 1)), gg)
    pool = lambda a: a.reshape(a.shape[0], PAGE_SIZE, KV_W)
    kc = _compress(page_table, pool(cache_k_cmp[0]), pos_k, bd1_k, bd2_k, k_gain, gg, True)
    vc = _compress(page_table, pool(cache_v_cmp[0]), pos_v, bd1_v, bd2_v, k_gain, gg, False)

    n_used = HPG * N_KV * t_s

    def q_block_diag(a):
        a = a.reshape(nb_s, t_s, N_KV, HPG, HEAD_DIM).astype(F32)
        bd = jnp.einsum("bqghd,gk->bgdhkq", a, jnp.eye(N_KV, dtype=F32)).reshape(nb_s, KV_W, n_used)
        return jnp.pad(bd, ((0, 0), (0, 0), (0, LANES - n_used))).astype(BF16)

    gt_col = gt[:, :N_KV * HPG * 3].reshape(nb_s, t_s, N_KV, HPG, 3).transpose(0, 3, 2, 1, 4)
    gt_col = jnp.pad(gt_col.reshape(nb_s, n_used, 3), ((0, 0), (0, LANES - n_used), (0, LANES - 3)))
    new_rows = lambda a: jnp.pad(a.reshape(nb_s, t_s, KV_W), ((0, 0), (0, 8 - t_s), (0, 0)))
    n_buf = cache_k_win.shape[2]
    o_s = _attn_sample(
        page_table, q_block_diag(qn), q_block_diag(qr), gt_col, kc, vc,
        cache_k_win[0].reshape(nb_s, n_buf, KV_W), cache_v_win[0].reshape(nb_s, n_buf, KV_W),
        new_rows(ks), new_rows(vs), new_rows(kw), new_rows(vw),
        pool(cache_k_sel[0]), pool(cache_v_sel[0]), t_s, past_len)
    o = o_s[:, :n_used, :HEAD_DIM].reshape(nb_s, HPG, N_KV, t_s, HEAD_DIM).transpose(0, 3, 2, 1, 4)
    xs = _proj_res(xs, o.reshape(nb_s * t_s, d).astype(BF16), w_out)
    st = lambda a: a.reshape(1, nb_s, t_s, N_KV, HEAD_DIM)
    win = lambda buf, new: jnp.concatenate([buf, st(new)], axis=2)[:, :, -n_buf:]
    nsa_s = (st(kc_r), st(vc_r), st(ks), st(vs), win(cache_k_win, kw), win(cache_v_win, vw))

    xp = _layer_tail(xp, pp[1], *lw)
    xs = _layer_tail(xs, ps[1], *lw)

    return (xp.reshape(nb_p, t_p, d), xs.reshape(nb_s, t_s, d), conv_p[None], conv_s[None]) + nsa_p + nsa_s
```

```python
import functools

import jax
import jax.numpy as jnp
from jax import lax
from jax.experimental import pallas as pl
from jax.experimental.pallas import tpu as pltpu

F32 = jnp.float32
BF16 = jnp.bfloat16

HEAD_DIM = 64
N_KV = 4
HPG = 4
CMP_BLK = 64
SEL_BLK = 64
TOP_N = 16
N_FORCED = 3
WINDOW = 512
PAGE_SIZE = 128
CONV_W = 3
QBLK = 128
KTILE = 512
ROPE_THETA = 10000.0
RMS_EPS = 1e-6
NEG_INF = float("-inf")
MASKED = -1e30
LOG2E = 1.4426950408889634
KV_W = N_KV * HEAD_DIM
LANES = 128
MIB = 1024 * 1024


def _cparams(n_grid, vmem_mib):
    return pltpu.CompilerParams(
        dimension_semantics=("arbitrary",) * n_grid,
        vmem_limit_bytes=vmem_mib * MIB,
    )


def _rms(x, g):
    return x * lax.rsqrt(jnp.mean(x * x, axis=-1, keepdims=True) + RMS_EPS) * g


def _dot(a, b):
    return jnp.dot(a, b, preferred_element_type=F32)


def _ffn_body(x_ref, g_ref, wg_ref, wu_ref, wd_ref, o_ref, h_ref, acc_ref, *, n_f):
    j = pl.program_id(1)

    @pl.when(j == 0)
    def _():
        h_ref[...] = _rms(x_ref[...], g_ref[...]).astype(BF16)
        acc_ref[...] = jnp.zeros_like(acc_ref)

    h = h_ref[...]
    gate = _dot(h, wg_ref[...])
    up = _dot(h, wu_ref[...])
    act = (jax.nn.silu(gate) * up).astype(BF16)
    acc_ref[...] += _dot(act, wd_ref[...])

    @pl.when(j == n_f - 1)
    def _():
        o_ref[...] = x_ref[...] + 0.5 * acc_ref[...]


def _ffn(x, g, w_gu, w_down):
    m, d = x.shape
    f = w_down.shape[0]
    tm = min(m, 512)
    tf = f // 2
    n_f = f // tf
    return pl.pallas_call(
        functools.partial(_ffn_body, n_f=n_f),
        grid=(m // tm, n_f),
        in_specs=[
            pl.BlockSpec((tm, d), lambda i, j: (i, 0)),
            pl.BlockSpec((1, d), lambda i, j: (0, 0)),
            pl.BlockSpec((d, tf), lambda i, j: (0, j)),
            pl.BlockSpec((d, tf), lambda i, j: (0, n_f + j)),
            pl.BlockSpec((tf, d), lambda i, j: (j, 0)),
        ],
        out_specs=pl.BlockSpec((tm, d), lambda i, j: (i, 0)),
        out_shape=jax.ShapeDtypeStruct((m, d), F32),
        scratch_shapes=[pltpu.VMEM((tm, d), BF16), pltpu.VMEM((tm, d), F32)],
        compiler_params=_cparams(2, 48),
        name="ffn",
    )(x, g.reshape(1, d), w_gu, w_gu, w_down)


def _ple_body(x_ref, g_ref, p_ref, wg_ref, wp_ref, o_ref):
    x = x_ref[...]
    h = _rms(x, g_ref[...]).astype(BF16)
    gate = jax.nn.sigmoid(_dot(h, wg_ref[...]))
    o_ref[...] = x + gate * _dot(p_ref[...].astype(BF16), wp_ref[...])


def _ple(x, g, p, w_gate, w_proj):
    m, d = x.shape
    pd = p.shape[1]
    tm = min(m, 512)
    return pl.pallas_call(
        _ple_body,
        grid=(m // tm,),
        in_specs=[
            pl.BlockSpec((tm, d), lambda i: (i, 0)),
            pl.BlockSpec((1, d), lambda i: (0, 0)),
            pl.BlockSpec((tm, pd), lambda i: (i, 0)),
            pl.BlockSpec((d, d), lambda i: (0, 0)),
            pl.BlockSpec((pd, d), lambda i: (0, 0)),
        ],
        out_specs=pl.BlockSpec((tm, d), lambda i: (i, 0)),
        out_shape=jax.ShapeDtypeStruct((m, d), F32),
        compiler_params=_cparams(1, 32),
        name="ple",
    )(x, g.reshape(1, d), p, w_gate, w_proj)


def _proj_res_body(x_ref, o_ref, w_ref, y_ref):
    y_ref[...] = x_ref[...] + _dot(o_ref[...], w_ref[...])


def _proj_res(x, o, w):
    m, d = x.shape
    k = o.shape[1]
    tm = min(m, 512)
    return pl.pallas_call(
        _proj_res_body,
        grid=(m // tm,),
        in_specs=[
            pl.BlockSpec((tm, d), lambda i: (i, 0)),
            pl.BlockSpec((tm, k), lambda i: (i, 0)),
            pl.BlockSpec((k, d), lambda i: (0, 0)),
        ],
        out_specs=pl.BlockSpec((tm, d), lambda i: (i, 0)),
        out_shape=jax.ShapeDtypeStruct((m, d), F32),
        compiler_params=_cparams(1, 32),
        name="proj_res",
    )(x, o, w)


def _conv_body(x_ref, prev_ref, g_ref, win_ref, wc_ref, wout_ref, y_ref, st_ref, u_ref,
               *, shift, halo, tm, d):
    t = pl.program_id(1)
    lo = halo - 2 * shift

    @pl.when(t == 0)
    def _():
        u_ref[lo:halo, :] = prev_ref[0]

    x = x_ref[0]
    h = _rms(x, g_ref[...]).astype(BF16)
    proj = _dot(h, win_ref[...])
    b_gate = proj[:, :d]
    u = proj[:, d:2 * d] * proj[:, 2 * d:]
    u_ref[halo:halo + tm, :] = u
    wc = wc_ref[...]
    y = (wc[0:1, :] * u_ref[lo:lo + tm, :]
         + wc[1:2, :] * u_ref[lo + shift:lo + shift + tm, :]
         + wc[2:3, :] * u)
    y_ref[0] = x + _dot((b_gate * y).astype(BF16), wout_ref[...])
    tail = u_ref[lo + tm:halo + tm, :]
    st_ref[0] = tail
    u_ref[lo:halo, :] = tail


def _conv_mixer(x, prev, g, w_in, w_conv, w_out, shift):
    nb, t, d = x.shape
    tm = min(t, 512)
    halo = -(-2 * shift // 8) * 8
    return pl.pallas_call(
        functools.partial(_conv_body, shift=shift, halo=halo, tm=tm, d=d),
        grid=(nb, t // tm),
        in_specs=[
            pl.BlockSpec((1, tm, d), lambda b, i: (b, i, 0)),
            pl.BlockSpec((1, 2 * shift, d), lambda b, i: (b, 0, 0)),
            pl.BlockSpec((1, d), lambda b, i: (0, 0)),
            pl.BlockSpec((d, 3 * d), lambda b, i: (0, 0)),
            pl.BlockSpec((CONV_W, d), lambda b, i: (0, 0)),
            pl.BlockSpec((d, d), lambda b, i: (0, 0)),
        ],
        out_specs=[
            pl.BlockSpec((1, tm, d), lambda b, i: (b, i, 0)),
            pl.BlockSpec((1, 2 * shift, d), lambda b, i: (b, 0, 0)),
        ],
        out_shape=[
            jax.ShapeDtypeStruct((nb, t, d), F32),
            jax.ShapeDtypeStruct((nb, 2 * shift, d), F32),
        ],
        scratch_shapes=[pltpu.VMEM((halo + tm, d), F32)],
        compiler_params=_cparams(2, 48),
        name="conv_mixer",
    )(x, prev, g.reshape(1, d), w_in, w_conv, w_out)


def _head_norm(xc, gain, gg):
    sq = xc * xc
    hi = sq.astype(BF16)
    lo = (sq - hi.astype(F32)).astype(BF16)
    ss = _dot(jnp.concatenate([hi, lo], axis=1), gg)
    return xc * lax.rsqrt(ss * (1.0 / HEAD_DIM) + RMS_EPS) * gain


def _nsa_proj_body(x_ref, g_ref, w_ref, qkg_ref, cos_ref, sin_ref, gg_ref,
                   qn_ref, qr_ref, kc_ref, vc_ref, ks_ref, vs_ref, kw_ref, vw_ref, gt_ref,
                   *, tm, d):
    h = _rms(x_ref[...], g_ref[...]).astype(BF16)
    proj = _dot(h, w_ref[...])
    cos = cos_ref[...]
    sin = sin_ref[...]
    gg = gg_ref[...]
    lane = lax.broadcasted_iota(jnp.int32, (tm, LANES), 1)
    first_half = (lane % HEAD_DIM) < (HEAD_DIM // 2)
    scale = HEAD_DIM ** -0.5 * LOG2E

    def rope(xc):
        swapped = jnp.where(first_half, pltpu.roll(xc, LANES - HEAD_DIM // 2, 1),
                            pltpu.roll(xc, HEAD_DIM // 2, 1))
        return xc * cos + swapped * sin

    def slab(c):
        return proj[:, c * LANES:(c + 1) * LANES]

    n_q = d // LANES
    for c in range(n_q):
        qc = _head_norm(slab(c), qkg_ref[0:1, :], gg)
        qn_ref[:, c * LANES:(c + 1) * LANES] = (qc * scale).astype(BF16)
        qr_ref[:, c * LANES:(c + 1) * LANES] = (rope(qc) * scale).astype(BF16)
    for c in range(2):
        kc_ref[:, c * LANES:(c + 1) * LANES] = slab(n_q + c)
        vc_ref[:, c * LANES:(c + 1) * LANES] = slab(n_q + 2 + c)
        ks_ref[:, c * LANES:(c + 1) * LANES] = rope(_head_norm(slab(n_q + 4 + c), qkg_ref[2:3, :], gg))
        vs_ref[:, c * LANES:(c + 1) * LANES] = slab(n_q + 6 + c)
        kw_ref[:, c * LANES:(c + 1) * LANES] = rope(_head_norm(slab(n_q + 8 + c), qkg_ref[3:4, :], gg))
        vw_ref[:, c * LANES:(c + 1) * LANES] = slab(n_q + 10 + c)
    gt_ref[...] = jax.nn.sigmoid(slab(n_q + 12))


def _nsa_proj(x, g, w_in_pad, qkg128, cos128, sin128, gg):
    m, d = x.shape
    n = w_in_pad.shape[1]
    tm = min(m, 512)
    row = lambda w: pl.BlockSpec((tm, w), lambda i: (i, 0))
    full = lambda a: pl.BlockSpec(a.shape, lambda i: (0,) * a.ndim)
    return pl.pallas_call(
        functools.partial(_nsa_proj_body, tm=tm, d=d),
        grid=(m // tm,),
        in_specs=[row(d), pl.BlockSpec((1, d), lambda i: (0, 0)), pl.BlockSpec((d, n), lambda i: (0, 0)),
                  full(qkg128), row(LANES), row(LANES), full(gg)],
        out_specs=[row(d), row(d)] + [row(KV_W)] * 6 + [row(LANES)],
        out_shape=[jax.ShapeDtypeStruct((m, d), BF16)] * 2
        + [jax.ShapeDtypeStruct((m, KV_W), F32)] * 6
        + [jax.ShapeDtypeStruct((m, LANES), F32)],
        compiler_params=_cparams(1, 48),
        name="nsa_proj",
    )(x, g.reshape(1, d), w_in_pad, qkg128, cos128, sin128, gg)


def _page_copies(pool_ref, buf_ref, sem, slot, page, p, split):
    rows = pl.ds(p * PAGE_SIZE, PAGE_SIZE)
    if not split:
        return [pltpu.make_async_copy(pool_ref.at[page], buf_ref.at[slot, rows], sem.at[slot])]
    return [pltpu.make_async_copy(pool_ref.at[page, :, pl.ds(hh * LANES, LANES)],
                                  buf_ref.at[slot, hh, rows], sem.at[slot])
            for hh in range(KV_W // LANES)]


def _gather_start(pt_ref, pool_ref, buf_ref, sem, b, slot, n_pages, split=False):
    for p in range(n_pages):
        for copy in _page_copies(pool_ref, buf_ref, sem, slot, pt_ref[b, p], p, split):
            copy.start()


def _gather_wait(pool_ref, buf_ref, sem, slot, n_pages, split=False):
    for p in range(n_pages):
        for copy in _page_copies(pool_ref, buf_ref, sem, slot, 0, p, split):
            copy.wait()


def _compress_body(pt_ref, pool_ref, pos_ref, bd1_ref, bd2_ref, gain_ref, gg_ref, o_ref,
                   buf_ref, sem, acc_ref, *, n_pages, normalize):
    b = pl.program_id(0)
    nb = pl.num_programs(0)
    slot = b % 2
    n_blk = n_pages * PAGE_SIZE // CMP_BLK
    n_half = KV_W // LANES
    unroll = 8

    @pl.when(b == 0)
    def _():
        _gather_start(pt_ref, pool_ref, buf_ref, sem, 0, 0, n_pages, split=True)

    @pl.when(b + 1 < nb)
    def _():
        _gather_start(pt_ref, pool_ref, buf_ref, sem, b + 1, 1 - slot, n_pages, split=True)

    _gather_wait(pool_ref, buf_ref, sem, slot, n_pages, split=True)
    acc_ref[...] = jnp.zeros_like(acc_ref)

    def chunk(c, carry):
        accs = [acc_ref[hh] for hh in range(n_half)]
        for rr in range(unroll):
            r = c * unroll + rr
            w = bd1_ref[r]
            pos = pos_ref[pl.ds(r, 1), :]
            for hh in range(n_half):
                rows = buf_ref[slot, hh, pl.ds(r, n_blk, stride=CMP_BLK), :] + pos
                accs[hh] = accs[hh] + _dot(rows.astype(BF16), w)
        for hh in range(n_half):
            acc_ref[hh] = accs[hh]
        return carry

    lax.fori_loop(0, CMP_BLK // unroll, chunk, 0)

    for hh in range(n_half):
        y = _dot(jax.nn.gelu(acc_ref[hh]).astype(BF16), bd2_ref[...])
        if normalize:
            y = _head_norm(y, gain_ref[...], gg_ref[...])
        o_ref[0, :, hh * LANES:(hh + 1) * LANES] = y.astype(BF16)


def _compress(page_table, pool, pos2, bd1, bd2, gain128, gg, normalize):
    nb, n_pages = page_table.shape
    n_blk = n_pages * PAGE_SIZE // CMP_BLK
    n_half = KV_W // LANES
    hid2 = bd1.shape[2]
    full = lambda a: pl.BlockSpec(a.shape, lambda b, pt: (0,) * a.ndim)
    grid_spec = pltpu.PrefetchScalarGridSpec(
        num_scalar_prefetch=1,
        grid=(nb,),
        in_specs=[pl.BlockSpec(memory_space=pl.ANY), full(pos2), full(bd1), full(bd2), full(gain128), full(gg)],
        out_specs=pl.BlockSpec((1, n_blk, KV_W), lambda b, pt: (b, 0, 0)),
        scratch_shapes=[
            pltpu.VMEM((2, n_half, n_pages * PAGE_SIZE, LANES), F32),
            pltpu.SemaphoreType.DMA((2,)),
            pltpu.VMEM((n_half, n_blk, hid2), F32),
        ],
    )
    return pl.pallas_call(
        functools.partial(_compress_body, n_pages=n_pages, normalize=normalize),
        grid_spec=grid_spec,
        out_shape=jax.ShapeDtypeStruct((nb, n_blk, KV_W), BF16),
        compiler_params=_cparams(1, 40),
        name="compress",
    )(page_table, pool, pos2, bd1, bd2, gain128, gg)


def _softmax_cols(s):
    m = jnp.max(s, axis=0, keepdims=True)
    m = jnp.where(m == NEG_INF, 0.0, m)
    e = jnp.exp2(s - m)
    return e, jnp.sum(e, axis=0, keepdims=True)


def _select_bias(imp, cur):
    jj = lax.broadcasted_iota(jnp.int32, imp.shape, 0)
    forced = (jj == 0) | (jj == cur) | (jj == cur - 1)
    cand = (jj <= cur) & jnp.logical_not(forced)
    v = jnp.where(cand, imp, -1.0)
    bias = jnp.where(forced, 0.0, NEG_INF)
    for _ in range(TOP_N - N_FORCED):
        m = jnp.max(v, axis=0, keepdims=True)
        idx = jnp.min(jnp.where(v == m, jj, imp.shape[0]), axis=0, keepdims=True)
        idx = jnp.where(m >= 0.0, idx, -1)
        pick = jj == idx
        bias = jnp.where(pick, 0.0, bias)
        v = jnp.where(pick, -1.0, v)
    return bias


def _block_bias(selb_ref, first_block, n_blocks, reps):
    rows = [jnp.broadcast_to(selb_ref[pl.ds(first_block + r, 1), :], (SEL_BLK, LANES))
            for r in range(n_blocks)]
    bias = jnp.concatenate(rows, axis=0)
    return jnp.concatenate([bias] * reps, axis=1) if reps > 1 else bias


def _attn_prompt_body(qn_ref, qr_ref, gt_ref, kc_ref, vct_ref, ks_ref, blk_ref, vst_ref, kw_ref, vwt_ref,
                      o_ref, qsel_ref, sa_ref, sb_ref):
    i = pl.program_id(2)
    q_start = i * QBLK
    n_col = HPG * QBLK
    qn = qn_ref[...]
    qr = qr_ref[...]
    q_pos = q_start + lax.broadcasted_iota(jnp.int32, (1, n_col), 1) % QBLK

    n_cmp = kc_ref.shape[0]
    s = _dot(kc_ref[...], qn)
    cmp_end = lax.broadcasted_iota(jnp.int32, (n_cmp, 1), 0) * CMP_BLK + (CMP_BLK - 1)
    s = jnp.where(cmp_end <= q_pos, s, NEG_INF)
    e, l = _softmax_cols(s)
    p_cmp = e / jnp.maximum(l, 1e-30)
    o_cmp = _dot(vct_ref[...], p_cmp.astype(BF16))

    imp = p_cmp[:, 0:QBLK]
    for hh in range(1, HPG):
        imp = imp + p_cmp[:, hh * QBLK:(hh + 1) * QBLK]
    bias = jnp.maximum(_select_bias(imp, q_pos[:, 0:QBLK] // SEL_BLK), MASKED)
    own = lax.broadcasted_iota(jnp.int32, (n_cmp, QBLK), 0) >= q_start // SEL_BLK
    bias = jnp.where(own, MASKED, bias)
    qsel_ref[0:HEAD_DIM, :] = qr
    qsel_ref[HEAD_DIM:LANES, :] = jnp.zeros((LANES - HEAD_DIM, n_col), BF16)
    qsel_ref[LANES:LANES + n_cmp, :] = jnp.concatenate([bias.astype(BF16)] * HPG, axis=1)
    if n_cmp < LANES:
        qsel_ref[LANES + n_cmp:, :] = jnp.zeros((LANES - n_cmp, n_col), BF16)

    n_win = WINDOW + QBLK
    w0 = pl.multiple_of(q_start, QBLK)
    s = _dot(kw_ref[pl.ds(w0, n_win), :], qr)
    r = lax.broadcasted_iota(jnp.int32, (n_win, QBLK), 0)
    q = lax.broadcasted_iota(jnp.int32, (n_win, QBLK), 1)
    valid = (r > q) & (r <= q + WINDOW) & (r + q_start >= WINDOW)
    wbias = jnp.where(valid, 0.0, NEG_INF)
    s = s + jnp.concatenate([wbias] * HPG, axis=1)
    e, l = _softmax_cols(s)
    o_win = _dot(vwt_ref[:, pl.ds(w0, n_win)], e.astype(BF16)) / jnp.maximum(l, 1e-30)

    s = _dot(ks_ref[pl.ds(w0, QBLK), :], qsel_ref[0:LANES, :])
    tri = jnp.where(lax.broadcasted_iota(jnp.int32, (QBLK, QBLK), 0)
                    <= lax.broadcasted_iota(jnp.int32, (QBLK, QBLK), 1), 0.0, NEG_INF)
    s = s + jnp.concatenate([tri] * HPG, axis=1)
    m0 = jnp.max(s, axis=0, keepdims=True)
    p = jnp.exp2(s - m0)
    init = (m0, jnp.sum(p, axis=0, keepdims=True), _dot(vst_ref[:, pl.ds(w0, QBLK)], p.astype(BF16)))

    n_tiles = (q_start + KTILE - 1) // KTILE
    max_tile = ks_ref.shape[0] // KTILE - 1

    def scores(t):
        k0 = pl.multiple_of(t * KTILE, KTILE)
        keys = jnp.concatenate([ks_ref[pl.ds(k0, KTILE), :], blk_ref[pl.ds(k0, KTILE), :]], axis=1)
        return _dot(keys, qsel_ref[...])

    def sel_tile(s_buf, t, carry):
        m, l, acc = carry
        k0 = pl.multiple_of(t * KTILE, KTILE)
        s = s_buf[...]
        m_new = jnp.maximum(m, jnp.max(s, axis=0, keepdims=True))
        alpha = jnp.exp2(m - m_new)
        p = jnp.exp2(s - m_new)
        l = alpha * l + jnp.sum(p, axis=0, keepdims=True)
        acc = alpha * acc + _dot(vst_ref[:, pl.ds(k0, KTILE)], p.astype(BF16))
        return m_new, l, acc

    def tile_pair(u, carry):
        t = 2 * u
        sb_ref[...] = scores(t + 1)
        carry = sel_tile(sa_ref, t, carry)
        sa_ref[...] = scores(jnp.minimum(t + 2, max_tile))
        return sel_tile(sb_ref, t + 1, carry)

    sa_ref[...] = scores(0)
    carry = lax.fori_loop(0, n_tiles // 2, tile_pair, init)
    _, l_sel, acc_sel = lax.fori_loop(0, n_tiles % 2, lambda _, c: sel_tile(sa_ref, n_tiles - 1, c), carry)
    o_sel = acc_sel / l_sel

    gt = gt_ref[...]
    o_ref[...] = (gt[0:1, :] * o_cmp + gt[1:2, :] * o_sel + gt[2:3, :] * o_win).astype(BF16)


def _attn_prompt(qn_t, qr_t, gt_t, kc, vc_t, ks_pad, blk_onehot, vs_t, kw_pad, vw_pad_t):
    nb, ng, nqb, hd, n_col = qn_t.shape
    n_cmp = kc.shape[2]
    assert n_cmp <= LANES
    per_q = lambda r: pl.BlockSpec((None, None, None, r, n_col), lambda b, g, i: (b, g, i, 0, 0))
    per_bg = lambda a: pl.BlockSpec((None, None) + a.shape[2:], lambda b, g, i: (b, g, 0, 0))
    return pl.pallas_call(
        _attn_prompt_body,
        grid=(nb, ng, nqb),
        in_specs=[per_q(hd), per_q(hd), per_q(8), per_bg(kc), per_bg(vc_t), per_bg(ks_pad),
                  pl.BlockSpec(blk_onehot.shape, lambda b, g, i: (0, 0)), per_bg(vs_t),
                  per_bg(kw_pad), per_bg(vw_pad_t)],
        out_specs=per_q(hd),
        out_shape=jax.ShapeDtypeStruct((nb, ng, nqb, hd, n_col), BF16),
        scratch_shapes=[pltpu.VMEM((2 * LANES, n_col), BF16), pltpu.VMEM((KTILE, n_col), F32),
                        pltpu.VMEM((KTILE, n_col), F32)],
        compiler_params=_cparams(3, 48),
        name="attn_prompt",
    )(qn_t, qr_t, gt_t, kc, vc_t, ks_pad, blk_onehot, vs_t, kw_pad, vw_pad_t)


def _attn_sample_body(pt_ref, qn_ref, qr_ref, gt_ref, kc_ref, vc_ref, kwin_ref, vwin_ref,
                      ksn_ref, vsn_ref, kwn_ref, vwn_ref, kpool_ref, vpool_ref, o_ref,
                      kbuf_ref, vbuf_ref, ksem, vsem, s_ref, selb_ref,
                      *, n_pages, n_new, past_len):
    b = pl.program_id(0)
    nb = pl.num_programs(0)
    slot = b % 2

    @pl.when(b == 0)
    def _():
        _gather_start(pt_ref, kpool_ref, kbuf_ref, ksem, 0, 0, n_pages)
        _gather_start(pt_ref, vpool_ref, vbuf_ref, vsem, 0, 0, n_pages)

    @pl.when(b + 1 < nb)
    def _():
        _gather_start(pt_ref, kpool_ref, kbuf_ref, ksem, b + 1, 1 - slot, n_pages)
        _gather_start(pt_ref, vpool_ref, vbuf_ref, vsem, b + 1, 1 - slot, n_pages)

    qn = qn_ref[0]
    qr = qr_ref[0]
    n_past = n_pages * PAGE_SIZE
    n_blk = n_past // SEL_BLK
    col = lax.broadcasted_iota(jnp.int32, (1, LANES), 1)
    q_idx = col % n_new
    q_pos = past_len + q_idx
    n_pad = ksn_ref.shape[1]
    new_row = lax.broadcasted_iota(jnp.int32, (n_pad, 1), 0)
    new_ok = (new_row <= q_idx) & (new_row < n_new)

    def weighted(p, v):
        return _dot(p.T.astype(BF16), v.astype(BF16))

    s = _dot(kc_ref[0], qn)
    cmp_end = lax.broadcasted_iota(jnp.int32, (n_blk, 1), 0) * CMP_BLK + (CMP_BLK - 1)
    s = jnp.where(cmp_end <= q_pos, s, NEG_INF)
    e, l = _softmax_cols(s)
    p_cmp = e / jnp.maximum(l, 1e-30)
    o_cmp = weighted(p_cmp, vc_ref[0])

    n_gq = N_KV * n_new
    imp = p_cmp
    for hh in range(1, HPG):
        imp = imp + pltpu.roll(p_cmp, LANES - hh * n_gq, 1)
    sel = _select_bias(imp, q_pos // SEL_BLK)
    sel01 = jnp.where((col < n_gq) & (sel == 0.0), 1.0, 0.0)
    spread = sel01
    for hh in range(1, HPG):
        spread = spread + pltpu.roll(sel01, hh * n_gq, 1)
    selb_ref[...] = jnp.where(spread > 0.5, 0.0, NEG_INF)

    _gather_wait(kpool_ref, kbuf_ref, ksem, slot, n_pages)
    _gather_wait(vpool_ref, vbuf_ref, vsem, slot, n_pages)
    chunk = 1024
    n_chunks = n_past // chunk
    blocks_per_chunk = chunk // SEL_BLK

    def score_chunk(c, m):
        k0 = pl.multiple_of(c * chunk, chunk)
        s = _dot(kbuf_ref[slot, pl.ds(k0, chunk), :].astype(BF16), qr)
        s = s + _block_bias(selb_ref, c * blocks_per_chunk, blocks_per_chunk, 1)
        s_ref[pl.ds(k0, chunk), :] = s
        return jnp.maximum(m, jnp.max(s, axis=0, keepdims=True))

    m = lax.fori_loop(0, n_chunks, score_chunk, jnp.full((1, LANES), NEG_INF, F32))
    s_new = jnp.where(new_ok, _dot(ksn_ref[0].astype(BF16), qr), NEG_INF)
    m = jnp.maximum(m, jnp.max(s_new, axis=0, keepdims=True))
    m = jnp.where(m == NEG_INF, 0.0, m)

    def value_chunk(c, carry):
        l, acc = carry
        k0 = pl.multiple_of(c * chunk, chunk)
        p = jnp.exp2(s_ref[pl.ds(k0, chunk), :] - m)
        l = l + jnp.sum(p, axis=0, keepdims=True)
        return l, acc + weighted(p, vbuf_ref[slot, pl.ds(k0, chunk), :])

    p_new = jnp.exp2(s_new - m)
    l_sel, acc = lax.fori_loop(
        0, n_chunks, value_chunk,
        (jnp.sum(p_new, axis=0, keepdims=True), weighted(p_new, vsn_ref[0])))
    o_sel = acc / jnp.maximum(l_sel, 1e-30).T

    n_buf = kwin_ref.shape[1]
    s_buf = _dot(kwin_ref[0].astype(BF16), qr)
    buf_pos = past_len - n_buf + lax.broadcasted_iota(jnp.int32, (n_buf, 1), 0)
    diff = q_pos - buf_pos
    s_buf = jnp.where((diff >= 0) & (diff < WINDOW) & (buf_pos >= 0), s_buf, NEG_INF)
    s_nw = jnp.where(new_ok, _dot(kwn_ref[0].astype(BF16), qr), NEG_INF)
    m = jnp.maximum(jnp.max(s_buf, axis=0, keepdims=True), jnp.max(s_nw, axis=0, keepdims=True))
    m = jnp.where(m == NEG_INF, 0.0, m)
    p_buf = jnp.exp2(s_buf - m)
    p_nw = jnp.exp2(s_nw - m)
    l_win = jnp.sum(p_buf, axis=0, keepdims=True) + jnp.sum(p_nw, axis=0, keepdims=True)
    o_win = (weighted(p_buf, vwin_ref[0]) + weighted(p_nw, vwn_ref[0])) / jnp.maximum(l_win, 1e-30).T

    gt = gt_ref[0]
    o = gt[:, 0:1] * o_cmp + gt[:, 1:2] * o_sel + gt[:, 2:3] * o_win
    row_g = (lax.broadcasted_iota(jnp.int32, (LANES, KV_W), 0) // n_new) % N_KV
    lane_g = lax.broadcasted_iota(jnp.int32, (LANES, KV_W), 1) // HEAD_DIM
    o = jnp.where(row_g == lane_g, o, 0.0)
    z = o[:, :LANES] + o[:, LANES:]
    o_ref[0] = z + pltpu.roll(z, HEAD_DIM, 1)


def _attn_sample(page_table, qn_bd, qr_bd, gt_col, kc, vc, kwin, vwin, ks_new, vs_new, kw_new, vw_new,
                 k_pool, v_pool, n_new, past_len):
    nb, n_pages = page_table.shape
    n_past = n_pages * PAGE_SIZE
    per_b = lambda a: pl.BlockSpec((1,) + a.shape[1:], lambda b, pt: (b,) + (0,) * (a.ndim - 1))
    any_spec = pl.BlockSpec(memory_space=pl.ANY)
    ins = (qn_bd, qr_bd, gt_col, kc, vc, kwin, vwin, ks_new, vs_new, kw_new, vw_new)
    grid_spec = pltpu.PrefetchScalarGridSpec(
        num_scalar_prefetch=1,
        grid=(nb,),
        in_specs=[per_b(a) for a in ins] + [any_spec, any_spec],
        out_specs=pl.BlockSpec((1, LANES, LANES), lambda b, pt: (b, 0, 0)),
        scratch_shapes=[
            pltpu.VMEM((2, n_past, KV_W), F32),
            pltpu.VMEM((2, n_past, KV_W), F32),
            pltpu.SemaphoreType.DMA((2,)),
            pltpu.SemaphoreType.DMA((2,)),
            pltpu.VMEM((n_past, LANES), F32),
            pltpu.VMEM((n_past // SEL_BLK, LANES), F32),
        ],
    )
    return pl.pallas_call(
        functools.partial(_attn_sample_body, n_pages=n_pages, n_new=n_new, past_len=past_len),
        grid_spec=grid_spec,
        out_shape=jax.ShapeDtypeStruct((nb, LANES, LANES), F32),
        compiler_params=_cparams(1, 56),
        name="attn_sample",
    )(page_table, *ins, k_pool, v_pool)


def _rope_tables(pos):
    half = HEAD_DIM // 2
    inv = ROPE_THETA ** (-jnp.arange(half, dtype=F32) / half)
    ang = pos.astype(F32)[:, None] * inv
    cos, sin = jnp.cos(ang), jnp.sin(ang)
    reps = LANES // HEAD_DIM
    return (jnp.tile(jnp.concatenate([cos, cos], axis=1), (1, reps)),
            jnp.tile(jnp.concatenate([-sin, sin], axis=1), (1, reps)))


def _group_sum_matrix():
    lane = jnp.arange(LANES)
    g = (lane[:, None] // HEAD_DIM == lane[None, :] // HEAD_DIM).astype(BF16)
    return jnp.concatenate([g, g], axis=0)


def _compress_weights(cmp_pos, cmp_w1, cmp_w2):
    n_pair = LANES // HEAD_DIM
    eye = jnp.eye(n_pair, dtype=F32)
    hid = cmp_w1.shape[1]
    w1 = cmp_w1.reshape(CMP_BLK, HEAD_DIM, hid)
    bd1 = jnp.einsum("gk,rdh->rgdkh", eye, w1).reshape(CMP_BLK, LANES, n_pair * hid).astype(BF16)
    bd2 = jnp.einsum("gk,hd->ghkd", eye, cmp_w2).reshape(n_pair * hid, LANES).astype(BF16)
    return jnp.tile(cmp_pos, (1, n_pair)), bd1, bd2


def _layer_tail(x, p, norm_g, w_gu, w_down, w_ple_proj, w_ple_gate):
    x = _ffn(x, norm_g[2], w_gu[1], w_down[1])
    return _ple(x, norm_g[3], p, w_ple_gate, w_ple_proj)


def kernel(x_prompt, x_sample, state_conv, cache_k_cmp, cache_v_cmp, cache_k_sel, cache_v_sel,
           cache_k_win, cache_v_win, page_table, p_prompt, p_sample, norm_g, ffn_w_gu, ffn_w_down,
           ple_w_proj, ple_w_gate, conv_w_in, conv_w, conv_w_out, nsa_w_in, nsa_qk_g, nsa_cmp_pos,
           nsa_cmp_w1, nsa_cmp_w2, nsa_w_out):
    nb_p, t_p, d = x_prompt.shape
    nb_s, t_s, _ = x_sample.shape
    ple_dim = p_prompt.shape[-1]
    past_len = page_table.shape[1] * PAGE_SIZE
    assert t_p % KTILE == 0 and t_p % PAGE_SIZE == 0 and d % LANES == 0

    w_gu = ffn_w_gu.astype(BF16)
    w_down = ffn_w_down.astype(BF16)
    w_pp = ple_w_proj.astype(BF16)
    w_pg = ple_w_gate.astype(BF16)

    xp = x_prompt.reshape(nb_p * t_p, d)
    xs = x_sample.reshape(nb_s * t_s, d)
    pp = p_prompt.reshape(-1, nb_p * t_p, ple_dim)
    ps = p_sample.reshape(-1, nb_s * t_s, ple_dim)

    lw = (norm_g[0], w_gu[0], w_down[0], w_pp[0], w_pg[0])
    cw = (conv_w_in[0].astype(BF16), conv_w[0], conv_w_out[0].astype(BF16))
    xp = _ffn(xp, norm_g[0, 0], w_gu[0, 0], w_down[0, 0])
    xs = _ffn(xs, norm_g[0, 0], w_gu[0, 0], w_down[0, 0])
    xp3, conv_p = _conv_mixer(xp.reshape(nb_p, t_p, d), jnp.zeros((nb_p, CONV_W - 1, d), F32),
                              norm_g[0, 1], *cw, shift=1)
    xs_tb = xs.reshape(nb_s, t_s, d).transpose(1, 0, 2).reshape(1, t_s * nb_s, d)
    prev_tb = state_conv[0].transpose(1, 0, 2).reshape(1, (CONV_W - 1) * nb_s, d)
    xs3, conv_s = _conv_mixer(xs_tb, prev_tb, norm_g[0, 1], *cw, shift=nb_s)
    xs = xs3.reshape(t_s, nb_s, d).transpose(1, 0, 2).reshape(nb_s * t_s, d)
    conv_s = conv_s.reshape(CONV_W - 1, nb_s, d).transpose(1, 0, 2)
    xp = _layer_tail(xp3.reshape(nb_p * t_p, d), pp[0], *lw)
    xs = _layer_tail(xs, ps[0], *lw)

    lw = (norm_g[1], w_gu[1], w_down[1], w_pp[1], w_pg[1])
    xp = _ffn(xp, norm_g[1, 0], w_gu[1, 0], w_down[1, 0])
    xs = _ffn(xs, norm_g[1, 0], w_gu[1, 0], w_down[1, 0])

    n_in = nsa_w_in.shape[-1]
    n_in_pad = -(-n_in // LANES) * LANES
    w_in = jnp.pad(nsa_w_in[0], ((0, 0), (0, n_in_pad - n_in))).astype(BF16)
    w_out = nsa_w_out[0].astype(BF16)
    qkg128 = jnp.tile(nsa_qk_g[0], (1, LANES // HEAD_DIM))
    gg = _group_sum_matrix()
    cos_p, sin_p = _rope_tables(jnp.arange(t_p))
    cos_s, sin_s = _rope_tables(past_len + jnp.arange(t_s))
    pos_k, bd1_k, bd2_k = _compress_weights(nsa_cmp_pos[0, 0], nsa_cmp_w1[0, 0], nsa_cmp_w2[0, 0])
    pos_v, bd1_v, bd2_v = _compress_weights(nsa_cmp_pos[0, 1], nsa_cmp_w1[0, 1], nsa_cmp_w2[0, 1])
    k_gain = qkg128[1:2]

    qn, qr, kc_r, vc_r, ks, vs, kw, vw, gt = _nsa_proj(
        xp, norm_g[1, 1], w_in, qkg128, jnp.tile(cos_p, (nb_p, 1)), jnp.tile(sin_p, (nb_p, 1)), gg)
    n_pages_p = t_p // PAGE_SIZE
    pt_p = jnp.arange(nb_p * n_pages_p, dtype=jnp.int32).reshape(nb_p, n_pages_p)
    as_pool = lambda a: a.reshape(nb_p * n_pages_p, PAGE_SIZE, KV_W)
    kc = _compress(pt_p, as_pool(kc_r), pos_k, bd1_k, bd2_k, k_gain, gg, True)
    vc = _compress(pt_p, as_pool(vc_r), pos_v, bd1_v, bd2_v, k_gain, gg, False)

    nqb = t_p // QBLK
    n_col = HPG * QBLK

    def q_cols(a):
        a = a.reshape(nb_p, nqb, QBLK, N_KV, HPG, HEAD_DIM).transpose(0, 3, 1, 5, 4, 2)
        return a.reshape(nb_p, N_KV, nqb, HEAD_DIM, n_col)

    def by_group(a):
        return a.reshape(nb_p, -1, N_KV, HEAD_DIM).transpose(0, 2, 1, 3)

    gt_t = gt[:, :N_KV * HPG * 3].reshape(nb_p, nqb, QBLK, N_KV, HPG, 3).transpose(0, 3, 1, 5, 4, 2)
    gt_t = jnp.pad(gt_t.reshape(nb_p, N_KV, nqb, 3, n_col), ((0, 0),) * 3 + ((0, 5), (0, 0)))
    front = ((0, 0), (0, 0), (WINDOW, 0), (0, 0))
    lane_pad = ((0, 0), (0, 0), (0, 0), (0, LANES - HEAD_DIM))
    blk_onehot = (jnp.arange(t_p)[:, None] // SEL_BLK == jnp.arange(LANES)[None, :]).astype(BF16)
    o_t = _attn_prompt(
        q_cols(qn), q_cols(qr), gt_t,
        by_group(kc), by_group(vc).transpose(0, 1, 3, 2),
        jnp.pad(by_group(ks).astype(BF16), lane_pad), blk_onehot,
        by_group(vs).astype(BF16).transpose(0, 1, 3, 2),
        jnp.pad(by_group(kw).astype(BF16), front),
        jnp.pad(by_group(vw).astype(BF16), front).transpose(0, 1, 3, 2))
    o = o_t.reshape(nb_p, N_KV, nqb, HEAD_DIM, HPG, QBLK).transpose(0, 2, 5, 1, 4, 3)
    xp = _proj_res(xp, o.reshape(nb_p * t_p, d), w_out)
    keep = min(WINDOW, t_p)
    st = lambda a: a.reshape(1, nb_p, t_p, N_KV, HEAD_DIM)
    nsa_p = (st(kc_r), st(vc_r), st(ks), st(vs), st(kw)[:, :, -keep:], st(vw)[:, :, -keep:])

    qn, qr, kc_r, vc_r, ks, vs, kw, vw, gt = _nsa_proj(
        xs, norm_g[1, 1], w_in, qkg128, jnp.tile(cos_s, (nb_s, 1)), jnp.tile(sin_s, (nb_s, 1)), gg)
    pool = lambda a: a.reshape(a.shape[0], PAGE_SIZE, KV_W)
    kc = _compress(page_table, pool(cache_k_cmp[0]), pos_k, bd1_k, bd2_k, k_gain, gg, True)
    vc = _compress(page_table, pool(cache_v_cmp[0]), pos_v, bd1_v, bd2_v, k_gain, gg, False)

    n_used = HPG * N_KV * t_s

    def q_block_diag(a):
        a = a.reshape(nb_s, t_s, N_KV, HPG, HEAD_DIM).astype(F32)
        bd = jnp.einsum("bqghd,gk->bgdhkq", a, jnp.eye(N_KV, dtype=F32)).reshape(nb_s, KV_W, n_used)
        return jnp.pad(bd, ((0, 0), (0, 0), (0, LANES - n_used))).astype(BF16)

    gt_col = gt[:, :N_KV * HPG * 3].reshape(nb_s, t_s, N_KV, HPG, 3).transpose(0, 3, 2, 1, 4)
    gt_col = jnp.pad(gt_col.reshape(nb_s, n_used, 3), ((0, 0), (0, LANES - n_used), (0, LANES - 3)))
    new_rows = lambda a: jnp.pad(a.reshape(nb_s, t_s, KV_W), ((0, 0), (0, 8 - t_s), (0, 0)))
    n_buf = cache_k_win.shape[2]
    o_s = _attn_sample(
        page_table, q_block_diag(qn), q_block_diag(qr), gt_col, kc, vc,
        cache_k_win[0].reshape(nb_s, n_buf, KV_W), cache_v_win[0].reshape(nb_s, n_buf, KV_W),
        new_rows(ks), new_rows(vs), new_rows(kw), new_rows(vw),
        pool(cache_k_sel[0]), pool(cache_v_sel[0]), t_s, past_len)
    o = o_s[:, :n_used, :HEAD_DIM].reshape(nb_s, HPG, N_KV, t_s, HEAD_DIM).transpose(0, 3, 2, 1, 4)
    xs = _proj_res(xs, o.reshape(nb_s * t_s, d).astype(BF16), w_out)
    st = lambda a: a.reshape(1, nb_s, t_s, N_KV, HEAD_DIM)
    win = lambda buf, new: jnp.concatenate([buf, st(new)], axis=2)[:, :, -n_buf:]
    nsa_s = (st(kc_r), st(vc_r), st(ks), st(vs), win(cache_k_win, kw), win(cache_v_win, vw))

    xp = _layer_tail(xp, pp[1], *lw)
    xs = _layer_tail(xs, ps[1], *lw)

    return (xp.reshape(nb_p, t_p, d), xs.reshape(nb_s, t_s, d), conv_p[None], conv_s[None]) + nsa_p + nsa_s
```

```python
import functools

import jax
import jax.numpy as jnp
from jax import lax
from jax.experimental import pallas as pl
from jax.experimental.pallas import tpu as pltpu

F32 = jnp.float32
BF16 = jnp.bfloat16

HEAD_DIM = 64
N_KV = 4
HPG = 4
CMP_BLK = 64
SEL_BLK = 64
TOP_N = 16
N_FORCED = 3
WINDOW = 512
PAGE_SIZE = 128
CONV_W = 3
QBLK = 128
KTILE = 512
ROPE_THETA = 10000.0
RMS_EPS = 1e-6
NEG_INF = float("-inf")
MASKED = -1e30
LOG2E = 1.4426950408889634
KV_W = N_KV * HEAD_DIM
LANES = 128
MIB = 1024 * 1024


def _cparams(n_grid, vmem_mib):
    return pltpu.CompilerParams(
        dimension_semantics=("arbitrary",) * n_grid,
        vmem_limit_bytes=vmem_mib * MIB,
    )


def _rms(x, g):
    return x * lax.rsqrt(jnp.mean(x * x, axis=-1, keepdims=True) + RMS_EPS) * g


def _dot(a, b):
    return jnp.dot(a, b, preferred_element_type=F32)


def _ffn_body(x_ref, g_ref, wg_ref, wu_ref, wd_ref, o_ref, h_ref, acc_ref, *, n_f):
    j = pl.program_id(1)

    @pl.when(j == 0)
    def _():
        h_ref[...] = _rms(x_ref[...], g_ref[...]).astype(BF16)
        acc_ref[...] = jnp.zeros_like(acc_ref)

    h = h_ref[...]
    gate = _dot(h, wg_ref[...])
    up = _dot(h, wu_ref[...])
    act = (jax.nn.silu(gate) * up).astype(BF16)
    acc_ref[...] += _dot(act, wd_ref[...])

    @pl.when(j == n_f - 1)
    def _():
        o_ref[...] = x_ref[...] + 0.5 * acc_ref[...]


def _ffn(x, g, w_gu, w_down):
    m, d = x.shape
    f = w_down.shape[0]
    tm = min(m, 512)
    tf = f // 2
    n_f = f // tf
    return pl.pallas_call(
        functools.partial(_ffn_body, n_f=n_f),
        grid=(m // tm, n_f),
        in_specs=[
            pl.BlockSpec((tm, d), lambda i, j: (i, 0)),
            pl.BlockSpec((1, d), lambda i, j: (0, 0)),
            pl.BlockSpec((d, tf), lambda i, j: (0, j)),
            pl.BlockSpec((d, tf), lambda i, j: (0, n_f + j)),
            pl.BlockSpec((tf, d), lambda i, j: (j, 0)),
        ],
        out_specs=pl.BlockSpec((tm, d), lambda i, j: (i, 0)),
        out_shape=jax.ShapeDtypeStruct((m, d), F32),
        scratch_shapes=[pltpu.VMEM((tm, d), BF16), pltpu.VMEM((tm, d), F32)],
        compiler_params=_cparams(2, 48),
        name="ffn",
    )(x, g.reshape(1, d), w_gu, w_gu, w_down)


def _ple_body(x_ref, g_ref, p_ref, wg_ref, wp_ref, o_ref):
    x = x_ref[...]
    h = _rms(x, g_ref[...]).astype(BF16)
    gate = jax.nn.sigmoid(_dot(h, wg_ref[...]))
    o_ref[...] = x + gate * _dot(p_ref[...].astype(BF16), wp_ref[...])


def _ple(x, g, p, w_gate, w_proj):
    m, d = x.shape
    pd = p.shape[1]
    tm = min(m, 512)
    return pl.pallas_call(
        _ple_body,
        grid=(m // tm,),
        in_specs=[
            pl.BlockSpec((tm, d), lambda i: (i, 0)),
            pl.BlockSpec((1, d), lambda i: (0, 0)),
            pl.BlockSpec((tm, pd), lambda i: (i, 0)),
            pl.BlockSpec((d, d), lambda i: (0, 0)),
            pl.BlockSpec((pd, d), lambda i: (0, 0)),
        ],
        out_specs=pl.BlockSpec((tm, d), lambda i: (i, 0)),
        out_shape=jax.ShapeDtypeStruct((m, d), F32),
        compiler_params=_cparams(1, 32),
        name="ple",
    )(x, g.reshape(1, d), p, w_gate, w_proj)


def _proj_res_body(x_ref, o_ref, w_ref, y_ref):
    y_ref[...] = x_ref[...] + _dot(o_ref[...], w_ref[...])


def _proj_res(x, o, w):
    m, d = x.shape
    k = o.shape[1]
    tm = min(m, 512)
    return pl.pallas_call(
        _proj_res_body,
        grid=(m // tm,),
        in_specs=[
            pl.BlockSpec((tm, d), lambda i: (i, 0)),
            pl.BlockSpec((tm, k), lambda i: (i, 0)),
            pl.BlockSpec((k, d), lambda i: (0, 0)),
        ],
        out_specs=pl.BlockSpec((tm, d), lambda i: (i, 0)),
        out_shape=jax.ShapeDtypeStruct((m, d), F32),
        compiler_params=_cparams(1, 32),
        name="proj_res",
    )(x, o, w)


def _conv_body(x_ref, prev_ref, g_ref, win_ref, wc_ref, wout_ref, y_ref, st_ref, u_ref,
               *, shift, halo, tm, d):
    t = pl.program_id(1)
    lo = halo - 2 * shift

    @pl.when(t == 0)
    def _():
        u_ref[lo:halo, :] = prev_ref[0]

    x = x_ref[0]
    h = _rms(x, g_ref[...]).astype(BF16)
    proj = _dot(h, win_ref[...])
    b_gate = proj[:, :d]
    u = proj[:, d:2 * d] * proj[:, 2 * d:]
    u_ref[halo:halo + tm, :] = u
    wc = wc_ref[...]
    y = (wc[0:1, :] * u_ref[lo:lo + tm, :]
         + wc[1:2, :] * u_ref[lo + shift:lo + shift + tm, :]
         + wc[2:3, :] * u)
    y_ref[0] = x + _dot((b_gate * y).astype(BF16), wout_ref[...])
    tail = u_ref[lo + tm:halo + tm, :]
    st_ref[0] = tail
    u_ref[lo:halo, :] = tail


def _conv_mixer(x, prev, g, w_in, w_conv, w_out, shift):
    nb, t, d = x.shape
    tm = min(t, 512)
    halo = -(-2 * shift // 8) * 8
    return pl.pallas_call(
        functools.partial(_conv_body, shift=shift, halo=halo, tm=tm, d=d),
        grid=(nb, t // tm),
        in_specs=[
            pl.BlockSpec((1, tm, d), lambda b, i: (b, i, 0)),
            pl.BlockSpec((1, 2 * shift, d), lambda b, i: (b, 0, 0)),
            pl.BlockSpec((1, d), lambda b, i: (0, 0)),
            pl.BlockSpec((d, 3 * d), lambda b, i: (0, 0)),
            pl.BlockSpec((CONV_W, d), lambda b, i: (0, 0)),
            pl.BlockSpec((d, d), lambda b, i: (0, 0)),
        ],
        out_specs=[
            pl.BlockSpec((1, tm, d), lambda b, i: (b, i, 0)),
            pl.BlockSpec((1, 2 * shift, d), lambda b, i: (b, 0, 0)),
        ],
        out_shape=[
            jax.ShapeDtypeStruct((nb, t, d), F32),
            jax.ShapeDtypeStruct((nb, 2 * shift, d), F32),
        ],
        scratch_shapes=[pltpu.VMEM((halo + tm, d), F32)],
        compiler_params=_cparams(2, 48),
        name="conv_mixer",
    )(x, prev, g.reshape(1, d), w_in, w_conv, w_out)


def _head_norm(xc, gain, gg):
    sq = xc * xc
    hi = sq.astype(BF16)
    lo = (sq - hi.astype(F32)).astype(BF16)
    ss = _dot(jnp.concatenate([hi, lo], axis=1), gg)
    return xc * lax.rsqrt(ss * (1.0 / HEAD_DIM) + RMS_EPS) * gain


def _nsa_proj_body(x_ref, g_ref, w_ref, qkg_ref, cos_ref, sin_ref, gg_ref,
                   qn_ref, qr_ref, kc_ref, vc_ref, ks_ref, vs_ref, kw_ref, vw_ref, gt_ref,
                   *, tm, d):
    h = _rms(x_ref[...], g_ref[...]).astype(BF16)
    proj = _dot(h, w_ref[...])
    cos = cos_ref[...]
    sin = sin_ref[...]
    gg = gg_ref[...]
    lane = lax.broadcasted_iota(jnp.int32, (tm, LANES), 1)
    first_half = (lane % HEAD_DIM) < (HEAD_DIM // 2)
    scale = HEAD_DIM ** -0.5 * LOG2E

    def rope(xc):
        swapped = jnp.where(first_half, pltpu.roll(xc, LANES - HEAD_DIM // 2, 1),
                            pltpu.roll(xc, HEAD_DIM // 2, 1))
        return xc * cos + swapped * sin

    def slab(c):
        return proj[:, c * LANES:(c + 1) * LANES]

    n_q = d // LANES
    for c in range(n_q):
        qc = _head_norm(slab(c), qkg_ref[0:1, :], gg)
        qn_ref[:, c * LANES:(c + 1) * LANES] = (qc * scale).astype(BF16)
        qr_ref[:, c * LANES:(c + 1) * LANES] = (rope(qc) * scale).astype(BF16)
    for c in range(2):
        kc_ref[:, c * LANES:(c + 1) * LANES] = slab(n_q + c)
        vc_ref[:, c * LANES:(c + 1) * LANES] = slab(n_q + 2 + c)
        ks_ref[:, c * LANES:(c + 1) * LANES] = rope(_head_norm(slab(n_q + 4 + c), qkg_ref[2:3, :], gg))
        vs_ref[:, c * LANES:(c + 1) * LANES] = slab(n_q + 6 + c)
        kw_ref[:, c * LANES:(c + 1) * LANES] = rope(_head_norm(slab(n_q + 8 + c), qkg_ref[3:4, :], gg))
        vw_ref[:, c * LANES:(c + 1) * LANES] = slab(n_q + 10 + c)
    gt_ref[...] = jax.nn.sigmoid(slab(n_q + 12))


def _nsa_proj(x, g, w_in_pad, qkg128, cos128, sin128, gg):
    m, d = x.shape
    n = w_in_pad.shape[1]
    tm = min(m, 512)
    row = lambda w: pl.BlockSpec((tm, w), lambda i: (i, 0))
    full = lambda a: pl.BlockSpec(a.shape, lambda i: (0,) * a.ndim)
    return pl.pallas_call(
        functools.partial(_nsa_proj_body, tm=tm, d=d),
        grid=(m // tm,),
        in_specs=[row(d), pl.BlockSpec((1, d), lambda i: (0, 0)), pl.BlockSpec((d, n), lambda i: (0, 0)),
                  full(qkg128), row(LANES), row(LANES), full(gg)],
        out_specs=[row(d), row(d)] + [row(KV_W)] * 6 + [row(LANES)],
        out_shape=[jax.ShapeDtypeStruct((m, d), BF16)] * 2
        + [jax.ShapeDtypeStruct((m, KV_W), F32)] * 6
        + [jax.ShapeDtypeStruct((m, LANES), F32)],
        compiler_params=_cparams(1, 48),
        name="nsa_proj",
    )(x, g.reshape(1, d), w_in_pad, qkg128, cos128, sin128, gg)


def _page_copy(src_ref, buf_ref, sem, slot, paged, index, p, dst_page):
    src = src_ref.at[index] if paged else src_ref.at[index, :, pl.ds(p * PAGE_SIZE, PAGE_SIZE)]
    return pltpu.make_async_copy(src, buf_ref.at[slot, pl.ds(dst_page * KV_W, KV_W)], sem.at[slot])


def _gather_start(pt_ref, src_ref, buf_ref, sem, step, slot, n_pages, n_seq, paged):
    for s in range(n_seq):
        seq = step * n_seq + s
        for p in range(n_pages):
            index = pt_ref[seq, p] if paged else seq
            _page_copy(src_ref, buf_ref, sem, slot, paged, index, p, s * n_pages + p).start()


def _gather_wait(src_ref, buf_ref, sem, slot, n_pages, n_seq, paged):
    for s in range(n_seq):
        for p in range(n_pages):
            _page_copy(src_ref, buf_ref, sem, slot, paged, 0, p, s * n_pages + p).wait()


def _compress_body(pt_ref, src_ref, pos_ref, wd_ref, bd2_ref, gain_ref, gg_ref, o_ref,
                   buf_ref, sem, acc_ref, *, n_pages, n_seq, paged, normalize):
    step = pl.program_id(0)
    n_steps = pl.num_programs(0)
    slot = step % 2
    m = n_seq * n_pages
    unroll = 8

    @pl.when(step == 0)
    def _():
        _gather_start(pt_ref, src_ref, buf_ref, sem, 0, 0, n_pages, n_seq, paged)

    @pl.when(step + 1 < n_steps)
    def _():
        _gather_start(pt_ref, src_ref, buf_ref, sem, step + 1, 1 - slot, n_pages, n_seq, paged)

    _gather_wait(src_ref, buf_ref, sem, slot, n_pages, n_seq, paged)
    acc_ref[...] = jnp.zeros_like(acc_ref)

    def chunk(c, carry):
        accs = [acc_ref[g] for g in range(N_KV)]
        for dd in range(unroll):
            d = c * unroll + dd
            w = wd_ref[d]
            pos = pos_ref[pl.ds(d, 1), :]
            for g in range(N_KV):
                rows = buf_ref[slot, pl.ds(g * HEAD_DIM + d, m, stride=KV_W), :] + pos
                accs[g] = accs[g] + _dot(rows.astype(BF16), w)
        for g in range(N_KV):
            acc_ref[g] = accs[g]
        return carry

    lax.fori_loop(0, HEAD_DIM // unroll, chunk, 0)

    for g in range(N_KV):
        y = _dot(jax.nn.gelu(acc_ref[g]).astype(BF16), bd2_ref[...])
        if normalize:
            y = _head_norm(y, gain_ref[...], gg_ref[...])
        for s in range(n_seq):
            o_ref[s, g] = y[s * n_pages:(s + 1) * n_pages].astype(BF16)


def _compress(page_table, src, pos_t, wd, bd2, gain128, gg, normalize, paged, n_seq):
    nb, n_pages = page_table.shape
    hid2 = wd.shape[2]
    full = lambda a: pl.BlockSpec(a.shape, lambda i, pt: (0,) * a.ndim)
    grid_spec = pltpu.PrefetchScalarGridSpec(
        num_scalar_prefetch=1,
        grid=(nb // n_seq,),
        in_specs=[pl.BlockSpec(memory_space=pl.ANY), full(pos_t), full(wd), full(bd2), full(gain128), full(gg)],
        out_specs=pl.BlockSpec((n_seq, N_KV, n_pages, LANES), lambda i, pt: (i, 0, 0, 0)),
        scratch_shapes=[
            pltpu.VMEM((2, n_seq * n_pages * KV_W, PAGE_SIZE), F32),
            pltpu.SemaphoreType.DMA((2,)),
            pltpu.VMEM((N_KV, n_seq * n_pages, hid2), F32),
        ],
    )
    return pl.pallas_call(
        functools.partial(_compress_body, n_pages=n_pages, n_seq=n_seq, paged=paged, normalize=normalize),
        grid_spec=grid_spec,
        out_shape=jax.ShapeDtypeStruct((nb, N_KV, n_pages, LANES), BF16),
        compiler_params=_cparams(1, 48),
        name="compress",
    )(page_table, src, pos_t, wd, bd2, gain128, gg)


def _softmax_cols(s):
    m = jnp.max(s, axis=0, keepdims=True)
    m = jnp.where(m == NEG_INF, 0.0, m)
    e = jnp.exp2(s - m)
    return e, jnp.sum(e, axis=0, keepdims=True)


def _select_bias(imp, cur):
    jj = lax.broadcasted_iota(jnp.int32, imp.shape, 0)
    forced = (jj == 0) | (jj == cur) | (jj == cur - 1)
    cand = (jj <= cur) & jnp.logical_not(forced)
    v = jnp.where(cand, imp, -1.0)
    bias = jnp.where(forced, 0.0, NEG_INF)
    for _ in range(TOP_N - N_FORCED):
        m = jnp.max(v, axis=0, keepdims=True)
        idx = jnp.min(jnp.where(v == m, jj, imp.shape[0]), axis=0, keepdims=True)
        idx = jnp.where(m >= 0.0, idx, -1)
        pick = jj == idx
        bias = jnp.where(pick, 0.0, bias)
        v = jnp.where(pick, -1.0, v)
    return bias


def _attn_prompt_body(qn_ref, qr_ref, gt_ref, kc_ref, vct_ref, ks_ref, blk_ref, vst_ref, kw_ref, vwt_ref,
                      o_ref, qsel_ref, sa_ref, sb_ref):
    i = pl.program_id(2)
    q_start = i * QBLK
    n_col = HPG * QBLK
    qn = qn_ref[...]
    qr = qr_ref[...]
    q_pos = q_start + lax.broadcasted_iota(jnp.int32, (1, n_col), 1) % QBLK

    n_cmp = kc_ref.shape[0]
    s = _dot(kc_ref[...], qn)
    cmp_end = lax.broadcasted_iota(jnp.int32, (n_cmp, 1), 0) * CMP_BLK + (CMP_BLK - 1)
    s = jnp.where(cmp_end <= q_pos, s, NEG_INF)
    e, l = _softmax_cols(s)
    p_cmp = e / jnp.maximum(l, 1e-30)
    o_cmp = _dot(vct_ref[...], p_cmp.astype(BF16))

    imp = p_cmp[:, 0:QBLK]
    for hh in range(1, HPG):
        imp = imp + p_cmp[:, hh * QBLK:(hh + 1) * QBLK]
    bias = jnp.maximum(_select_bias(imp, q_pos[:, 0:QBLK] // SEL_BLK), MASKED)
    own = lax.broadcasted_iota(jnp.int32, (n_cmp, QBLK), 0) >= q_start // SEL_BLK
    bias = jnp.where(own, MASKED, bias)
    qsel_ref[0:HEAD_DIM, :] = qr
    qsel_ref[HEAD_DIM:LANES, :] = jnp.zeros((LANES - HEAD_DIM, n_col), BF16)
    qsel_ref[LANES:LANES + n_cmp, :] = jnp.concatenate([bias.astype(BF16)] * HPG, axis=1)
    if n_cmp < LANES:
        qsel_ref[LANES + n_cmp:, :] = jnp.zeros((LANES - n_cmp, n_col), BF16)

    n_win = WINDOW + QBLK
    w0 = pl.multiple_of(q_start, QBLK)
    s = _dot(kw_ref[pl.ds(w0, n_win), :], qr)
    r = lax.broadcasted_iota(jnp.int32, (n_win, QBLK), 0)
    q = lax.broadcasted_iota(jnp.int32, (n_win, QBLK), 1)
    valid = (r > q) & (r <= q + WINDOW) & (r + q_start >= WINDOW)
    wbias = jnp.where(valid, 0.0, NEG_INF)
    s = s + jnp.concatenate([wbias] * HPG, axis=1)
    e, l = _softmax_cols(s)
    o_win = _dot(vwt_ref[:, pl.ds(w0, n_win)], e.astype(BF16)) / jnp.maximum(l, 1e-30)

    s = _dot(ks_ref[pl.ds(w0, QBLK), :], qsel_ref[0:LANES, :])
    tri = jnp.where(lax.broadcasted_iota(jnp.int32, (QBLK, QBLK), 0)
                    <= lax.broadcasted_iota(jnp.int32, (QBLK, QBLK), 1), 0.0, NEG_INF)
    s = s + jnp.concatenate([tri] * HPG, axis=1)
    m0 = jnp.max(s, axis=0, keepdims=True)
    p = jnp.exp2(s - m0)
    init = (m0, jnp.sum(p, axis=0, keepdims=True), _dot(vst_ref[:, pl.ds(w0, QBLK)], p.astype(BF16)))

    n_tiles = (q_start + KTILE - 1) // KTILE
    max_tile = ks_ref.shape[0] // KTILE - 1

    def scores(t):
        k0 = pl.multiple_of(t * KTILE, KTILE)
        keys = jnp.concatenate([ks_ref[pl.ds(k0, KTILE), :], blk_ref[pl.ds(k0, KTILE), :]], axis=1)
        return _dot(keys, qsel_ref[...])

    def sel_tile(s_buf, t, carry):
        m, l, acc = carry
        k0 = pl.multiple_of(t * KTILE, KTILE)
        s = s_buf[...]
        m_new = jnp.maximum(m, jnp.max(s, axis=0, keepdims=True))
        alpha = jnp.exp2(m - m_new)
        p = jnp.exp2(s - m_new)
        l = alpha * l + jnp.sum(p, axis=0, keepdims=True)
        acc = alpha * acc + _dot(vst_ref[:, pl.ds(k0, KTILE)], p.astype(BF16))
        return m_new, l, acc

    def tile_pair(u, carry):
        t = 2 * u
        sb_ref[...] = scores(t + 1)
        carry = sel_tile(sa_ref, t, carry)
        sa_ref[...] = scores(jnp.minimum(t + 2, max_tile))
        return sel_tile(sb_ref, t + 1, carry)

    sa_ref[...] = scores(0)
    carry = lax.fori_loop(0, n_tiles // 2, tile_pair, init)
    _, l_sel, acc_sel = lax.fori_loop(0, n_tiles % 2, lambda _, c: sel_tile(sa_ref, n_tiles - 1, c), carry)
    o_sel = acc_sel / l_sel

    gt = gt_ref[...]
    o_ref[...] = (gt[0:1, :] * o_cmp + gt[1:2, :] * o_sel + gt[2:3, :] * o_win).astype(BF16)


def _attn_prompt(qn_t, qr_t, gt_t, kc, vc_t, ks_pad, blk_onehot, vs_t, kw_pad, vw_pad_t):
    nb, ng, nqb, hd, n_col = qn_t.shape
    n_cmp = kc.shape[2]
    assert n_cmp <= LANES
    per_q = lambda r: pl.BlockSpec((None, None, None, r, n_col), lambda b, g, i: (b, g, i, 0, 0))
    per_bg = lambda a: pl.BlockSpec((None, None) + a.shape[2:], lambda b, g, i: (b, g, 0, 0))
    return pl.pallas_call(
        _attn_prompt_body,
        grid=(nb, ng, nqb),
        in_specs=[per_q(hd), per_q(hd), per_q(8), per_bg(kc), per_bg(vc_t), per_bg(ks_pad),
                  pl.BlockSpec(blk_onehot.shape, lambda b, g, i: (0, 0)), per_bg(vs_t),
                  per_bg(kw_pad), per_bg(vw_pad_t)],
        out_specs=per_q(hd),
        out_shape=jax.ShapeDtypeStruct((nb, ng, nqb, hd, n_col), BF16),
        scratch_shapes=[pltpu.VMEM((2 * LANES, n_col), BF16), pltpu.VMEM((KTILE, n_col), F32),
                        pltpu.VMEM((KTILE, n_col), F32)],
        compiler_params=_cparams(3, 48),
        name="attn_prompt",
    )(qn_t, qr_t, gt_t, kc, vc_t, ks_pad, blk_onehot, vs_t, kw_pad, vw_pad_t)


def _attn_sample_body(pt_ref, qn_ref, qr_ref, gt_ref, kc_ref, vc_ref, kwin_ref, vwin_ref,
                      ksn_ref, vsn_ref, kwn_ref, vwn_ref, spread_ref, kpool_ref, vpool_ref, o_ref,
                      kbuf_ref, vbuf_ref, ksem, vsem, s_ref, *, n_pages, n_new, past_len):
    b = pl.program_id(0)
    nb = pl.num_programs(0)
    slot = b % 2

    @pl.when(b == 0)
    def _():
        _gather_start(pt_ref, kpool_ref, kbuf_ref, ksem, 0, 0, n_pages, 1, True)
        _gather_start(pt_ref, vpool_ref, vbuf_ref, vsem, 0, 0, n_pages, 1, True)

    @pl.when(b + 1 < nb)
    def _():
        _gather_start(pt_ref, kpool_ref, kbuf_ref, ksem, b + 1, 1 - slot, n_pages, 1, True)
        _gather_start(pt_ref, vpool_ref, vbuf_ref, vsem, b + 1, 1 - slot, n_pages, 1, True)

    qn = qn_ref[0]
    qr = qr_ref[0]
    n_col = qr.shape[0]
    n_past = n_pages * PAGE_SIZE
    n_blk = n_past // SEL_BLK
    n_gq = N_KV * n_new
    dot_nt = lambda a, bt: lax.dot_general(a, bt, (((1,), (1,)), ((), ())), preferred_element_type=F32)

    lane = lax.broadcasted_iota(jnp.int32, (1, LANES), 1)
    lane_pos = past_len + lane % n_new
    s = _dot(kc_ref[0], qn)
    cmp_end = lax.broadcasted_iota(jnp.int32, (n_blk, 1), 0) * CMP_BLK + (CMP_BLK - 1)
    s = jnp.where(cmp_end <= lane_pos, s, NEG_INF)
    e, l = _softmax_cols(s)
    p_cmp = e / jnp.maximum(l, 1e-30)
    o_cmp = _dot(p_cmp.T.astype(BF16), vc_ref[0])[:n_col]

    imp = p_cmp
    for hh in range(1, HPG):
        imp = imp + pltpu.roll(p_cmp, LANES - hh * n_gq, 1)
    sel = _select_bias(imp, lane_pos // SEL_BLK)
    sel01 = jnp.where((lane < n_gq) & (sel == 0.0), 1.0, 0.0)
    spread = sel01
    for hh in range(1, HPG):
        spread = spread + pltpu.roll(sel01, hh * n_gq, 1)
    sel_bias = jnp.where(spread.T[:n_col] > 0.5, 0.0, MASKED).astype(BF16)

    row = lax.broadcasted_iota(jnp.int32, (n_col, 1), 0)
    q_idx = row % n_new
    q_pos = past_len + q_idx
    n_pad = ksn_ref.shape[1]
    new_idx = lax.broadcasted_iota(jnp.int32, (1, n_pad), 1)
    new_ok = (new_idx <= q_idx) & (new_idx < n_new)

    _gather_wait(kpool_ref, kbuf_ref, ksem, slot, n_pages, 1, True)
    _gather_wait(vpool_ref, vbuf_ref, vsem, slot, n_pages, 1, True)
    pages_per_chunk = 8
    chunk = pages_per_chunk * PAGE_SIZE

    def page_rows(c, pp):
        return pl.ds(pl.multiple_of((c * pages_per_chunk + pp) * KV_W, KV_W), KV_W)

    def score_chunk(c, m):
        k0 = pl.multiple_of(c * chunk, chunk)
        s = jnp.concatenate([_dot(qr, kbuf_ref[slot, page_rows(c, pp), :].astype(BF16))
                             for pp in range(pages_per_chunk)], axis=1)
        s = s + _dot(sel_bias, spread_ref[:, pl.ds(k0, chunk)])
        s_ref[:, pl.ds(k0, chunk)] = s
        return jnp.maximum(m, jnp.max(s, axis=1, keepdims=True))

    m = lax.fori_loop(0, n_pages // pages_per_chunk, score_chunk, jnp.full((n_col, 1), NEG_INF, F32))
    s_new = jnp.where(new_ok, dot_nt(qr, ksn_ref[0].astype(BF16)), NEG_INF)
    m = jnp.maximum(m, jnp.max(s_new, axis=1, keepdims=True))
    m = jnp.where(m == NEG_INF, 0.0, m)

    def value_chunk(c, carry):
        l, acc = carry
        k0 = pl.multiple_of(c * chunk, chunk)
        p = jnp.exp2(s_ref[:, pl.ds(k0, chunk)] - m)
        l = l + jnp.sum(p, axis=1, keepdims=True)
        p = p.astype(BF16)
        for pp in range(pages_per_chunk):
            acc = acc + dot_nt(p[:, pp * PAGE_SIZE:(pp + 1) * PAGE_SIZE],
                               vbuf_ref[slot, page_rows(c, pp), :].astype(BF16))
        return l, acc

    p_new = jnp.exp2(s_new - m)
    l_sel, acc = lax.fori_loop(
        0, n_pages // pages_per_chunk, value_chunk,
        (jnp.sum(p_new, axis=1, keepdims=True), _dot(p_new.astype(BF16), vsn_ref[0].astype(BF16))))
    o_sel = acc / jnp.maximum(l_sel, 1e-30)

    n_buf = kwin_ref.shape[2]
    s_buf = _dot(qr, kwin_ref[0].astype(BF16))
    buf_pos = past_len - n_buf + lax.broadcasted_iota(jnp.int32, (1, n_buf), 1)
    diff = q_pos - buf_pos
    s_buf = jnp.where((diff >= 0) & (diff < WINDOW) & (buf_pos >= 0), s_buf, NEG_INF)
    s_nw = jnp.where(new_ok, dot_nt(qr, kwn_ref[0].astype(BF16)), NEG_INF)
    m = jnp.maximum(jnp.max(s_buf, axis=1, keepdims=True), jnp.max(s_nw, axis=1, keepdims=True))
    m = jnp.where(m == NEG_INF, 0.0, m)
    p_buf = jnp.exp2(s_buf - m)
    p_nw = jnp.exp2(s_nw - m)
    l_win = jnp.sum(p_buf, axis=1, keepdims=True) + jnp.sum(p_nw, axis=1, keepdims=True)
    o_win = (dot_nt(p_buf.astype(BF16), vwin_ref[0].astype(BF16))
             + _dot(p_nw.astype(BF16), vwn_ref[0].astype(BF16))) / jnp.maximum(l_win, 1e-30)

    gt = gt_ref[0]
    o = gt[:, 0:1] * o_cmp + gt[:, 1:2] * o_sel + gt[:, 2:3] * o_win
    row_g = (lax.broadcasted_iota(jnp.int32, (n_col, KV_W), 0) // n_new) % N_KV
    lane_g = lax.broadcasted_iota(jnp.int32, (n_col, KV_W), 1) // HEAD_DIM
    o = jnp.where(row_g == lane_g, o, 0.0)
    z = o[:, :LANES] + o[:, LANES:]
    o_ref[0] = z + pltpu.roll(z, HEAD_DIM, 1)


def _attn_sample(page_table, qn_bd, qr_bd_t, gt_col, kc, vc, kwin_t, vwin_t, ks_new, vs_new, kw_new, vw_new,
                 blk_spread, k_pool, v_pool, n_new, past_len):
    nb, n_pages = page_table.shape
    n_past = n_pages * PAGE_SIZE
    n_col = qr_bd_t.shape[1]
    per_b = lambda a: pl.BlockSpec((1,) + a.shape[1:], lambda b, pt: (b,) + (0,) * (a.ndim - 1))
    any_spec = pl.BlockSpec(memory_space=pl.ANY)
    ins = (qn_bd, qr_bd_t, gt_col, kc, vc, kwin_t, vwin_t, ks_new, vs_new, kw_new, vw_new)
    grid_spec = pltpu.PrefetchScalarGridSpec(
        num_scalar_prefetch=1,
        grid=(nb,),
        in_specs=[per_b(a) for a in ins]
        + [pl.BlockSpec(blk_spread.shape, lambda b, pt: (0, 0)), any_spec, any_spec],
        out_specs=pl.BlockSpec((1, n_col, LANES), lambda b, pt: (b, 0, 0)),
        scratch_shapes=[
            pltpu.VMEM((2, n_pages * KV_W, PAGE_SIZE), F32),
            pltpu.VMEM((2, n_pages * KV_W, PAGE_SIZE), F32),
            pltpu.SemaphoreType.DMA((2,)),
            pltpu.SemaphoreType.DMA((2,)),
            pltpu.VMEM((n_col, n_past), F32),
        ],
    )
    return pl.pallas_call(
        functools.partial(_attn_sample_body, n_pages=n_pages, n_new=n_new, past_len=past_len),
        grid_spec=grid_spec,
        out_shape=jax.ShapeDtypeStruct((nb, n_col, LANES), F32),
        compiler_params=_cparams(1, 56),
        name="attn_sample",
    )(page_table, *ins, blk_spread, k_pool, v_pool)


def _rope_tables(pos):
    half = HEAD_DIM // 2
    inv = ROPE_THETA ** (-jnp.arange(half, dtype=F32) / half)
    ang = pos.astype(F32)[:, None] * inv
    cos, sin = jnp.cos(ang), jnp.sin(ang)
    reps = LANES // HEAD_DIM
    return (jnp.tile(jnp.concatenate([cos, cos], axis=1), (1, reps)),
            jnp.tile(jnp.concatenate([-sin, sin], axis=1), (1, reps)))


def _group_sum_matrix():
    lane = jnp.arange(LANES)
    g = (lane[:, None] // HEAD_DIM == lane[None, :] // HEAD_DIM).astype(BF16)
    return jnp.concatenate([g, g], axis=0)


def _compress_weights(cmp_pos, cmp_w1, cmp_w2):
    n_pair = PAGE_SIZE // CMP_BLK
    eye = jnp.eye(n_pair, dtype=F32)
    hid = cmp_w1.shape[1]
    w1 = cmp_w1.reshape(CMP_BLK, HEAD_DIM, hid)
    wd = jnp.einsum("bk,rdh->dbrkh", eye, w1).reshape(HEAD_DIM, PAGE_SIZE, n_pair * hid).astype(BF16)
    bd2 = jnp.einsum("bk,hd->bhkd", eye, cmp_w2).reshape(n_pair * hid, n_pair * HEAD_DIM).astype(BF16)
    return jnp.tile(cmp_pos.T, (1, n_pair)), wd, bd2


def _to_rows_major(a_t):
    nb, _, rows = a_t.shape
    return a_t.reshape(nb, N_KV, HEAD_DIM, rows).transpose(0, 3, 1, 2)[None]


def _from_rows_major(a):
    n, rows = a.shape[:2]
    return a.transpose(0, 2, 3, 1).reshape(n, KV_W, rows)


def _layer_tail(x, p, norm_g, w_gu, w_down, w_ple_proj, w_ple_gate):
    x = _ffn(x, norm_g[2], w_gu[1], w_down[1])
    return _ple(x, norm_g[3], p, w_ple_gate, w_ple_proj)


def kernel(x_prompt, x_sample, state_conv, cache_k_cmp, cache_v_cmp, cache_k_sel, cache_v_sel,
           cache_k_win, cache_v_win, page_table, p_prompt, p_sample, norm_g, ffn_w_gu, ffn_w_down,
           ple_w_proj, ple_w_gate, conv_w_in, conv_w, conv_w_out, nsa_w_in, nsa_qk_g, nsa_cmp_pos,
           nsa_cmp_w1, nsa_cmp_w2, nsa_w_out):
    nb_p, t_p, d = x_prompt.shape
    nb_s, t_s, _ = x_sample.shape
    ple_dim = p_prompt.shape[-1]
    past_len = page_table.shape[1] * PAGE_SIZE
    assert t_p % KTILE == 0 and t_p % PAGE_SIZE == 0 and d % LANES == 0

    w_gu = ffn_w_gu.astype(BF16)
    w_down = ffn_w_down.astype(BF16)
    w_pp = ple_w_proj.astype(BF16)
    w_pg = ple_w_gate.astype(BF16)

    xp = x_prompt.reshape(nb_p * t_p, d)
    xs = x_sample.reshape(nb_s * t_s, d)
    pp = p_prompt.reshape(-1, nb_p * t_p, ple_dim)
    ps = p_sample.reshape(-1, nb_s * t_s, ple_dim)

    lw = (norm_g[0], w_gu[0], w_down[0], w_pp[0], w_pg[0])
    cw = (conv_w_in[0].astype(BF16), conv_w[0], conv_w_out[0].astype(BF16))
    xp = _ffn(xp, norm_g[0, 0], w_gu[0, 0], w_down[0, 0])
    xs = _ffn(xs, norm_g[0, 0], w_gu[0, 0], w_down[0, 0])
    xp3, conv_p = _conv_mixer(xp.reshape(nb_p, t_p, d), jnp.zeros((nb_p, CONV_W - 1, d), F32),
                              norm_g[0, 1], *cw, shift=1)
    xs_tb = xs.reshape(nb_s, t_s, d).transpose(1, 0, 2).reshape(1, t_s * nb_s, d)
    prev_tb = state_conv[0].transpose(1, 0, 2).reshape(1, (CONV_W - 1) * nb_s, d)
    xs3, conv_s = _conv_mixer(xs_tb, prev_tb, norm_g[0, 1], *cw, shift=nb_s)
    xs = xs3.reshape(t_s, nb_s, d).transpose(1, 0, 2).reshape(nb_s * t_s, d)
    conv_s = conv_s.reshape(CONV_W - 1, nb_s, d).transpose(1, 0, 2)
    xp = _layer_tail(xp3.reshape(nb_p * t_p, d), pp[0], *lw)
    xs = _layer_tail(xs, ps[0], *lw)

    lw = (norm_g[1], w_gu[1], w_down[1], w_pp[1], w_pg[1])
    xp = _ffn(xp, norm_g[1, 0], w_gu[1, 0], w_down[1, 0])
    xs = _ffn(xs, norm_g[1, 0], w_gu[1, 0], w_down[1, 0])

    n_in = nsa_w_in.shape[-1]
    n_in_pad = -(-n_in // LANES) * LANES
    w_in = jnp.pad(nsa_w_in[0], ((0, 0), (0, n_in_pad - n_in))).astype(BF16)
    w_out = nsa_w_out[0].astype(BF16)
    qkg128 = jnp.tile(nsa_qk_g[0], (1, LANES // HEAD_DIM))
    gg = _group_sum_matrix()
    cos_p, sin_p = _rope_tables(jnp.arange(t_p))
    cos_s, sin_s = _rope_tables(past_len + jnp.arange(t_s))
    pos_k, wd_k, bd2_k = _compress_weights(nsa_cmp_pos[0, 0], nsa_cmp_w1[0, 0], nsa_cmp_w2[0, 0])
    pos_v, wd_v, bd2_v = _compress_weights(nsa_cmp_pos[0, 1], nsa_cmp_w1[0, 1], nsa_cmp_w2[0, 1])
    k_gain = qkg128[1:2]

    qn, qr, kc_r, vc_r, ks, vs, kw, vw, gt = _nsa_proj(
        xp, norm_g[1, 1], w_in, qkg128, jnp.tile(cos_p, (nb_p, 1)), jnp.tile(sin_p, (nb_p, 1)), gg)
    pt_p = jnp.zeros((nb_p, t_p // PAGE_SIZE), jnp.int32)
    rows_last = lambda a: a.reshape(nb_p, t_p, KV_W).transpose(0, 2, 1)
    as_blocks = lambda c: c.reshape(c.shape[0], N_KV, -1, HEAD_DIM)
    kc = as_blocks(_compress(pt_p, rows_last(kc_r), pos_k, wd_k, bd2_k, k_gain, gg, True, False, nb_p))
    vc = as_blocks(_compress(pt_p, rows_last(vc_r), pos_v, wd_v, bd2_v, k_gain, gg, False, False, nb_p))

    nqb = t_p // QBLK
    n_col = HPG * QBLK

    def q_cols(a):
        a = a.reshape(nb_p, nqb, QBLK, N_KV, HPG, HEAD_DIM).transpose(0, 3, 1, 5, 4, 2)
        return a.reshape(nb_p, N_KV, nqb, HEAD_DIM, n_col)

    def by_group(a):
        return a.reshape(nb_p, -1, N_KV, HEAD_DIM).transpose(0, 2, 1, 3)

    gt_t = gt[:, :N_KV * HPG * 3].reshape(nb_p, nqb, QBLK, N_KV, HPG, 3).transpose(0, 3, 1, 5, 4, 2)
    gt_t = jnp.pad(gt_t.reshape(nb_p, N_KV, nqb, 3, n_col), ((0, 0),) * 3 + ((0, 5), (0, 0)))
    front = ((0, 0), (0, 0), (WINDOW, 0), (0, 0))
    lane_pad = ((0, 0), (0, 0), (0, 0), (0, LANES - HEAD_DIM))
    blk_onehot = (jnp.arange(t_p)[:, None] // SEL_BLK == jnp.arange(LANES)[None, :]).astype(BF16)
    o_t = _attn_prompt(
        q_cols(qn), q_cols(qr), gt_t,
        kc, vc.transpose(0, 1, 3, 2),
        jnp.pad(by_group(ks).astype(BF16), lane_pad), blk_onehot,
        by_group(vs).astype(BF16).transpose(0, 1, 3, 2),
        jnp.pad(by_group(kw).astype(BF16), front),
        jnp.pad(by_group(vw).astype(BF16), front).transpose(0, 1, 3, 2))
    o = o_t.reshape(nb_p, N_KV, nqb, HEAD_DIM, HPG, QBLK).transpose(0, 2, 5, 1, 4, 3)
    xp = _proj_res(xp, o.reshape(nb_p * t_p, d), w_out)
    keep = min(WINDOW, t_p)
    st = lambda a: a.reshape(1, nb_p, t_p, N_KV, HEAD_DIM)
    nsa_p = (st(kc_r), st(vc_r), st(ks), st(vs), st(kw)[:, :, -keep:], st(vw)[:, :, -keep:])

    qn, qr, kc_r, vc_r, ks, vs, kw, vw, gt = _nsa_proj(
        xs, norm_g[1, 1], w_in, qkg128, jnp.tile(cos_s, (nb_s, 1)), jnp.tile(sin_s, (nb_s, 1)), gg)
    pool_t = lambda c: _from_rows_major(c[0])
    n_seq = 2 if nb_s % 2 == 0 else 1
    by_block = lambda c: as_blocks(c).transpose(0, 2, 1, 3).reshape(nb_s, -1, KV_W)
    kc = by_block(_compress(page_table, pool_t(cache_k_cmp), pos_k, wd_k, bd2_k, k_gain, gg, True, True, n_seq))
    vc = by_block(_compress(page_table, pool_t(cache_v_cmp), pos_v, wd_v, bd2_v, k_gain, gg, False, True, n_seq))

    n_used = HPG * N_KV * t_s

    def q_block_diag(a):
        a = a.reshape(nb_s, t_s, N_KV, HPG, HEAD_DIM).astype(F32)
        return jnp.einsum("bqghd,gk->bgdhkq", a, jnp.eye(N_KV, dtype=F32)).reshape(nb_s, KV_W, n_used)

    qn_bd = jnp.pad(q_block_diag(qn), ((0, 0), (0, 0), (0, LANES - n_used))).astype(BF16)
    qr_bd_t = q_block_diag(qr).transpose(0, 2, 1).astype(BF16)
    gt_col = gt[:, :N_KV * HPG * 3].reshape(nb_s, t_s, N_KV, HPG, 3).transpose(0, 3, 2, 1, 4)
    gt_col = jnp.pad(gt_col.reshape(nb_s, n_used, 3), ((0, 0), (0, 0), (0, LANES - 3)))
    new_rows = lambda a: jnp.pad(a.reshape(nb_s, t_s, KV_W), ((0, 0), (0, 16 - t_s), (0, 0)))
    n_buf = cache_k_win.shape[2]
    kwin_t, vwin_t = _from_rows_major(cache_k_win[0]), _from_rows_major(cache_v_win[0])
    blk_spread = (jnp.arange(past_len // SEL_BLK)[:, None] == jnp.arange(past_len)[None, :] // SEL_BLK).astype(BF16)
    o_s = _attn_sample(
        page_table, qn_bd, qr_bd_t, gt_col, kc, vc, kwin_t, vwin_t,
        new_rows(ks), new_rows(vs), new_rows(kw), new_rows(vw), blk_spread,
        pool_t(cache_k_sel), pool_t(cache_v_sel), t_s, past_len)
    o = o_s[:, :, :HEAD_DIM].reshape(nb_s, HPG, N_KV, t_s, HEAD_DIM).transpose(0, 3, 2, 1, 4)
    xs = _proj_res(xs, o.reshape(nb_s * t_s, d).astype(BF16), w_out)
    st = lambda a: a.reshape(1, nb_s, t_s, N_KV, HEAD_DIM)
    new_t = lambda a: a.reshape(nb_s, t_s, KV_W).transpose(0, 2, 1)
    win = lambda buf_t, new: _to_rows_major(jnp.concatenate([buf_t, new_t(new)], axis=2)[:, :, -n_buf:])
    nsa_s = (st(kc_r), st(vc_r), st(ks), st(vs), win(kwin_t, kw), win(vwin_t, vw))

    xp = _layer_tail(xp, pp[1], *lw)
    xs = _layer_tail(xs, ps[1], *lw)

    return (xp.reshape(nb_p, t_p, d), xs.reshape(nb_s, t_s, d), conv_p[None], conv_s[None]) + nsa_p + nsa_s
```

```python
import functools

import jax
import jax.numpy as jnp
from jax import lax
from jax.experimental import pallas as pl
from jax.experimental.pallas import tpu as pltpu

F32 = jnp.float32
BF16 = jnp.bfloat16

HEAD_DIM = 64
N_KV = 4
HPG = 4
CMP_BLK = 64
SEL_BLK = 64
TOP_N = 16
N_FORCED = 3
WINDOW = 512
PAGE_SIZE = 128
CONV_W = 3
QBLK = 128
KTILE = 512
ROPE_THETA = 10000.0
RMS_EPS = 1e-6
NEG_INF = float("-inf")
MASKED = -1e30
LOG2E = 1.4426950408889634
KV_W = N_KV * HEAD_DIM
LANES = 128
MIB = 1024 * 1024


def _cparams(n_grid, vmem_mib):
    return pltpu.CompilerParams(
        dimension_semantics=("arbitrary",) * n_grid,
        vmem_limit_bytes=vmem_mib * MIB,
    )


def _rms(x, g):
    return x * lax.rsqrt(jnp.mean(x * x, axis=-1, keepdims=True) + RMS_EPS) * g


def _dot(a, b):
    return jnp.dot(a, b, preferred_element_type=F32)


def _ffn_body(x_ref, g_ref, wg_ref, wu_ref, wd_ref, o_ref, h_ref, acc_ref, *, n_f):
    j = pl.program_id(1)

    @pl.when(j == 0)
    def _():
        h_ref[...] = _rms(x_ref[...], g_ref[...]).astype(BF16)
        acc_ref[...] = jnp.zeros_like(acc_ref)

    h = h_ref[...]
    gate = _dot(h, wg_ref[...])
    up = _dot(h, wu_ref[...])
    act = (jax.nn.silu(gate) * up).astype(BF16)
    acc_ref[...] += _dot(act, wd_ref[...])

    @pl.when(j == n_f - 1)
    def _():
        o_ref[...] = x_ref[...] + 0.5 * acc_ref[...]


def _ffn(x, g, w_gu, w_down):
    m, d = x.shape
    f = w_down.shape[0]
    tm = min(m, 512)
    tf = f // 2
    n_f = f // tf
    return pl.pallas_call(
        functools.partial(_ffn_body, n_f=n_f),
        grid=(m // tm, n_f),
        in_specs=[
            pl.BlockSpec((tm, d), lambda i, j: (i, 0)),
            pl.BlockSpec((1, d), lambda i, j: (0, 0)),
            pl.BlockSpec((d, tf), lambda i, j: (0, j)),
            pl.BlockSpec((d, tf), lambda i, j: (0, n_f + j)),
            pl.BlockSpec((tf, d), lambda i, j: (j, 0)),
        ],
        out_specs=pl.BlockSpec((tm, d), lambda i, j: (i, 0)),
        out_shape=jax.ShapeDtypeStruct((m, d), F32),
        scratch_shapes=[pltpu.VMEM((tm, d), BF16), pltpu.VMEM((tm, d), F32)],
        compiler_params=_cparams(2, 48),
        name="ffn",
    )(x, g.reshape(1, d), w_gu, w_gu, w_down)


def _ple_body(x_ref, g_ref, p_ref, wg_ref, wp_ref, o_ref):
    x = x_ref[...]
    h = _rms(x, g_ref[...]).astype(BF16)
    gate = jax.nn.sigmoid(_dot(h, wg_ref[...]))
    o_ref[...] = x + gate * _dot(p_ref[...].astype(BF16), wp_ref[...])


def _ple(x, g, p, w_gate, w_proj):
    m, d = x.shape
    pd = p.shape[1]
    tm = min(m, 512)
    return pl.pallas_call(
        _ple_body,
        grid=(m // tm,),
        in_specs=[
            pl.BlockSpec((tm, d), lambda i: (i, 0)),
            pl.BlockSpec((1, d), lambda i: (0, 0)),
            pl.BlockSpec((tm, pd), lambda i: (i, 0)),
            pl.BlockSpec((d, d), lambda i: (0, 0)),
            pl.BlockSpec((pd, d), lambda i: (0, 0)),
        ],
        out_specs=pl.BlockSpec((tm, d), lambda i: (i, 0)),
        out_shape=jax.ShapeDtypeStruct((m, d), F32),
        compiler_params=_cparams(1, 32),
        name="ple",
    )(x, g.reshape(1, d), p, w_gate, w_proj)


def _proj_res_body(x_ref, o_ref, w_ref, y_ref):
    y_ref[...] = x_ref[...] + _dot(o_ref[...], w_ref[...])


def _proj_res(x, o, w):
    m, d = x.shape
    k = o.shape[1]
    tm = min(m, 512)
    return pl.pallas_call(
        _proj_res_body,
        grid=(m // tm,),
        in_specs=[
            pl.BlockSpec((tm, d), lambda i: (i, 0)),
            pl.BlockSpec((tm, k), lambda i: (i, 0)),
            pl.BlockSpec((k, d), lambda i: (0, 0)),
        ],
        out_specs=pl.BlockSpec((tm, d), lambda i: (i, 0)),
        out_shape=jax.ShapeDtypeStruct((m, d), F32),
        compiler_params=_cparams(1, 32),
        name="proj_res",
    )(x, o, w)


def _conv_body(x_ref, prev_ref, g_ref, win_ref, wc_ref, wout_ref, y_ref, st_ref, u_ref,
               *, shift, halo, tm, d):
    t = pl.program_id(1)
    lo = halo - 2 * shift

    @pl.when(t == 0)
    def _():
        u_ref[lo:halo, :] = prev_ref[0]

    x = x_ref[0]
    h = _rms(x, g_ref[...]).astype(BF16)
    proj = _dot(h, win_ref[...])
    b_gate = proj[:, :d]
    u = proj[:, d:2 * d] * proj[:, 2 * d:]
    u_ref[halo:halo + tm, :] = u
    wc = wc_ref[...]
    y = (wc[0:1, :] * u_ref[lo:lo + tm, :]
         + wc[1:2, :] * u_ref[lo + shift:lo + shift + tm, :]
         + wc[2:3, :] * u)
    y_ref[0] = x + _dot((b_gate * y).astype(BF16), wout_ref[...])
    tail = u_ref[lo + tm:halo + tm, :]
    st_ref[0] = tail
    u_ref[lo:halo, :] = tail


def _conv_mixer(x, prev, g, w_in, w_conv, w_out, shift):
    nb, t, d = x.shape
    tm = min(t, 512)
    halo = -(-2 * shift // 8) * 8
    return pl.pallas_call(
        functools.partial(_conv_body, shift=shift, halo=halo, tm=tm, d=d),
        grid=(nb, t // tm),
        in_specs=[
            pl.BlockSpec((1, tm, d), lambda b, i: (b, i, 0)),
            pl.BlockSpec((1, 2 * shift, d), lambda b, i: (b, 0, 0)),
            pl.BlockSpec((1, d), lambda b, i: (0, 0)),
            pl.BlockSpec((d, 3 * d), lambda b, i: (0, 0)),
            pl.BlockSpec((CONV_W, d), lambda b, i: (0, 0)),
            pl.BlockSpec((d, d), lambda b, i: (0, 0)),
        ],
        out_specs=[
            pl.BlockSpec((1, tm, d), lambda b, i: (b, i, 0)),
            pl.BlockSpec((1, 2 * shift, d), lambda b, i: (b, 0, 0)),
        ],
        out_shape=[
            jax.ShapeDtypeStruct((nb, t, d), F32),
            jax.ShapeDtypeStruct((nb, 2 * shift, d), F32),
        ],
        scratch_shapes=[pltpu.VMEM((halo + tm, d), F32)],
        compiler_params=_cparams(2, 48),
        name="conv_mixer",
    )(x, prev, g.reshape(1, d), w_in, w_conv, w_out)


def _head_norm(xc, gain, gg):
    sq = xc * xc
    hi = sq.astype(BF16)
    lo = (sq - hi.astype(F32)).astype(BF16)
    ss = _dot(jnp.concatenate([hi, lo], axis=1), gg)
    return xc * lax.rsqrt(ss * (1.0 / HEAD_DIM) + RMS_EPS) * gain


def _nsa_proj_body(x_ref, g_ref, w_ref, qkg_ref, cos_ref, sin_ref, gg_ref,
                   qn_ref, qr_ref, kc_ref, vc_ref, ks_ref, vs_ref, kw_ref, vw_ref, gt_ref, *bf16_refs,
                   tm, d, rows_last):
    h = _rms(x_ref[...], g_ref[...]).astype(BF16)
    proj = _dot(h, w_ref[...])
    cos = cos_ref[...]
    sin = sin_ref[...]
    gg = gg_ref[...]
    lane = lax.broadcasted_iota(jnp.int32, (tm, LANES), 1)
    first_half = (lane % HEAD_DIM) < (HEAD_DIM // 2)
    scale = HEAD_DIM ** -0.5 * LOG2E

    def rope(xc):
        swapped = jnp.where(first_half, pltpu.roll(xc, LANES - HEAD_DIM // 2, 1),
                            pltpu.roll(xc, HEAD_DIM // 2, 1))
        return xc * cos + swapped * sin

    def slab(c):
        return proj[:, c * LANES:(c + 1) * LANES]

    n_q = d // LANES
    for c in range(n_q):
        qc = _head_norm(slab(c), qkg_ref[0:1, :], gg)
        qn_ref[:, c * LANES:(c + 1) * LANES] = (qc * scale).astype(BF16)
        qr_ref[:, c * LANES:(c + 1) * LANES] = (rope(qc) * scale).astype(BF16)
    def put(ref, c, val):
        if rows_last:
            ref[c * LANES:(c + 1) * LANES, :] = val
        else:
            ref[:, c * LANES:(c + 1) * LANES] = val

    for c in range(2):
        lanes = slice(c * LANES, (c + 1) * LANES)
        k_sel = rope(_head_norm(slab(n_q + 4 + c), qkg_ref[2:3, :], gg))
        k_win = rope(_head_norm(slab(n_q + 8 + c), qkg_ref[3:4, :], gg))
        vals = (slab(n_q + c), slab(n_q + 2 + c), k_sel, slab(n_q + 6 + c), k_win, slab(n_q + 10 + c))
        if rows_last:
            vals = tuple(v.T for v in vals)
            ksb_ref, kwb_ref, vsb_ref, vwb_ref = bf16_refs
            ksb_ref[:, lanes] = k_sel.astype(BF16)
            kwb_ref[:, lanes] = k_win.astype(BF16)
            vsb_ref[lanes, :] = vals[3].astype(BF16)
            vwb_ref[lanes, :] = vals[5].astype(BF16)
        for ref, v in zip((kc_ref, vc_ref, ks_ref, vs_ref, kw_ref, vw_ref), vals):
            put(ref, c, v)
    gt_ref[...] = jax.nn.sigmoid(slab(n_q + 12))


def _nsa_proj(x, g, w_in_pad, qkg128, cos128, sin128, gg, n_batch=None):
    m, d = x.shape
    n = w_in_pad.shape[1]
    tm = min(m, 512)
    row = lambda w: pl.BlockSpec((tm, w), lambda i: (i, 0))
    full = lambda a: pl.BlockSpec(a.shape, lambda i: (0,) * a.ndim)
    rows_last = n_batch is not None
    if rows_last:
        t = m // n_batch
        tiles = t // tm
        kv_spec = pl.BlockSpec((None, KV_W, tm), lambda i: (i // tiles, 0, i % tiles))
        kv_shape = lambda dt: jax.ShapeDtypeStruct((n_batch, KV_W, t), dt)
        extra_specs = [row(KV_W), row(KV_W), kv_spec, kv_spec]
        extra_shapes = [jax.ShapeDtypeStruct((m, KV_W), BF16)] * 2 + [kv_shape(BF16)] * 2
    else:
        kv_spec, kv_shape = row(KV_W), lambda dt: jax.ShapeDtypeStruct((m, KV_W), dt)
        extra_specs, extra_shapes = [], []
    return pl.pallas_call(
        functools.partial(_nsa_proj_body, tm=tm, d=d, rows_last=rows_last),
        grid=(m // tm,),
        in_specs=[row(d), pl.BlockSpec((1, d), lambda i: (0, 0)), pl.BlockSpec((d, n), lambda i: (0, 0)),
                  full(qkg128), row(LANES), row(LANES), full(gg)],
        out_specs=[row(d), row(d)] + [kv_spec] * 6 + [row(LANES)] + extra_specs,
        out_shape=[jax.ShapeDtypeStruct((m, d), BF16)] * 2 + [kv_shape(F32)] * 6
        + [jax.ShapeDtypeStruct((m, LANES), F32)] + extra_shapes,
        compiler_params=_cparams(1, 48),
        name="nsa_proj",
    )(x, g.reshape(1, d), w_in_pad, qkg128, cos128, sin128, gg)


def _page_copy(src_ref, buf_ref, sem, slot, paged, index, p, dst_page):
    src = src_ref.at[index] if paged else src_ref.at[index, :, pl.ds(p * PAGE_SIZE, PAGE_SIZE)]
    dst = buf_ref.at[slot, :, dst_page, :] if buf_ref.ndim == 4 else buf_ref.at[slot, pl.ds(dst_page * KV_W, KV_W)]
    return pltpu.make_async_copy(src, dst, sem.at[slot])


def _gather_start(pt_ref, src_ref, buf_ref, sem, step, slot, n_pages, n_seq, paged):
    for s in range(n_seq):
        seq = step * n_seq + s
        for p in range(n_pages):
            index = pt_ref[seq, p] if paged else seq
            _page_copy(src_ref, buf_ref, sem, slot, paged, index, p, s * n_pages + p).start()


def _gather_wait(src_ref, buf_ref, sem, slot, n_pages, n_seq, paged):
    for s in range(n_seq):
        for p in range(n_pages):
            _page_copy(src_ref, buf_ref, sem, slot, paged, 0, p, s * n_pages + p).wait()


def _compress_body(pt_ref, src_ref, pos_ref, wd_ref, bd2_ref, gain_ref, gg_ref, o_ref,
                   buf_ref, sem, acc_ref, *, n_pages, n_seq, paged, normalize):
    step = pl.program_id(0)
    n_steps = pl.num_programs(0)
    slot = step % 2
    m = n_seq * n_pages
    unroll = 8

    @pl.when(step == 0)
    def _():
        _gather_start(pt_ref, src_ref, buf_ref, sem, 0, 0, n_pages, n_seq, paged)

    @pl.when(step + 1 < n_steps)
    def _():
        _gather_start(pt_ref, src_ref, buf_ref, sem, step + 1, 1 - slot, n_pages, n_seq, paged)

    _gather_wait(src_ref, buf_ref, sem, slot, n_pages, n_seq, paged)
    acc_ref[...] = jnp.zeros_like(acc_ref)

    def chunk(c, carry):
        accs = [acc_ref[g] for g in range(N_KV)]
        for dd in range(unroll):
            d = c * unroll + dd
            w = wd_ref[d]
            pos = pos_ref[pl.ds(d, 1), :]
            for g in range(N_KV):
                rows = buf_ref[slot, g * HEAD_DIM + d] + pos
                accs[g] = accs[g] + _dot(rows.astype(BF16), w)
        for g in range(N_KV):
            acc_ref[g] = accs[g]
        return carry

    lax.fori_loop(0, HEAD_DIM // unroll, chunk, 0)

    for g in range(N_KV):
        y = _dot(jax.nn.gelu(acc_ref[g]).astype(BF16), bd2_ref[...])
        if normalize:
            y = _head_norm(y, gain_ref[...], gg_ref[...])
        for s in range(n_seq):
            o_ref[s, g] = y[s * n_pages:(s + 1) * n_pages].astype(BF16)


def _compress(page_table, src, pos_t, wd, bd2, gain128, gg, normalize, paged, n_seq):
    nb, n_pages = page_table.shape
    hid2 = wd.shape[2]
    full = lambda a: pl.BlockSpec(a.shape, lambda i, pt: (0,) * a.ndim)
    grid_spec = pltpu.PrefetchScalarGridSpec(
        num_scalar_prefetch=1,
        grid=(nb // n_seq,),
        in_specs=[pl.BlockSpec(memory_space=pl.ANY), full(pos_t), full(wd), full(bd2), full(gain128), full(gg)],
        out_specs=pl.BlockSpec((n_seq, N_KV, n_pages, LANES), lambda i, pt: (i, 0, 0, 0)),
        scratch_shapes=[
            pltpu.VMEM((2, KV_W, n_seq * n_pages, PAGE_SIZE), F32),
            pltpu.SemaphoreType.DMA((2,)),
            pltpu.VMEM((N_KV, n_seq * n_pages, hid2), F32),
        ],
    )
    return pl.pallas_call(
        functools.partial(_compress_body, n_pages=n_pages, n_seq=n_seq, paged=paged, normalize=normalize),
        grid_spec=grid_spec,
        out_shape=jax.ShapeDtypeStruct((nb, N_KV, n_pages, LANES), BF16),
        compiler_params=_cparams(1, 48),
        name="compress",
    )(page_table, src, pos_t, wd, bd2, gain128, gg)


def _softmax_cols(s):
    m = jnp.max(s, axis=0, keepdims=True)
    m = jnp.where(m == NEG_INF, 0.0, m)
    e = jnp.exp2(s - m)
    return e, jnp.sum(e, axis=0, keepdims=True)


def _select_bias(imp, cur):
    jj = lax.broadcasted_iota(jnp.int32, imp.shape, 0)
    forced = (jj == 0) | (jj == cur) | (jj == cur - 1)
    cand = (jj <= cur) & jnp.logical_not(forced)
    v = jnp.where(cand, imp, -1.0)
    bias = jnp.where(forced, 0.0, NEG_INF)
    for _ in range(TOP_N - N_FORCED):
        m = jnp.max(v, axis=0, keepdims=True)
        idx = jnp.min(jnp.where(v == m, jj, imp.shape[0]), axis=0, keepdims=True)
        idx = jnp.where(m >= 0.0, idx, -1)
        pick = jj == idx
        bias = jnp.where(pick, 0.0, bias)
        v = jnp.where(pick, -1.0, v)
    return bias


def _attn_prompt_body(qn_ref, qr_ref, gt_ref, kc_ref, vct_ref, ks_ref, blk_ref, vst_ref, kw_ref, vwt_ref,
                      o_ref, qsel_ref, sa_ref, sb_ref):
    i = pl.program_id(2)
    q_start = i * QBLK
    n_col = HPG * QBLK
    q_pos = q_start + lax.broadcasted_iota(jnp.int32, (1, n_col), 1) % QBLK

    in_half = lax.broadcasted_iota(jnp.int32, (LANES, 1), 0) // HEAD_DIM == pl.program_id(1) % 2
    place = lambda q64: jnp.where(in_half, jnp.concatenate([q64, q64], axis=0), jnp.zeros((), BF16))
    qn = place(qn_ref[...])
    qr = place(qr_ref[...])

    n_cmp = kc_ref.shape[0]
    s = _dot(kc_ref[...], qn)
    cmp_end = lax.broadcasted_iota(jnp.int32, (n_cmp, 1), 0) * CMP_BLK + (CMP_BLK - 1)
    s = jnp.where(cmp_end <= q_pos, s, NEG_INF)
    e, l = _softmax_cols(s)
    p_cmp = e / jnp.maximum(l, 1e-30)
    o_cmp = _dot(vct_ref[...], p_cmp.astype(BF16))

    imp = p_cmp[:, 0:QBLK]
    for hh in range(1, HPG):
        imp = imp + p_cmp[:, hh * QBLK:(hh + 1) * QBLK]
    bias = jnp.maximum(_select_bias(imp, q_pos[:, 0:QBLK] // SEL_BLK), MASKED)
    own = lax.broadcasted_iota(jnp.int32, (n_cmp, QBLK), 0) >= q_start // SEL_BLK
    bias = jnp.where(own, MASKED, bias)
    qsel_ref[0:LANES, :] = qr
    qsel_ref[LANES:LANES + n_cmp, :] = jnp.concatenate([bias.astype(BF16)] * HPG, axis=1)
    if n_cmp < LANES:
        qsel_ref[LANES + n_cmp:, :] = jnp.zeros((LANES - n_cmp, n_col), BF16)

    n_win = WINDOW + QBLK
    w0 = pl.multiple_of(q_start, QBLK)
    s = _dot(kw_ref[pl.ds(w0, n_win), :], qr)
    r = lax.broadcasted_iota(jnp.int32, (n_win, QBLK), 0)
    q = lax.broadcasted_iota(jnp.int32, (n_win, QBLK), 1)
    valid = (r > q) & (r <= q + WINDOW) & (r + q_start >= WINDOW)
    wbias = jnp.where(valid, 0.0, NEG_INF)
    s = s + jnp.concatenate([wbias] * HPG, axis=1)
    e, l = _softmax_cols(s)
    o_win = _dot(vwt_ref[:, pl.ds(w0, n_win)], e.astype(BF16)) / jnp.maximum(l, 1e-30)

    s = _dot(ks_ref[pl.ds(w0, QBLK), :], qr)
    tri = jnp.where(lax.broadcasted_iota(jnp.int32, (QBLK, QBLK), 0)
                    <= lax.broadcasted_iota(jnp.int32, (QBLK, QBLK), 1), 0.0, NEG_INF)
    s = s + jnp.concatenate([tri] * HPG, axis=1)
    m0 = jnp.max(s, axis=0, keepdims=True)
    p = jnp.exp2(s - m0)
    init = (m0, jnp.sum(p, axis=0, keepdims=True), _dot(vst_ref[:, pl.ds(w0, QBLK)], p.astype(BF16)))

    n_tiles = (q_start + KTILE - 1) // KTILE
    max_tile = ks_ref.shape[0] // KTILE - 1

    def scores(t):
        k0 = pl.multiple_of(t * KTILE, KTILE)
        keys = jnp.concatenate([ks_ref[pl.ds(k0, KTILE), :], blk_ref[pl.ds(k0, KTILE), :]], axis=1)
        return _dot(keys, qsel_ref[...])

    def sel_tile(s_buf, t, carry):
        m, l, acc = carry
        k0 = pl.multiple_of(t * KTILE, KTILE)
        s = s_buf[...]
        m_new = jnp.maximum(m, jnp.max(s, axis=0, keepdims=True))
        alpha = jnp.exp2(m - m_new)
        p = jnp.exp2(s - m_new)
        l = alpha * l + jnp.sum(p, axis=0, keepdims=True)
        acc = alpha * acc + _dot(vst_ref[:, pl.ds(k0, KTILE)], p.astype(BF16))
        return m_new, l, acc

    def tile_pair(u, carry):
        t = 2 * u
        sb_ref[...] = scores(t + 1)
        carry = sel_tile(sa_ref, t, carry)
        sa_ref[...] = scores(jnp.minimum(t + 2, max_tile))
        return sel_tile(sb_ref, t + 1, carry)

    sa_ref[...] = scores(0)
    carry = lax.fori_loop(0, n_tiles // 2, tile_pair, init)
    _, l_sel, acc_sel = lax.fori_loop(0, n_tiles % 2, lambda _, c: sel_tile(sa_ref, n_tiles - 1, c), carry)
    o_sel = acc_sel / l_sel

    gt = gt_ref[...]
    o_ref[...] = (gt[0:1, :] * o_cmp + gt[1:2, :] * o_sel + gt[2:3, :] * o_win).astype(BF16)


def _attn_prompt(qn_t, qr_t, gt_t, kc, vc_t, ks, blk_onehot, vs_t, kw_pad, vw_pad_t):
    nb, ng, nqb, hd, n_col = qn_t.shape
    n_cmp = kc.shape[1]
    assert n_cmp <= LANES
    per_q = lambda r: pl.BlockSpec((None, None, None, r, n_col), lambda b, g, i: (b, g, i, 0, 0))
    pair = lambda a: pl.BlockSpec((None, a.shape[1], LANES), lambda b, g, i: (b, 0, g // 2))
    group_rows = lambda a: pl.BlockSpec((None, hd, a.shape[2]), lambda b, g, i: (b, g, 0))
    return pl.pallas_call(
        _attn_prompt_body,
        grid=(nb, ng, nqb),
        in_specs=[per_q(hd), per_q(hd), per_q(8), pair(kc), group_rows(vc_t), pair(ks),
                  pl.BlockSpec(blk_onehot.shape, lambda b, g, i: (0, 0)), group_rows(vs_t),
                  pair(kw_pad), group_rows(vw_pad_t)],
        out_specs=per_q(hd),
        out_shape=jax.ShapeDtypeStruct((nb, ng, nqb, hd, n_col), BF16),
        scratch_shapes=[pltpu.VMEM((2 * LANES, n_col), BF16), pltpu.VMEM((KTILE, n_col), F32),
                        pltpu.VMEM((KTILE, n_col), F32)],
        compiler_params=_cparams(3, 48),
        name="attn_prompt",
    )(qn_t, qr_t, gt_t, kc, vc_t, ks, blk_onehot, vs_t, kw_pad, vw_pad_t)


def _attn_sample_body(pt_ref, qn_ref, qr_ref, gt_ref, kc_ref, vc_ref, kwin_ref, vwin_ref,
                      ksn_ref, vsn_ref, kwn_ref, vwn_ref, spread_ref, kpool_ref, vpool_ref, o_ref,
                      kbuf_ref, vbuf_ref, ksem, vsem, s_ref, *, n_pages, n_new, past_len):
    b = pl.program_id(0)
    nb = pl.num_programs(0)
    slot = b % 2

    @pl.when(b == 0)
    def _():
        _gather_start(pt_ref, kpool_ref, kbuf_ref, ksem, 0, 0, n_pages, 1, True)
        _gather_start(pt_ref, vpool_ref, vbuf_ref, vsem, 0, 0, n_pages, 1, True)

    @pl.when(b + 1 < nb)
    def _():
        _gather_start(pt_ref, kpool_ref, kbuf_ref, ksem, b + 1, 1 - slot, n_pages, 1, True)
        _gather_start(pt_ref, vpool_ref, vbuf_ref, vsem, b + 1, 1 - slot, n_pages, 1, True)

    qn = qn_ref[0]
    qr = qr_ref[0]
    n_col = qr.shape[0]
    n_past = n_pages * PAGE_SIZE
    n_blk = n_past // SEL_BLK
    n_gq = N_KV * n_new
    dot_nt = lambda a, bt: lax.dot_general(a, bt, (((1,), (1,)), ((), ())), preferred_element_type=F32)

    lane = lax.broadcasted_iota(jnp.int32, (1, LANES), 1)
    lane_pos = past_len + lane % n_new
    s = _dot(kc_ref[0], qn)
    cmp_end = lax.broadcasted_iota(jnp.int32, (n_blk, 1), 0) * CMP_BLK + (CMP_BLK - 1)
    s = jnp.where(cmp_end <= lane_pos, s, NEG_INF)
    e, l = _softmax_cols(s)
    p_cmp = e / jnp.maximum(l, 1e-30)
    o_cmp = _dot(p_cmp.T.astype(BF16), vc_ref[0])[:n_col]

    imp = p_cmp
    for hh in range(1, HPG):
        imp = imp + pltpu.roll(p_cmp, LANES - hh * n_gq, 1)
    sel = _select_bias(imp, lane_pos // SEL_BLK)
    sel01 = jnp.where((lane < n_gq) & (sel == 0.0), 1.0, 0.0)
    spread = sel01
    for hh in range(1, HPG):
        spread = spread + pltpu.roll(sel01, hh * n_gq, 1)
    sel_bias = jnp.where(spread.T[:n_col] > 0.5, 0.0, MASKED).astype(BF16)

    row = lax.broadcasted_iota(jnp.int32, (n_col, 1), 0)
    q_idx = row % n_new
    q_pos = past_len + q_idx
    n_pad = ksn_ref.shape[1]
    new_idx = lax.broadcasted_iota(jnp.int32, (1, n_pad), 1)
    new_ok = (new_idx <= q_idx) & (new_idx < n_new)

    _gather_wait(kpool_ref, kbuf_ref, ksem, slot, n_pages, 1, True)
    _gather_wait(vpool_ref, vbuf_ref, vsem, slot, n_pages, 1, True)
    pages_per_chunk = 8
    chunk = pages_per_chunk * PAGE_SIZE

    def page_rows(c, pp):
        return pl.ds(pl.multiple_of((c * pages_per_chunk + pp) * KV_W, KV_W), KV_W)

    def score_chunk(c, m):
        k0 = pl.multiple_of(c * chunk, chunk)
        s = jnp.concatenate([_dot(qr, kbuf_ref[slot, page_rows(c, pp), :].astype(BF16))
                             for pp in range(pages_per_chunk)], axis=1)
        s = s + _dot(sel_bias, spread_ref[:, pl.ds(k0, chunk)])
        s_ref[:, pl.ds(k0, chunk)] = s
        return jnp.maximum(m, jnp.max(s, axis=1, keepdims=True))

    m = lax.fori_loop(0, n_pages // pages_per_chunk, score_chunk, jnp.full((n_col, 1), NEG_INF, F32))
    s_new = jnp.where(new_ok, dot_nt(qr, ksn_ref[0].astype(BF16)), NEG_INF)
    m = jnp.maximum(m, jnp.max(s_new, axis=1, keepdims=True))
    m = jnp.where(m == NEG_INF, 0.0, m)

    def value_chunk(c, carry):
        l, acc = carry
        k0 = pl.multiple_of(c * chunk, chunk)
        p = jnp.exp2(s_ref[:, pl.ds(k0, chunk)] - m)
        l = l + jnp.sum(p, axis=1, keepdims=True)
        p = p.astype(BF16)
        for pp in range(pages_per_chunk):
            acc = acc + dot_nt(p[:, pp * PAGE_SIZE:(pp + 1) * PAGE_SIZE],
                               vbuf_ref[slot, page_rows(c, pp), :].astype(BF16))
        return l, acc

    p_new = jnp.exp2(s_new - m)
    l_sel, acc = lax.fori_loop(
        0, n_pages // pages_per_chunk, value_chunk,
        (jnp.sum(p_new, axis=1, keepdims=True), _dot(p_new.astype(BF16), vsn_ref[0].astype(BF16))))
    o_sel = acc / jnp.maximum(l_sel, 1e-30)

    n_buf = kwin_ref.shape[2]
    s_buf = _dot(qr, kwin_ref[0].astype(BF16))
    buf_pos = past_len - n_buf + lax.broadcasted_iota(jnp.int32, (1, n_buf), 1)
    diff = q_pos - buf_pos
    s_buf = jnp.where((diff >= 0) & (diff < WINDOW) & (buf_pos >= 0), s_buf, NEG_INF)
    s_nw = jnp.where(new_ok, dot_nt(qr, kwn_ref[0].astype(BF16)), NEG_INF)
    m = jnp.maximum(jnp.max(s_buf, axis=1, keepdims=True), jnp.max(s_nw, axis=1, keepdims=True))
    m = jnp.where(m == NEG_INF, 0.0, m)
    p_buf = jnp.exp2(s_buf - m)
    p_nw = jnp.exp2(s_nw - m)
    l_win = jnp.sum(p_buf, axis=1, keepdims=True) + jnp.sum(p_nw, axis=1, keepdims=True)
    o_win = (dot_nt(p_buf.astype(BF16), vwin_ref[0].astype(BF16))
             + _dot(p_nw.astype(BF16), vwn_ref[0].astype(BF16))) / jnp.maximum(l_win, 1e-30)

    gt = gt_ref[0]
    o = gt[:, 0:1] * o_cmp + gt[:, 1:2] * o_sel + gt[:, 2:3] * o_win
    row_g = (lax.broadcasted_iota(jnp.int32, (n_col, KV_W), 0) // n_new) % N_KV
    lane_g = lax.broadcasted_iota(jnp.int32, (n_col, KV_W), 1) // HEAD_DIM
    o = jnp.where(row_g == lane_g, o, 0.0)
    z = o[:, :LANES] + o[:, LANES:]
    o_ref[0] = z + pltpu.roll(z, HEAD_DIM, 1)


def _attn_sample(page_table, qn_bd, qr_bd_t, gt_col, kc, vc, kwin_t, vwin_t, ks_new, vs_new, kw_new, vw_new,
                 blk_spread, k_pool, v_pool, n_new, past_len):
    nb, n_pages = page_table.shape
    n_past = n_pages * PAGE_SIZE
    n_col = qr_bd_t.shape[1]
    per_b = lambda a: pl.BlockSpec((1,) + a.shape[1:], lambda b, pt: (b,) + (0,) * (a.ndim - 1))
    any_spec = pl.BlockSpec(memory_space=pl.ANY)
    ins = (qn_bd, qr_bd_t, gt_col, kc, vc, kwin_t, vwin_t, ks_new, vs_new, kw_new, vw_new)
    grid_spec = pltpu.PrefetchScalarGridSpec(
        num_scalar_prefetch=1,
        grid=(nb,),
        in_specs=[per_b(a) for a in ins]
        + [pl.BlockSpec(blk_spread.shape, lambda b, pt: (0, 0)), any_spec, any_spec],
        out_specs=pl.BlockSpec((1, n_col, LANES), lambda b, pt: (b, 0, 0)),
        scratch_shapes=[
            pltpu.VMEM((2, n_pages * KV_W, PAGE_SIZE), F32),
            pltpu.VMEM((2, n_pages * KV_W, PAGE_SIZE), F32),
            pltpu.SemaphoreType.DMA((2,)),
            pltpu.SemaphoreType.DMA((2,)),
            pltpu.VMEM((n_col, n_past), F32),
        ],
    )
    return pl.pallas_call(
        functools.partial(_attn_sample_body, n_pages=n_pages, n_new=n_new, past_len=past_len),
        grid_spec=grid_spec,
        out_shape=jax.ShapeDtypeStruct((nb, n_col, LANES), F32),
        compiler_params=_cparams(1, 56),
        name="attn_sample",
    )(page_table, *ins, blk_spread, k_pool, v_pool)


def _rope_tables(pos):
    half = HEAD_DIM // 2
    inv = ROPE_THETA ** (-jnp.arange(half, dtype=F32) / half)
    ang = pos.astype(F32)[:, None] * inv
    cos, sin = jnp.cos(ang), jnp.sin(ang)
    reps = LANES // HEAD_DIM
    return (jnp.tile(jnp.concatenate([cos, cos], axis=1), (1, reps)),
            jnp.tile(jnp.concatenate([-sin, sin], axis=1), (1, reps)))


def _group_sum_matrix():
    lane = jnp.arange(LANES)
    g = (lane[:, None] // HEAD_DIM == lane[None, :] // HEAD_DIM).astype(BF16)
    return jnp.concatenate([g, g], axis=0)


def _compress_weights(cmp_pos, cmp_w1, cmp_w2):
    n_pair = PAGE_SIZE // CMP_BLK
    eye = jnp.eye(n_pair, dtype=F32)
    hid = cmp_w1.shape[1]
    w1 = cmp_w1.reshape(CMP_BLK, HEAD_DIM, hid)
    wd = jnp.einsum("bk,rdh->dbrkh", eye, w1).reshape(HEAD_DIM, PAGE_SIZE, n_pair * hid).astype(BF16)
    bd2 = jnp.einsum("bk,hd->bhkd", eye, cmp_w2).reshape(n_pair * hid, n_pair * HEAD_DIM).astype(BF16)
    return jnp.tile(cmp_pos.T, (1, n_pair)), wd, bd2


def _to_rows_major(a_t):
    nb, _, rows = a_t.shape
    return a_t.reshape(nb, N_KV, HEAD_DIM, rows).transpose(0, 3, 1, 2)[None]


def _from_rows_major(a):
    n, rows = a.shape[:2]
    return a.transpose(0, 2, 3, 1).reshape(n, KV_W, rows)


def _layer_tail(x, p, norm_g, w_gu, w_down, w_ple_proj, w_ple_gate):
    x = _ffn(x, norm_g[2], w_gu[1], w_down[1])
    return _ple(x, norm_g[3], p, w_ple_gate, w_ple_proj)


def kernel(x_prompt, x_sample, state_conv, cache_k_cmp, cache_v_cmp, cache_k_sel, cache_v_sel,
           cache_k_win, cache_v_win, page_table, p_prompt, p_sample, norm_g, ffn_w_gu, ffn_w_down,
           ple_w_proj, ple_w_gate, conv_w_in, conv_w, conv_w_out, nsa_w_in, nsa_qk_g, nsa_cmp_pos,
           nsa_cmp_w1, nsa_cmp_w2, nsa_w_out):
    nb_p, t_p, d = x_prompt.shape
    nb_s, t_s, _ = x_sample.shape
    ple_dim = p_prompt.shape[-1]
    past_len = page_table.shape[1] * PAGE_SIZE
    assert t_p % KTILE == 0 and t_p % PAGE_SIZE == 0 and d % LANES == 0

    w_gu = ffn_w_gu.astype(BF16)
    w_down = ffn_w_down.astype(BF16)
    w_pp = ple_w_proj.astype(BF16)
    w_pg = ple_w_gate.astype(BF16)

    xp = x_prompt.reshape(nb_p * t_p, d)
    xs = x_sample.reshape(nb_s * t_s, d)
    pp = p_prompt.reshape(-1, nb_p * t_p, ple_dim)
    ps = p_sample.reshape(-1, nb_s * t_s, ple_dim)

    lw = (norm_g[0], w_gu[0], w_down[0], w_pp[0], w_pg[0])
    cw = (conv_w_in[0].astype(BF16), conv_w[0], conv_w_out[0].astype(BF16))
    xp = _ffn(xp, norm_g[0, 0], w_gu[0, 0], w_down[0, 0])
    xs = _ffn(xs, norm_g[0, 0], w_gu[0, 0], w_down[0, 0])
    xp3, conv_p = _conv_mixer(xp.reshape(nb_p, t_p, d), jnp.zeros((nb_p, CONV_W - 1, d), F32),
                              norm_g[0, 1], *cw, shift=1)
    xs_tb = xs.reshape(nb_s, t_s, d).transpose(1, 0, 2).reshape(1, t_s * nb_s, d)
    prev_tb = state_conv[0].transpose(1, 0, 2).reshape(1, (CONV_W - 1) * nb_s, d)
    xs3, conv_s = _conv_mixer(xs_tb, prev_tb, norm_g[0, 1], *cw, shift=nb_s)
    xs = xs3.reshape(t_s, nb_s, d).transpose(1, 0, 2).reshape(nb_s * t_s, d)
    conv_s = conv_s.reshape(CONV_W - 1, nb_s, d).transpose(1, 0, 2)
    xp = _layer_tail(xp3.reshape(nb_p * t_p, d), pp[0], *lw)
    xs = _layer_tail(xs, ps[0], *lw)

    lw = (norm_g[1], w_gu[1], w_down[1], w_pp[1], w_pg[1])
    xp = _ffn(xp, norm_g[1, 0], w_gu[1, 0], w_down[1, 0])
    xs = _ffn(xs, norm_g[1, 0], w_gu[1, 0], w_down[1, 0])

    n_in = nsa_w_in.shape[-1]
    n_in_pad = -(-n_in // LANES) * LANES
    w_in = jnp.pad(nsa_w_in[0], ((0, 0), (0, n_in_pad - n_in))).astype(BF16)
    w_out = nsa_w_out[0].astype(BF16)
    qkg128 = jnp.tile(nsa_qk_g[0], (1, LANES // HEAD_DIM))
    gg = _group_sum_matrix()
    cos_p, sin_p = _rope_tables(jnp.arange(t_p))
    cos_s, sin_s = _rope_tables(past_len + jnp.arange(t_s))
    pos_k, wd_k, bd2_k = _compress_weights(nsa_cmp_pos[0, 0], nsa_cmp_w1[0, 0], nsa_cmp_w2[0, 0])
    pos_v, wd_v, bd2_v = _compress_weights(nsa_cmp_pos[0, 1], nsa_cmp_w1[0, 1], nsa_cmp_w2[0, 1])
    k_gain = qkg128[1:2]

    qn, qr, kc_t, vc_t, ks_t, vs_t, kw_t, vw_t, gt, ks_b, kw_b, vs_tb, vw_tb = _nsa_proj(
        xp, norm_g[1, 1], w_in, qkg128, jnp.tile(cos_p, (nb_p, 1)), jnp.tile(sin_p, (nb_p, 1)), gg, n_batch=nb_p)
    pt_p = jnp.zeros((nb_p, t_p // PAGE_SIZE), jnp.int32)
    as_blocks = lambda c: c.reshape(c.shape[0], N_KV, -1, HEAD_DIM)
    by_block = lambda c: as_blocks(c).transpose(0, 2, 1, 3).reshape(c.shape[0], -1, KV_W)
    kc = _compress(pt_p, kc_t, pos_k, wd_k, bd2_k, k_gain, gg, True, False, nb_p)
    vc = _compress(pt_p, vc_t, pos_v, wd_v, bd2_v, k_gain, gg, False, False, nb_p)

    nqb = t_p // QBLK
    n_col = HPG * QBLK

    def q_cols(a):
        a = a.reshape(nb_p, nqb, QBLK, N_KV, HPG, HEAD_DIM).transpose(0, 3, 1, 5, 4, 2)
        return a.reshape(nb_p, N_KV, nqb, HEAD_DIM, n_col)

    gt_t = gt[:, :N_KV * HPG * 3].reshape(nb_p, nqb, QBLK, N_KV, HPG, 3).transpose(0, 3, 1, 5, 4, 2)
    gt_t = jnp.pad(gt_t.reshape(nb_p, N_KV, nqb, 3, n_col), ((0, 0),) * 3 + ((0, 5), (0, 0)))
    blk_onehot = (jnp.arange(t_p)[:, None] // SEL_BLK == jnp.arange(LANES)[None, :]).astype(BF16)
    o_t = _attn_prompt(
        q_cols(qn), q_cols(qr), gt_t,
        by_block(kc), as_blocks(vc).transpose(0, 1, 3, 2).reshape(nb_p, KV_W, -1),
        ks_b.reshape(nb_p, t_p, KV_W), blk_onehot, vs_tb,
        jnp.pad(kw_b.reshape(nb_p, t_p, KV_W), ((0, 0), (WINDOW, 0), (0, 0))),
        jnp.pad(vw_tb, ((0, 0), (0, 0), (WINDOW, 0))))
    o = o_t.reshape(nb_p, N_KV, nqb, HEAD_DIM, HPG, QBLK).transpose(0, 2, 5, 1, 4, 3)
    xp = _proj_res(xp, o.reshape(nb_p * t_p, d), w_out)
    keep = min(WINDOW, t_p)
    nsa_p = tuple(_to_rows_major(a) for a in (kc_t, vc_t, ks_t, vs_t, kw_t[:, :, -keep:], vw_t[:, :, -keep:]))

    qn, qr, kc_r, vc_r, ks, vs, kw, vw, gt = _nsa_proj(
        xs, norm_g[1, 1], w_in, qkg128, jnp.tile(cos_s, (nb_s, 1)), jnp.tile(sin_s, (nb_s, 1)), gg)
    pool_t = lambda c: _from_rows_major(c[0])
    n_seq = 2 if nb_s % 2 == 0 else 1
    kc = by_block(_compress(page_table, pool_t(cache_k_cmp), pos_k, wd_k, bd2_k, k_gain, gg, True, True, n_seq))
    vc = by_block(_compress(page_table, pool_t(cache_v_cmp), pos_v, wd_v, bd2_v, k_gain, gg, False, True, n_seq))

    n_used = HPG * N_KV * t_s

    def q_block_diag(a):
        a = a.reshape(nb_s, t_s, N_KV, HPG, HEAD_DIM).astype(F32)
        return jnp.einsum("bqghd,gk->bgdhkq", a, jnp.eye(N_KV, dtype=F32)).reshape(nb_s, KV_W, n_used)

    qn_bd = jnp.pad(q_block_diag(qn), ((0, 0), (0, 0), (0, LANES - n_used))).astype(BF16)
    qr_bd_t = q_block_diag(qr).transpose(0, 2, 1).astype(BF16)
    gt_col = gt[:, :N_KV * HPG * 3].reshape(nb_s, t_s, N_KV, HPG, 3).transpose(0, 3, 2, 1, 4)
    gt_col = jnp.pad(gt_col.reshape(nb_s, n_used, 3), ((0, 0), (0, 0), (0, LANES - 3)))
    new_rows = lambda a: jnp.pad(a.reshape(nb_s, t_s, KV_W), ((0, 0), (0, 16 - t_s), (0, 0)))
    n_buf = cache_k_win.shape[2]
    kwin_t, vwin_t = _from_rows_major(cache_k_win[0]), _from_rows_major(cache_v_win[0])
    blk_spread = (jnp.arange(past_len // SEL_BLK)[:, None] == jnp.arange(past_len)[None, :] // SEL_BLK).astype(BF16)
    o_s = _attn_sample(
        page_table, qn_bd, qr_bd_t, gt_col, kc, vc, kwin_t, vwin_t,
        new_rows(ks), new_rows(vs), new_rows(kw), new_rows(vw), blk_spread,
        pool_t(cache_k_sel), pool_t(cache_v_sel), t_s, past_len)
    o = o_s[:, :, :HEAD_DIM].reshape(nb_s, HPG, N_KV, t_s, HEAD_DIM).transpose(0, 3, 2, 1, 4)
    xs = _proj_res(xs, o.reshape(nb_s * t_s, d).astype(BF16), w_out)
    st = lambda a: a.reshape(1, nb_s, t_s, N_KV, HEAD_DIM)
    new_t = lambda a: a.reshape(nb_s, t_s, KV_W).transpose(0, 2, 1)
    win = lambda buf_t, new: _to_rows_major(jnp.concatenate([buf_t, new_t(new)], axis=2)[:, :, -n_buf:])
    nsa_s = (st(kc_r), st(vc_r), st(ks), st(vs), win(kwin_t, kw), win(vwin_t, vw))

    xp = _layer_tail(xp, pp[1], *lw)
    xs = _layer_tail(xs, ps[1], *lw)

    return (xp.reshape(nb_p, t_p, d), xs.reshape(nb_s, t_s, d), conv_p[None], conv_s[None]) + nsa_p + nsa_s
```

```python
import functools

import jax
import jax.numpy as jnp
from jax import lax
from jax.experimental import pallas as pl
from jax.experimental.pallas import tpu as pltpu

F32 = jnp.float32
BF16 = jnp.bfloat16

HEAD_DIM = 64
N_KV = 4
HPG = 4
CMP_BLK = 64
SEL_BLK = 64
TOP_N = 16
N_FORCED = 3
WINDOW = 512
PAGE_SIZE = 128
CONV_W = 3
QBLK = 256
KTILE = 512
ROPE_THETA = 10000.0
RMS_EPS = 1e-6
NEG_INF = float("-inf")
MASKED = -1e30
LOG2E = 1.4426950408889634
BOUND_SLACK = 1.02
MIN_DENOM = 2.0 ** -100
ONES_ROWS = 16
KV_W = N_KV * HEAD_DIM
LANES = 128
MIB = 1024 * 1024


def _cparams(n_grid, vmem_mib):
    return pltpu.CompilerParams(
        dimension_semantics=("arbitrary",) * n_grid,
        vmem_limit_bytes=vmem_mib * MIB,
    )


def _rms(x, g):
    return x * lax.rsqrt(jnp.mean(x * x, axis=-1, keepdims=True) + RMS_EPS) * g


def _dot(a, b):
    return jnp.dot(a, b, preferred_element_type=F32)


def _ffn_body(x_ref, g_ref, wg_ref, wu_ref, wd_ref, o_ref, h_ref, acc_ref, *, n_f):
    j = pl.program_id(1)

    @pl.when(j == 0)
    def _():
        h_ref[...] = _rms(x_ref[...], g_ref[...]).astype(BF16)
        acc_ref[...] = jnp.zeros_like(acc_ref)

    h = h_ref[...]
    gate = _dot(h, wg_ref[...])
    up = _dot(h, wu_ref[...])
    act = (jax.nn.silu(gate) * up).astype(BF16)
    acc_ref[...] += _dot(act, wd_ref[...])

    @pl.when(j == n_f - 1)
    def _():
        o_ref[...] = x_ref[...] + 0.5 * acc_ref[...]


def _ffn(x, g, w_gu, w_down):
    m, d = x.shape
    f = w_down.shape[0]
    tm = min(m, 512)
    tf = f // 2
    n_f = f // tf
    return pl.pallas_call(
        functools.partial(_ffn_body, n_f=n_f),
        grid=(m // tm, n_f),
        in_specs=[
            pl.BlockSpec((tm, d), lambda i, j: (i, 0)),
            pl.BlockSpec((1, d), lambda i, j: (0, 0)),
            pl.BlockSpec((d, tf), lambda i, j: (0, j)),
            pl.BlockSpec((d, tf), lambda i, j: (0, n_f + j)),
            pl.BlockSpec((tf, d), lambda i, j: (j, 0)),
        ],
        out_specs=pl.BlockSpec((tm, d), lambda i, j: (i, 0)),
        out_shape=jax.ShapeDtypeStruct((m, d), F32),
        scratch_shapes=[pltpu.VMEM((tm, d), BF16), pltpu.VMEM((tm, d), F32)],
        compiler_params=_cparams(2, 48),
        name="ffn",
    )(x, g.reshape(1, d), w_gu, w_gu, w_down)


def _ple_body(x_ref, g_ref, p_ref, wg_ref, wp_ref, o_ref):
    x = x_ref[...]
    h = _rms(x, g_ref[...]).astype(BF16)
    gate = jax.nn.sigmoid(_dot(h, wg_ref[...]))
    o_ref[...] = x + gate * _dot(p_ref[...].astype(BF16), wp_ref[...])


def _ple(x, g, p, w_gate, w_proj):
    m, d = x.shape
    pd = p.shape[1]
    tm = min(m, 512)
    return pl.pallas_call(
        _ple_body,
        grid=(m // tm,),
        in_specs=[
            pl.BlockSpec((tm, d), lambda i: (i, 0)),
            pl.BlockSpec((1, d), lambda i: (0, 0)),
            pl.BlockSpec((tm, pd), lambda i: (i, 0)),
            pl.BlockSpec((d, d), lambda i: (0, 0)),
            pl.BlockSpec((pd, d), lambda i: (0, 0)),
        ],
        out_specs=pl.BlockSpec((tm, d), lambda i: (i, 0)),
        out_shape=jax.ShapeDtypeStruct((m, d), F32),
        compiler_params=_cparams(1, 32),
        name="ple",
    )(x, g.reshape(1, d), p, w_gate, w_proj)


def _proj_res_body(x_ref, o_ref, w_ref, y_ref):
    y_ref[...] = x_ref[...] + _dot(o_ref[...], w_ref[...])


def _proj_res(x, o, w):
    m, d = x.shape
    k = o.shape[1]
    tm = min(m, 512)
    return pl.pallas_call(
        _proj_res_body,
        grid=(m // tm,),
        in_specs=[
            pl.BlockSpec((tm, d), lambda i: (i, 0)),
            pl.BlockSpec((tm, k), lambda i: (i, 0)),
            pl.BlockSpec((k, d), lambda i: (0, 0)),
        ],
        out_specs=pl.BlockSpec((tm, d), lambda i: (i, 0)),
        out_shape=jax.ShapeDtypeStruct((m, d), F32),
        compiler_params=_cparams(1, 32),
        name="proj_res",
    )(x, o, w)


def _conv_body(x_ref, prev_ref, g_ref, win_ref, wc_ref, wout_ref, y_ref, st_ref, u_ref,
               *, shift, halo, tm, d):
    t = pl.program_id(1)
    lo = halo - 2 * shift

    @pl.when(t == 0)
    def _():
        u_ref[lo:halo, :] = prev_ref[0]

    x = x_ref[0]
    h = _rms(x, g_ref[...]).astype(BF16)
    proj = _dot(h, win_ref[...])
    b_gate = proj[:, :d]
    u = proj[:, d:2 * d] * proj[:, 2 * d:]
    u_ref[halo:halo + tm, :] = u
    wc = wc_ref[...]
    y = (wc[0:1, :] * u_ref[lo:lo + tm, :]
         + wc[1:2, :] * u_ref[lo + shift:lo + shift + tm, :]
         + wc[2:3, :] * u)
    y_ref[0] = x + _dot((b_gate * y).astype(BF16), wout_ref[...])
    tail = u_ref[lo + tm:halo + tm, :]
    st_ref[0] = tail
    u_ref[lo:halo, :] = tail


def _conv_mixer(x, prev, g, w_in, w_conv, w_out, shift):
    nb, t, d = x.shape
    tm = min(t, 512)
    halo = -(-2 * shift // 8) * 8
    return pl.pallas_call(
        functools.partial(_conv_body, shift=shift, halo=halo, tm=tm, d=d),
        grid=(nb, t // tm),
        in_specs=[
            pl.BlockSpec((1, tm, d), lambda b, i: (b, i, 0)),
            pl.BlockSpec((1, 2 * shift, d), lambda b, i: (b, 0, 0)),
            pl.BlockSpec((1, d), lambda b, i: (0, 0)),
            pl.BlockSpec((d, 3 * d), lambda b, i: (0, 0)),
            pl.BlockSpec((CONV_W, d), lambda b, i: (0, 0)),
            pl.BlockSpec((d, d), lambda b, i: (0, 0)),
        ],
        out_specs=[
            pl.BlockSpec((1, tm, d), lambda b, i: (b, i, 0)),
            pl.BlockSpec((1, 2 * shift, d), lambda b, i: (b, 0, 0)),
        ],
        out_shape=[
            jax.ShapeDtypeStruct((nb, t, d), F32),
            jax.ShapeDtypeStruct((nb, 2 * shift, d), F32),
        ],
        scratch_shapes=[pltpu.VMEM((halo + tm, d), F32)],
        compiler_params=_cparams(2, 48),
        name="conv_mixer",
    )(x, prev, g.reshape(1, d), w_in, w_conv, w_out)


def _head_norm(xc, gain, gg):
    sq = xc * xc
    hi = sq.astype(BF16)
    lo = (sq - hi.astype(F32)).astype(BF16)
    ss = _dot(jnp.concatenate([hi, lo], axis=1), gg)
    return xc * lax.rsqrt(ss * (1.0 / HEAD_DIM) + RMS_EPS) * gain


def _nsa_proj_body(x_ref, g_ref, w_ref, qkg_ref, cos_ref, sin_ref, gg_ref,
                   qn_ref, qr_ref, kc_ref, vc_ref, ks_ref, vs_ref, kw_ref, vw_ref, gt_ref, *extra_refs,
                   tm, d, rows_last):
    h = _rms(x_ref[...], g_ref[...]).astype(BF16)
    proj = _dot(h, w_ref[...])
    cos = cos_ref[...]
    sin = sin_ref[...]
    gg = gg_ref[...]
    lane = lax.broadcasted_iota(jnp.int32, (tm, LANES), 1)
    first_half = (lane % HEAD_DIM) < (HEAD_DIM // 2)
    scale = HEAD_DIM ** -0.5 * LOG2E

    def rope(xc):
        swapped = jnp.where(first_half, pltpu.roll(xc, LANES - HEAD_DIM // 2, 1),
                            pltpu.roll(xc, HEAD_DIM // 2, 1))
        return xc * cos + swapped * sin

    def slab(c):
        return proj[:, c * LANES:(c + 1) * LANES]

    n_q = d // LANES
    for c in range(n_q):
        qc = _head_norm(slab(c), qkg_ref[0:1, :], gg)
        qn_ref[:, c * LANES:(c + 1) * LANES] = (qc * scale).astype(BF16)
        qr_ref[:, c * LANES:(c + 1) * LANES] = (rope(qc) * scale).astype(BF16)
    def put(ref, c, val):
        if rows_last:
            ref[c * LANES:(c + 1) * LANES, :] = val
        else:
            ref[:, c * LANES:(c + 1) * LANES] = val

    for c in range(2):
        lanes = slice(c * LANES, (c + 1) * LANES)
        k_sel = rope(_head_norm(slab(n_q + 4 + c), qkg_ref[2:3, :], gg))
        k_win = rope(_head_norm(slab(n_q + 8 + c), qkg_ref[3:4, :], gg))
        vals = (slab(n_q + c), slab(n_q + 2 + c), k_sel, slab(n_q + 6 + c), k_win, slab(n_q + 10 + c))
        if rows_last:
            vals = tuple(v.T for v in vals)
            ksb_ref, kwb_ref, vsb_ref, vwb_ref, kn_ref = extra_refs
            k16 = k_sel.astype(BF16)
            ksb_ref[:, lanes] = k16
            k32 = k16.astype(F32)
            sq = k32 * k32
            hi = sq.astype(BF16)
            norm2 = _dot(jnp.concatenate([hi, (sq - hi.astype(F32)).astype(BF16)], axis=1), gg)
            kn_ref[:, lanes] = jnp.broadcast_to(jnp.max(norm2, axis=0, keepdims=True), (8, LANES))
            kwb_ref[:, lanes] = k_win.astype(BF16)
            vsb_ref[lanes, :] = vals[3].astype(BF16)
            vwb_ref[lanes, :] = vals[5].astype(BF16)
        for ref, v in zip((kc_ref, vc_ref, ks_ref, vs_ref, kw_ref, vw_ref), vals):
            put(ref, c, v)
    gt_ref[...] = jax.nn.sigmoid(slab(n_q + 12))


def _nsa_proj(x, g, w_in_pad, qkg128, cos128, sin128, gg, n_batch=None):
    m, d = x.shape
    n = w_in_pad.shape[1]
    tm = min(m, 512)
    row = lambda w: pl.BlockSpec((tm, w), lambda i: (i, 0))
    full = lambda a: pl.BlockSpec(a.shape, lambda i: (0,) * a.ndim)
    rows_last = n_batch is not None
    if rows_last:
        t = m // n_batch
        tiles = t // tm
        kv_spec = pl.BlockSpec((None, KV_W, tm), lambda i: (i // tiles, 0, i % tiles))
        kv_shape = lambda dt: jax.ShapeDtypeStruct((n_batch, KV_W, t), dt)
        extra_specs = [row(KV_W), row(KV_W), kv_spec, kv_spec, pl.BlockSpec((8, KV_W), lambda i: (i, 0))]
        extra_shapes = ([jax.ShapeDtypeStruct((m, KV_W), BF16)] * 2 + [kv_shape(BF16)] * 2
                        + [jax.ShapeDtypeStruct((m // tm * 8, KV_W), F32)])
    else:
        kv_spec, kv_shape = row(KV_W), lambda dt: jax.ShapeDtypeStruct((m, KV_W), dt)
        extra_specs, extra_shapes = [], []
    return pl.pallas_call(
        functools.partial(_nsa_proj_body, tm=tm, d=d, rows_last=rows_last),
        grid=(m // tm,),
        in_specs=[row(d), pl.BlockSpec((1, d), lambda i: (0, 0)), pl.BlockSpec((d, n), lambda i: (0, 0)),
                  full(qkg128), row(LANES), row(LANES), full(gg)],
        out_specs=[row(d), row(d)] + [kv_spec] * 6 + [row(LANES)] + extra_specs,
        out_shape=[jax.ShapeDtypeStruct((m, d), BF16)] * 2 + [kv_shape(F32)] * 6
        + [jax.ShapeDtypeStruct((m, LANES), F32)] + extra_shapes,
        compiler_params=_cparams(1, 48),
        name="nsa_proj",
    )(x, g.reshape(1, d), w_in_pad, qkg128, cos128, sin128, gg)


def _page_copy(src_ref, buf_ref, sem, slot, paged, index, p, dst_page):
    src = src_ref.at[index] if paged else src_ref.at[index, :, pl.ds(p * PAGE_SIZE, PAGE_SIZE)]
    dst = buf_ref.at[slot, :, dst_page, :] if buf_ref.ndim == 4 else buf_ref.at[slot, pl.ds(dst_page * KV_W, KV_W)]
    return pltpu.make_async_copy(src, dst, sem.at[slot])


def _gather_start(pt_ref, src_ref, buf_ref, sem, step, slot, n_pages, n_seq, paged):
    for s in range(n_seq):
        seq = step * n_seq + s
        for p in range(n_pages):
            index = pt_ref[seq, p] if paged else seq
            _page_copy(src_ref, buf_ref, sem, slot, paged, index, p, s * n_pages + p).start()


def _gather_wait(src_ref, buf_ref, sem, slot, n_pages, n_seq, paged):
    for s in range(n_seq):
        for p in range(n_pages):
            _page_copy(src_ref, buf_ref, sem, slot, paged, 0, p, s * n_pages + p).wait()


def _compress_body(pt_ref, src_ref, pos_ref, wd_ref, bd2_ref, gain_ref, gg_ref, o_ref,
                   buf_ref, sem, acc_ref, *, n_pages, n_seq, paged, normalize):
    step = pl.program_id(0)
    n_steps = pl.num_programs(0)
    slot = step % 2
    m = n_seq * n_pages
    unroll = 8

    @pl.when(step == 0)
    def _():
        _gather_start(pt_ref, src_ref, buf_ref, sem, 0, 0, n_pages, n_seq, paged)

    @pl.when(step + 1 < n_steps)
    def _():
        _gather_start(pt_ref, src_ref, buf_ref, sem, step + 1, 1 - slot, n_pages, n_seq, paged)

    _gather_wait(src_ref, buf_ref, sem, slot, n_pages, n_seq, paged)
    acc_ref[...] = jnp.zeros_like(acc_ref)

    def chunk(c, carry):
        accs = [acc_ref[g] for g in range(N_KV)]
        for dd in range(unroll):
            d = c * unroll + dd
            w = wd_ref[d]
            pos = pos_ref[pl.ds(d, 1), :]
            for g in range(N_KV):
                rows = buf_ref[slot, g * HEAD_DIM + d] + pos
                accs[g] = accs[g] + _dot(rows.astype(BF16), w)
        for g in range(N_KV):
            acc_ref[g] = accs[g]
        return carry

    lax.fori_loop(0, HEAD_DIM // unroll, chunk, 0)

    for g in range(N_KV):
        y = _dot(jax.nn.gelu(acc_ref[g]).astype(BF16), bd2_ref[...])
        if normalize:
            y = _head_norm(y, gain_ref[...], gg_ref[...])
        for s in range(n_seq):
            o_ref[s, g] = y[s * n_pages:(s + 1) * n_pages].astype(BF16)


def _compress(page_table, src, pos_t, wd, bd2, gain128, gg, normalize, paged, n_seq):
    nb, n_pages = page_table.shape
    hid2 = wd.shape[2]
    full = lambda a: pl.BlockSpec(a.shape, lambda i, pt: (0,) * a.ndim)
    grid_spec = pltpu.PrefetchScalarGridSpec(
        num_scalar_prefetch=1,
        grid=(nb // n_seq,),
        in_specs=[pl.BlockSpec(memory_space=pl.ANY), full(pos_t), full(wd), full(bd2), full(gain128), full(gg)],
        out_specs=pl.BlockSpec((n_seq, N_KV, n_pages, LANES), lambda i, pt: (i, 0, 0, 0)),
        scratch_shapes=[
            pltpu.VMEM((2, KV_W, n_seq * n_pages, PAGE_SIZE), F32),
            pltpu.SemaphoreType.DMA((2,)),
            pltpu.VMEM((N_KV, n_seq * n_pages, hid2), F32),
        ],
    )
    return pl.pallas_call(
        functools.partial(_compress_body, n_pages=n_pages, n_seq=n_seq, paged=paged, normalize=normalize),
        grid_spec=grid_spec,
        out_shape=jax.ShapeDtypeStruct((nb, N_KV, n_pages, LANES), BF16),
        compiler_params=_cparams(1, 48),
        name="compress",
    )(page_table, src, pos_t, wd, bd2, gain128, gg)


def _softmax_cols(s):
    m = jnp.max(s, axis=0, keepdims=True)
    m = jnp.where(m == NEG_INF, 0.0, m)
    e = jnp.exp2(s - m)
    return e, jnp.sum(e, axis=0, keepdims=True)


def _select_bias(imp, cur):
    jj = lax.broadcasted_iota(jnp.int32, imp.shape, 0)
    forced = (jj == 0) | (jj == cur) | (jj == cur - 1)
    cand = (jj <= cur) & jnp.logical_not(forced)
    v = jnp.where(cand, imp, -1.0)
    bias = jnp.where(forced, 0.0, NEG_INF)
    for _ in range(TOP_N - N_FORCED):
        m = jnp.max(v, axis=0, keepdims=True)
        idx = jnp.min(jnp.where(v == m, jj, imp.shape[0]), axis=0, keepdims=True)
        idx = jnp.where(m >= 0.0, idx, -1)
        pick = jj == idx
        bias = jnp.where(pick, 0.0, bias)
        v = jnp.where(pick, -1.0, v)
    return bias


def _attn_prompt_body(qn_ref, qr_ref, gt_ref, kn_ref, kc_ref, vct_ref, ks_ref, blk_ref, vst_ref, kw_ref, vwt_ref,
                      o_ref, qsel_ref):
    i = pl.program_id(2)
    q_start = i * QBLK
    n_col = HPG * QBLK
    q_pos = q_start + lax.broadcasted_iota(jnp.int32, (1, n_col), 1) % QBLK

    in_half = lax.broadcasted_iota(jnp.int32, (LANES, 1), 0) // HEAD_DIM == pl.program_id(1) % 2
    place = lambda q64: jnp.where(in_half, jnp.concatenate([q64, q64], axis=0), jnp.zeros((), BF16))
    qn = place(qn_ref[...])
    qr = place(qr_ref[...])

    n_cmp = kc_ref.shape[0]
    s = _dot(kc_ref[...], qn)
    cmp_end = lax.broadcasted_iota(jnp.int32, (n_cmp, 1), 0) * CMP_BLK + (CMP_BLK - 1)
    s = jnp.where(cmp_end <= q_pos, s, NEG_INF)
    e, l = _softmax_cols(s)
    p_cmp = e / jnp.maximum(l, 1e-30)
    o_cmp = _dot(vct_ref[...], p_cmp.astype(BF16))

    imp = p_cmp[:, 0:QBLK]
    for hh in range(1, HPG):
        imp = imp + p_cmp[:, hh * QBLK:(hh + 1) * QBLK]
    bias = jnp.maximum(_select_bias(imp, q_pos[:, 0:QBLK] // SEL_BLK), MASKED)
    bias = jnp.concatenate([bias] * HPG, axis=1)
    if n_cmp < LANES:
        bias = jnp.concatenate([bias, jnp.zeros((LANES - n_cmp, n_col), F32)], axis=0)

    n_win = WINDOW + QBLK
    w0 = pl.multiple_of(q_start, QBLK)
    s = _dot(kw_ref[pl.ds(w0, n_win), :], qr)
    r = lax.broadcasted_iota(jnp.int32, (n_win, QBLK), 0)
    q = lax.broadcasted_iota(jnp.int32, (n_win, QBLK), 1)
    valid = (r > q) & (r <= q + WINDOW) & (r + q_start >= WINDOW)
    wbias = jnp.where(valid, 0.0, NEG_INF)
    s = s + jnp.concatenate([wbias] * HPG, axis=1)
    e, l = _softmax_cols(s)
    o_win = _dot(vwt_ref[:, pl.ds(w0, n_win)], e.astype(BF16)) / jnp.maximum(l, 1e-30)

    own = lax.broadcasted_iota(jnp.int32, (LANES, 1), 0) >= q_start // SEL_BLK
    masked = jnp.full((), MASKED, BF16)
    own_keys = jnp.concatenate([ks_ref[pl.ds(w0, QBLK), :], blk_ref[pl.ds(w0, QBLK), :]], axis=1)
    tri = jnp.where(lax.broadcasted_iota(jnp.int32, (QBLK, QBLK), 0)
                    <= lax.broadcasted_iota(jnp.int32, (QBLK, QBLK), 1), 0.0, NEG_INF)
    tri = jnp.concatenate([tri] * HPG, axis=1)
    n_tiles = (q_start + KTILE - 1) // KTILE

    def tile_scores(k0):
        keys = jnp.concatenate([ks_ref[pl.ds(k0, KTILE), :], blk_ref[pl.ds(k0, KTILE), :]], axis=1)
        return _dot(keys, qsel_ref[...])

    q32 = qr.astype(F32)
    bound = jnp.sqrt(jnp.sum(q32 * q32, axis=0, keepdims=True)) * (kn_ref[0:1, 0:1] * BOUND_SLACK)
    shifted = (bias - bound).astype(BF16)
    qsel_ref[0:LANES, :] = qr
    qsel_ref[LANES:, :] = jnp.where(own, masked, shifted)

    def values_and_ones(k0, n):
        return jnp.concatenate([vst_ref[:, pl.ds(k0, n)], jnp.ones((ONES_ROWS, n), BF16)], axis=0)

    def fast_tile(t, acc):
        k0 = pl.multiple_of(t * KTILE, KTILE)
        return acc + _dot(values_and_ones(k0, KTILE), jnp.exp2(tile_scores(k0)).astype(BF16))

    s = _dot(own_keys, jnp.concatenate([qr, shifted], axis=0)) + tri
    acc = _dot(values_and_ones(w0, QBLK), jnp.exp2(s).astype(BF16))
    acc = lax.fori_loop(0, n_tiles // 2, lambda u, a: fast_tile(2 * u + 1, fast_tile(2 * u, a)), acc)
    acc = lax.fori_loop(0, n_tiles % 2, lambda _, a: fast_tile(n_tiles - 1, a), acc)
    l_fast = acc[HEAD_DIM:HEAD_DIM + 1]

    def exact(_):
        bias16 = bias.astype(BF16)
        qsel_ref[LANES:, :] = jnp.where(own, masked, bias16)
        s = _dot(own_keys, jnp.concatenate([qr, bias16], axis=0)) + tri
        m0 = jnp.max(s, axis=0, keepdims=True)
        p = jnp.exp2(s - m0)
        init = (m0, jnp.sum(p, axis=0, keepdims=True), _dot(vst_ref[:, pl.ds(w0, QBLK)], p.astype(BF16)))

        def tile(t, carry):
            m, l, a = carry
            k0 = pl.multiple_of(t * KTILE, KTILE)
            s = tile_scores(k0)
            m_new = jnp.maximum(m, jnp.max(s, axis=0, keepdims=True))
            alpha = jnp.exp2(m - m_new)
            p = jnp.exp2(s - m_new)
            l = alpha * l + jnp.sum(p, axis=0, keepdims=True)
            a = alpha * a + _dot(vst_ref[:, pl.ds(k0, KTILE)], p.astype(BF16))
            return m_new, l, a

        _, l, a = lax.fori_loop(0, n_tiles, tile, init)
        return a / l

    o_sel = lax.cond(jnp.min(l_fast) >= MIN_DENOM, lambda _: acc[:HEAD_DIM] / l_fast, exact, 0)

    gt = gt_ref[...]
    o_ref[...] = (gt[0:1, :] * o_cmp + gt[1:2, :] * o_sel + gt[2:3, :] * o_win).astype(BF16)


def _attn_prompt(qn_t, qr_t, gt_t, k_norm, kc, vc_t, ks, blk_onehot, vs_t, kw_pad, vw_pad_t):
    nb, ng, nqb, hd, n_col = qn_t.shape
    n_cmp = kc.shape[1]
    assert n_cmp <= LANES
    per_q = lambda r: pl.BlockSpec((None, None, None, r, n_col), lambda b, g, i: (b, g, i, 0, 0))
    pair = lambda a: pl.BlockSpec((None, a.shape[1], LANES), lambda b, g, i: (b, 0, g // 2))
    group_rows = lambda a: pl.BlockSpec((None, hd, a.shape[2]), lambda b, g, i: (b, g, 0))
    return pl.pallas_call(
        _attn_prompt_body,
        grid=(nb, ng, nqb),
        in_specs=[per_q(hd), per_q(hd), per_q(8),
                  pl.BlockSpec((None, None, 8, LANES), lambda b, g, i: (b, g, 0, 0)),
                  pair(kc), group_rows(vc_t), pair(ks),
                  pl.BlockSpec(blk_onehot.shape, lambda b, g, i: (0, 0)), group_rows(vs_t),
                  pair(kw_pad), group_rows(vw_pad_t)],
        out_specs=per_q(hd),
        out_shape=jax.ShapeDtypeStruct((nb, ng, nqb, hd, n_col), BF16),
        scratch_shapes=[pltpu.VMEM((2 * LANES, n_col), BF16)],
        compiler_params=_cparams(3, 48),
        name="attn_prompt",
    )(qn_t, qr_t, gt_t, k_norm, kc, vc_t, ks, blk_onehot, vs_t, kw_pad, vw_pad_t)


def _attn_sample_body(pt_ref, qn_ref, qr_ref, gt_ref, kc_ref, vc_ref, kwin_ref, vwin_ref,
                      ksn_ref, vsn_ref, kwn_ref, vwn_ref, spread_ref, kpool_ref, vpool_ref, o_ref,
                      kbuf_ref, vbuf_ref, ksem, vsem, s_ref, *, n_pages, n_new, past_len):
    b = pl.program_id(0)
    nb = pl.num_programs(0)
    slot = b % 2

    @pl.when(b == 0)
    def _():
        _gather_start(pt_ref, kpool_ref, kbuf_ref, ksem, 0, 0, n_pages, 1, True)
        _gather_start(pt_ref, vpool_ref, vbuf_ref, vsem, 0, 0, n_pages, 1, True)

    @pl.when(b + 1 < nb)
    def _():
        _gather_start(pt_ref, kpool_ref, kbuf_ref, ksem, b + 1, 1 - slot, n_pages, 1, True)
        _gather_start(pt_ref, vpool_ref, vbuf_ref, vsem, b + 1, 1 - slot, n_pages, 1, True)

    qn = qn_ref[0]
    qr = qr_ref[0]
    n_col = qr.shape[0]
    n_past = n_pages * PAGE_SIZE
    n_blk = n_past // SEL_BLK
    n_gq = N_KV * n_new
    dot_nt = lambda a, bt: lax.dot_general(a, bt, (((1,), (1,)), ((), ())), preferred_element_type=F32)

    lane = lax.broadcasted_iota(jnp.int32, (1, LANES), 1)
    lane_pos = past_len + lane % n_new
    s = _dot(kc_ref[0], qn)
    cmp_end = lax.broadcasted_iota(jnp.int32, (n_blk, 1), 0) * CMP_BLK + (CMP_BLK - 1)
    s = jnp.where(cmp_end <= lane_pos, s, NEG_INF)
    e, l = _softmax_cols(s)
    p_cmp = e / jnp.maximum(l, 1e-30)
    o_cmp = _dot(p_cmp.T.astype(BF16), vc_ref[0])[:n_col]

    imp = p_cmp
    for hh in range(1, HPG):
        imp = imp + pltpu.roll(p_cmp, LANES - hh * n_gq, 1)
    sel = _select_bias(imp, lane_pos // SEL_BLK)
    sel01 = jnp.where((lane < n_gq) & (sel == 0.0), 1.0, 0.0)
    spread = sel01
    for hh in range(1, HPG):
        spread = spread + pltpu.roll(sel01, hh * n_gq, 1)
    sel_bias = jnp.where(spread.T[:n_col] > 0.5, 0.0, MASKED).astype(BF16)

    row = lax.broadcasted_iota(jnp.int32, (n_col, 1), 0)
    q_idx = row % n_new
    q_pos = past_len + q_idx
    n_pad = ksn_ref.shape[1]
    new_idx = lax.broadcasted_iota(jnp.int32, (1, n_pad), 1)
    new_ok = (new_idx <= q_idx) & (new_idx < n_new)

    _gather_wait(kpool_ref, kbuf_ref, ksem, slot, n_pages, 1, True)
    _gather_wait(vpool_ref, vbuf_ref, vsem, slot, n_pages, 1, True)
    pages_per_chunk = 8
    chunk = pages_per_chunk * PAGE_SIZE

    def page_rows(c, pp):
        return pl.ds(pl.multiple_of((c * pages_per_chunk + pp) * KV_W, KV_W), KV_W)

    def score_chunk(c, m):
        k0 = pl.multiple_of(c * chunk, chunk)
        s = jnp.concatenate([_dot(qr, kbuf_ref[slot, page_rows(c, pp), :].astype(BF16))
                             for pp in range(pages_per_chunk)], axis=1)
        s = s + _dot(sel_bias, spread_ref[:, pl.ds(k0, chunk)])
        s_ref[:, pl.ds(k0, chunk)] = s
        return jnp.maximum(m, jnp.max(s, axis=1, keepdims=True))

    m = lax.fori_loop(0, n_pages // pages_per_chunk, score_chunk, jnp.full((n_col, 1), NEG_INF, F32))
    s_new = jnp.where(new_ok, dot_nt(qr, ksn_ref[0].astype(BF16)), NEG_INF)
    m = jnp.maximum(m, jnp.max(s_new, axis=1, keepdims=True))
    m = jnp.where(m == NEG_INF, 0.0, m)

    def value_chunk(c, carry):
        l, acc = carry
        k0 = pl.multiple_of(c * chunk, chunk)
        p = jnp.exp2(s_ref[:, pl.ds(k0, chunk)] - m)
        l = l + jnp.sum(p, axis=1, keepdims=True)
        p = p.astype(BF16)
        for pp in range(pages_per_chunk):
            acc = acc + dot_nt(p[:, pp * PAGE_SIZE:(pp + 1) * PAGE_SIZE],
                               vbuf_ref[slot, page_rows(c, pp), :].astype(BF16))
        return l, acc

    p_new = jnp.exp2(s_new - m)
    l_sel, acc = lax.fori_loop(
        0, n_pages // pages_per_chunk, value_chunk,
        (jnp.sum(p_new, axis=1, keepdims=True), _dot(p_new.astype(BF16), vsn_ref[0].astype(BF16))))
    o_sel = acc / jnp.maximum(l_sel, 1e-30)

    n_buf = kwin_ref.shape[2]
    s_buf = _dot(qr, kwin_ref[0].astype(BF16))
    buf_pos = past_len - n_buf + lax.broadcasted_iota(jnp.int32, (1, n_buf), 1)
    diff = q_pos - buf_pos
    s_buf = jnp.where((diff >= 0) & (diff < WINDOW) & (buf_pos >= 0), s_buf, NEG_INF)
    s_nw = jnp.where(new_ok, dot_nt(qr, kwn_ref[0].astype(BF16)), NEG_INF)
    m = jnp.maximum(jnp.max(s_buf, axis=1, keepdims=True), jnp.max(s_nw, axis=1, keepdims=True))
    m = jnp.where(m == NEG_INF, 0.0, m)
    p_buf = jnp.exp2(s_buf - m)
    p_nw = jnp.exp2(s_nw - m)
    l_win = jnp.sum(p_buf, axis=1, keepdims=True) + jnp.sum(p_nw, axis=1, keepdims=True)
    o_win = (dot_nt(p_buf.astype(BF16), vwin_ref[0].astype(BF16))
             + _dot(p_nw.astype(BF16), vwn_ref[0].astype(BF16))) / jnp.maximum(l_win, 1e-30)

    gt = gt_ref[0]
    o = gt[:, 0:1] * o_cmp + gt[:, 1:2] * o_sel + gt[:, 2:3] * o_win
    row_g = (lax.broadcasted_iota(jnp.int32, (n_col, KV_W), 0) // n_new) % N_KV
    lane_g = lax.broadcasted_iota(jnp.int32, (n_col, KV_W), 1) // HEAD_DIM
    o = jnp.where(row_g == lane_g, o, 0.0)
    z = o[:, :LANES] + o[:, LANES:]
    o_ref[0] = z + pltpu.roll(z, HEAD_DIM, 1)


def _attn_sample(page_table, qn_bd, qr_bd_t, gt_col, kc, vc, kwin_t, vwin_t, ks_new, vs_new, kw_new, vw_new,
                 blk_spread, k_pool, v_pool, n_new, past_len):
    nb, n_pages = page_table.shape
    n_past = n_pages * PAGE_SIZE
    n_col = qr_bd_t.shape[1]
    per_b = lambda a: pl.BlockSpec((1,) + a.shape[1:], lambda b, pt: (b,) + (0,) * (a.ndim - 1))
    any_spec = pl.BlockSpec(memory_space=pl.ANY)
    ins = (qn_bd, qr_bd_t, gt_col, kc, vc, kwin_t, vwin_t, ks_new, vs_new, kw_new, vw_new)
    grid_spec = pltpu.PrefetchScalarGridSpec(
        num_scalar_prefetch=1,
        grid=(nb,),
        in_specs=[per_b(a) for a in ins]
        + [pl.BlockSpec(blk_spread.shape, lambda b, pt: (0, 0)), any_spec, any_spec],
        out_specs=pl.BlockSpec((1, n_col, LANES), lambda b, pt: (b, 0, 0)),
        scratch_shapes=[
            pltpu.VMEM((2, n_pages * KV_W, PAGE_SIZE), F32),
            pltpu.VMEM((2, n_pages * KV_W, PAGE_SIZE), F32),
            pltpu.SemaphoreType.DMA((2,)),
            pltpu.SemaphoreType.DMA((2,)),
            pltpu.VMEM((n_col, n_past), F32),
        ],
    )
    return pl.pallas_call(
        functools.partial(_attn_sample_body, n_pages=n_pages, n_new=n_new, past_len=past_len),
        grid_spec=grid_spec,
        out_shape=jax.ShapeDtypeStruct((nb, n_col, LANES), F32),
        compiler_params=_cparams(1, 56),
        name="attn_sample",
    )(page_table, *ins, blk_spread, k_pool, v_pool)


def _rope_tables(pos):
    half = HEAD_DIM // 2
    inv = ROPE_THETA ** (-jnp.arange(half, dtype=F32) / half)
    ang = pos.astype(F32)[:, None] * inv
    cos, sin = jnp.cos(ang), jnp.sin(ang)
    reps = LANES // HEAD_DIM
    return (jnp.tile(jnp.concatenate([cos, cos], axis=1), (1, reps)),
            jnp.tile(jnp.concatenate([-sin, sin], axis=1), (1, reps)))


def _group_sum_matrix():
    lane = jnp.arange(LANES)
    g = (lane[:, None] // HEAD_DIM == lane[None, :] // HEAD_DIM).astype(BF16)
    return jnp.concatenate([g, g], axis=0)


def _compress_weights(cmp_pos, cmp_w1, cmp_w2):
    n_pair = PAGE_SIZE // CMP_BLK
    eye = jnp.eye(n_pair, dtype=F32)
    hid = cmp_w1.shape[1]
    w1 = cmp_w1.reshape(CMP_BLK, HEAD_DIM, hid)
    wd = jnp.einsum("bk,rdh->dbrkh", eye, w1).reshape(HEAD_DIM, PAGE_SIZE, n_pair * hid).astype(BF16)
    bd2 = jnp.einsum("bk,hd->bhkd", eye, cmp_w2).reshape(n_pair * hid, n_pair * HEAD_DIM).astype(BF16)
    return jnp.tile(cmp_pos.T, (1, n_pair)), wd, bd2


def _to_rows_major(a_t):
    nb, _, rows = a_t.shape
    return a_t.reshape(nb, N_KV, HEAD_DIM, rows).transpose(0, 3, 1, 2)[None]


def _from_rows_major(a):
    n, rows = a.shape[:2]
    return a.transpose(0, 2, 3, 1).reshape(n, KV_W, rows)


def _layer_tail(x, p, norm_g, w_gu, w_down, w_ple_proj, w_ple_gate):
    x = _ffn(x, norm_g[2], w_gu[1], w_down[1])
    return _ple(x, norm_g[3], p, w_ple_gate, w_ple_proj)


def kernel(x_prompt, x_sample, state_conv, cache_k_cmp, cache_v_cmp, cache_k_sel, cache_v_sel,
           cache_k_win, cache_v_win, page_table, p_prompt, p_sample, norm_g, ffn_w_gu, ffn_w_down,
           ple_w_proj, ple_w_gate, conv_w_in, conv_w, conv_w_out, nsa_w_in, nsa_qk_g, nsa_cmp_pos,
           nsa_cmp_w1, nsa_cmp_w2, nsa_w_out):
    nb_p, t_p, d = x_prompt.shape
    nb_s, t_s, _ = x_sample.shape
    ple_dim = p_prompt.shape[-1]
    past_len = page_table.shape[1] * PAGE_SIZE
    assert t_p % KTILE == 0 and t_p % PAGE_SIZE == 0 and d % LANES == 0

    w_gu = ffn_w_gu.astype(BF16)
    w_down = ffn_w_down.astype(BF16)
    w_pp = ple_w_proj.astype(BF16)
    w_pg = ple_w_gate.astype(BF16)

    xp = x_prompt.reshape(nb_p * t_p, d)
    xs = x_sample.reshape(nb_s * t_s, d)
    pp = p_prompt.reshape(-1, nb_p * t_p, ple_dim)
    ps = p_sample.reshape(-1, nb_s * t_s, ple_dim)

    lw = (norm_g[0], w_gu[0], w_down[0], w_pp[0], w_pg[0])
    cw = (conv_w_in[0].astype(BF16), conv_w[0], conv_w_out[0].astype(BF16))
    xp = _ffn(xp, norm_g[0, 0], w_gu[0, 0], w_down[0, 0])
    xs = _ffn(xs, norm_g[0, 0], w_gu[0, 0], w_down[0, 0])
    xp3, conv_p = _conv_mixer(xp.reshape(nb_p, t_p, d), jnp.zeros((nb_p, CONV_W - 1, d), F32),
                              norm_g[0, 1], *cw, shift=1)
    xs_tb = xs.reshape(nb_s, t_s, d).transpose(1, 0, 2).reshape(1, t_s * nb_s, d)
    prev_tb = state_conv[0].transpose(1, 0, 2).reshape(1, (CONV_W - 1) * nb_s, d)
    xs3, conv_s = _conv_mixer(xs_tb, prev_tb, norm_g[0, 1], *cw, shift=nb_s)
    xs = xs3.reshape(t_s, nb_s, d).transpose(1, 0, 2).reshape(nb_s * t_s, d)
    conv_s = conv_s.reshape(CONV_W - 1, nb_s, d).transpose(1, 0, 2)
    xp = _layer_tail(xp3.reshape(nb_p * t_p, d), pp[0], *lw)
    xs = _layer_tail(xs, ps[0], *lw)

    lw = (norm_g[1], w_gu[1], w_down[1], w_pp[1], w_pg[1])
    xp = _ffn(xp, norm_g[1, 0], w_gu[1, 0], w_down[1, 0])
    xs = _ffn(xs, norm_g[1, 0], w_gu[1, 0], w_down[1, 0])

    n_in = nsa_w_in.shape[-1]
    n_in_pad = -(-n_in // LANES) * LANES
    w_in = jnp.pad(nsa_w_in[0], ((0, 0), (0, n_in_pad - n_in))).astype(BF16)
    w_out = nsa_w_out[0].astype(BF16)
    qkg128 = jnp.tile(nsa_qk_g[0], (1, LANES // HEAD_DIM))
    gg = _group_sum_matrix()
    cos_p, sin_p = _rope_tables(jnp.arange(t_p))
    cos_s, sin_s = _rope_tables(past_len + jnp.arange(t_s))
    pos_k, wd_k, bd2_k = _compress_weights(nsa_cmp_pos[0, 0], nsa_cmp_w1[0, 0], nsa_cmp_w2[0, 0])
    pos_v, wd_v, bd2_v = _compress_weights(nsa_cmp_pos[0, 1], nsa_cmp_w1[0, 1], nsa_cmp_w2[0, 1])
    k_gain = qkg128[1:2]

    qn, qr, kc_t, vc_t, ks_t, vs_t, kw_t, vw_t, gt, ks_b, kw_b, vs_tb, vw_tb, k_norm2 = _nsa_proj(
        xp, norm_g[1, 1], w_in, qkg128, jnp.tile(cos_p, (nb_p, 1)), jnp.tile(sin_p, (nb_p, 1)), gg, n_batch=nb_p)
    pt_p = jnp.zeros((nb_p, t_p // PAGE_SIZE), jnp.int32)
    as_blocks = lambda c: c.reshape(c.shape[0], N_KV, -1, HEAD_DIM)
    by_block = lambda c: as_blocks(c).transpose(0, 2, 1, 3).reshape(c.shape[0], -1, KV_W)
    kc = _compress(pt_p, kc_t, pos_k, wd_k, bd2_k, k_gain, gg, True, False, nb_p)
    vc = _compress(pt_p, vc_t, pos_v, wd_v, bd2_v, k_gain, gg, False, False, nb_p)

    nqb = t_p // QBLK
    n_col = HPG * QBLK

    def q_cols(a):
        a = a.reshape(nb_p, nqb, QBLK, N_KV, HPG, HEAD_DIM).transpose(0, 3, 1, 5, 4, 2)
        return a.reshape(nb_p, N_KV, nqb, HEAD_DIM, n_col)

    gt_t = gt[:, :N_KV * HPG * 3].reshape(nb_p, nqb, QBLK, N_KV, HPG, 3).transpose(0, 3, 1, 5, 4, 2)
    gt_t = jnp.pad(gt_t.reshape(nb_p, N_KV, nqb, 3, n_col), ((0, 0),) * 3 + ((0, 5), (0, 0)))
    blk_onehot = (jnp.arange(t_p)[:, None] // SEL_BLK == jnp.arange(LANES)[None, :]).astype(BF16)
    k_norm = jnp.sqrt(k_norm2.reshape(nb_p, -1, N_KV, HEAD_DIM).max(axis=(1, 3)))
    k_norm = jnp.broadcast_to(k_norm[:, :, None, None], (nb_p, N_KV, 8, LANES))
    o_t = _attn_prompt(
        q_cols(qn), q_cols(qr), gt_t, k_norm,
        by_block(kc), as_blocks(vc).transpose(0, 1, 3, 2).reshape(nb_p, KV_W, -1),
        ks_b.reshape(nb_p, t_p, KV_W), blk_onehot, vs_tb,
        jnp.pad(kw_b.reshape(nb_p, t_p, KV_W), ((0, 0), (WINDOW, 0), (0, 0))),
        jnp.pad(vw_tb, ((0, 0), (0, 0), (WINDOW, 0))))
    o = o_t.reshape(nb_p, N_KV, nqb, HEAD_DIM, HPG, QBLK).transpose(0, 2, 5, 1, 4, 3)
    xp = _proj_res(xp, o.reshape(nb_p * t_p, d), w_out)
    keep = min(WINDOW, t_p)
    nsa_p = tuple(_to_rows_major(a) for a in (kc_t, vc_t, ks_t, vs_t, kw_t[:, :, -keep:], vw_t[:, :, -keep:]))

    qn, qr, kc_r, vc_r, ks, vs, kw, vw, gt = _nsa_proj(
        xs, norm_g[1, 1], w_in, qkg128, jnp.tile(cos_s, (nb_s, 1)), jnp.tile(sin_s, (nb_s, 1)), gg)
    pool_t = lambda c: _from_rows_major(c[0])
    n_seq = 2 if nb_s % 2 == 0 else 1
    kc = by_block(_compress(page_table, pool_t(cache_k_cmp), pos_k, wd_k, bd2_k, k_gain, gg, True, True, n_seq))
    vc = by_block(_compress(page_table, pool_t(cache_v_cmp), pos_v, wd_v, bd2_v, k_gain, gg, False, True, n_seq))

    n_used = HPG * N_KV * t_s

    def q_block_diag(a):
        a = a.reshape(nb_s, t_s, N_KV, HPG, HEAD_DIM).astype(F32)
        return jnp.einsum("bqghd,gk->bgdhkq", a, jnp.eye(N_KV, dtype=F32)).reshape(nb_s, KV_W, n_used)

    qn_bd = jnp.pad(q_block_diag(qn), ((0, 0), (0, 0), (0, LANES - n_used))).astype(BF16)
    qr_bd_t = q_block_diag(qr).transpose(0, 2, 1).astype(BF16)
    gt_col = gt[:, :N_KV * HPG * 3].reshape(nb_s, t_s, N_KV, HPG, 3).transpose(0, 3, 2, 1, 4)
    gt_col = jnp.pad(gt_col.reshape(nb_s, n_used, 3), ((0, 0), (0, 0), (0, LANES - 3)))
    new_rows = lambda a: jnp.pad(a.reshape(nb_s, t_s, KV_W), ((0, 0), (0, 16 - t_s), (0, 0)))
    n_buf = cache_k_win.shape[2]
    kwin_t, vwin_t = _from_rows_major(cache_k_win[0]), _from_rows_major(cache_v_win[0])
    blk_spread = (jnp.arange(past_len // SEL_BLK)[:, None] == jnp.arange(past_len)[None, :] // SEL_BLK).astype(BF16)
    o_s = _attn_sample(
        page_table, qn_bd, qr_bd_t, gt_col, kc, vc, kwin_t, vwin_t,
        new_rows(ks), new_rows(vs), new_rows(kw), new_rows(vw), blk_spread,
        pool_t(cache_k_sel), pool_t(cache_v_sel), t_s, past_len)
    o = o_s[:, :, :HEAD_DIM].reshape(nb_s, HPG, N_KV, t_s, HEAD_DIM).transpose(0, 3, 2, 1, 4)
    xs = _proj_res(xs, o.reshape(nb_s * t_s, d).astype(BF16), w_out)
    st = lambda a: a.reshape(1, nb_s, t_s, N_KV, HEAD_DIM)
    new_t = lambda a: a.reshape(nb_s, t_s, KV_W).transpose(0, 2, 1)
    win = lambda buf_t, new: _to_rows_major(jnp.concatenate([buf_t, new_t(new)], axis=2)[:, :, -n_buf:])
    nsa_s = (st(kc_r), st(vc_r), st(ks), st(vs), win(kwin_t, kw), win(vwin_t, vw))

    xp = _layer_tail(xp, pp[1], *lw)
    xs = _layer_tail(xs, ps[1], *lw)

    return (xp.reshape(nb_p, t_p, d), xs.reshape(nb_s, t_s, d), conv_p[None], conv_s[None]) + nsa_p + nsa_s
```

```python
import functools

import jax
import jax.numpy as jnp
from jax import lax
from jax.experimental import pallas as pl
from jax.experimental.pallas import tpu as pltpu

F32 = jnp.float32
BF16 = jnp.bfloat16

HEAD_DIM = 64
N_KV = 4
HPG = 4
CMP_BLK = 64
SEL_BLK = 64
TOP_N = 16
N_FORCED = 3
WINDOW = 512
PAGE_SIZE = 128
CONV_W = 3
QBLK = 256
KTILE = 512
ROPE_THETA = 10000.0
RMS_EPS = 1e-6
NEG_INF = float("-inf")
MASKED = -1e30
LOG2E = 1.4426950408889634
BOUND_SLACK = 1.02
MIN_DENOM = 2.0 ** -100
ONES_ROWS = 16
KV_W = N_KV * HEAD_DIM
LANES = 128
MIB = 1024 * 1024


def _cparams(n_grid, vmem_mib):
    return pltpu.CompilerParams(
        dimension_semantics=("arbitrary",) * n_grid,
        vmem_limit_bytes=vmem_mib * MIB,
    )


def _rms(x, g):
    return x * lax.rsqrt(jnp.mean(x * x, axis=-1, keepdims=True) + RMS_EPS) * g


def _dot(a, b):
    return jnp.dot(a, b, preferred_element_type=F32)


def _ffn_body(x_ref, g_ref, wg_ref, wu_ref, wd_ref, o_ref, h_ref, acc_ref, *, n_f):
    j = pl.program_id(1)

    @pl.when(j == 0)
    def _():
        h_ref[...] = _rms(x_ref[...], g_ref[...]).astype(BF16)
        acc_ref[...] = jnp.zeros_like(acc_ref)

    h = h_ref[...]
    gate = _dot(h, wg_ref[...])
    up = _dot(h, wu_ref[...])
    act = (jax.nn.silu(gate) * up).astype(BF16)
    acc_ref[...] += _dot(act, wd_ref[...])

    @pl.when(j == n_f - 1)
    def _():
        o_ref[...] = x_ref[...] + 0.5 * acc_ref[...]


def _ffn(x, g, w_gu, w_down):
    m, d = x.shape
    f = w_down.shape[0]
    tm = min(m, 512)
    tf = f // 2
    n_f = f // tf
    return pl.pallas_call(
        functools.partial(_ffn_body, n_f=n_f),
        grid=(m // tm, n_f),
        in_specs=[
            pl.BlockSpec((tm, d), lambda i, j: (i, 0)),
            pl.BlockSpec((1, d), lambda i, j: (0, 0)),
            pl.BlockSpec((d, tf), lambda i, j: (0, j)),
            pl.BlockSpec((d, tf), lambda i, j: (0, n_f + j)),
            pl.BlockSpec((tf, d), lambda i, j: (j, 0)),
        ],
        out_specs=pl.BlockSpec((tm, d), lambda i, j: (i, 0)),
        out_shape=jax.ShapeDtypeStruct((m, d), F32),
        scratch_shapes=[pltpu.VMEM((tm, d), BF16), pltpu.VMEM((tm, d), F32)],
        compiler_params=_cparams(2, 48),
        name="ffn",
    )(x, g.reshape(1, d), w_gu, w_gu, w_down)


def _ple_body(x_ref, g_ref, p_ref, wg_ref, wp_ref, o_ref):
    x = x_ref[...]
    h = _rms(x, g_ref[...]).astype(BF16)
    gate = jax.nn.sigmoid(_dot(h, wg_ref[...]))
    o_ref[...] = x + gate * _dot(p_ref[...].astype(BF16), wp_ref[...])


def _ple(x, g, p, w_gate, w_proj):
    m, d = x.shape
    pd = p.shape[1]
    tm = min(m, 512)
    return pl.pallas_call(
        _ple_body,
        grid=(m // tm,),
        in_specs=[
            pl.BlockSpec((tm, d), lambda i: (i, 0)),
            pl.BlockSpec((1, d), lambda i: (0, 0)),
            pl.BlockSpec((tm, pd), lambda i: (i, 0)),
            pl.BlockSpec((d, d), lambda i: (0, 0)),
            pl.BlockSpec((pd, d), lambda i: (0, 0)),
        ],
        out_specs=pl.BlockSpec((tm, d), lambda i: (i, 0)),
        out_shape=jax.ShapeDtypeStruct((m, d), F32),
        compiler_params=_cparams(1, 32),
        name="ple",
    )(x, g.reshape(1, d), p, w_gate, w_proj)


def _proj_res_body(x_ref, o_ref, w_ref, y_ref, *, rows_last):
    if rows_last:
        y = lax.dot_general(o_ref[...], w_ref[...], (((0,), (0,)), ((), ())), preferred_element_type=F32)
    else:
        y = _dot(o_ref[...], w_ref[...])
    y_ref[...] = x_ref[...] + y


def _proj_res(x, o, w):
    m, d = x.shape
    tm = min(m, 512)
    rows_last = o.ndim == 3
    if rows_last:
        k = o.shape[1]
        tiles = o.shape[2] // tm
        o_spec = pl.BlockSpec((None, k, tm), lambda i: (i // tiles, 0, i % tiles))
    else:
        k = o.shape[1]
        o_spec = pl.BlockSpec((tm, k), lambda i: (i, 0))
    return pl.pallas_call(
        functools.partial(_proj_res_body, rows_last=rows_last),
        grid=(m // tm,),
        in_specs=[
            pl.BlockSpec((tm, d), lambda i: (i, 0)),
            o_spec,
            pl.BlockSpec((k, d), lambda i: (0, 0)),
        ],
        out_specs=pl.BlockSpec((tm, d), lambda i: (i, 0)),
        out_shape=jax.ShapeDtypeStruct((m, d), F32),
        compiler_params=_cparams(1, 32),
        name="proj_res",
    )(x, o, w)


def _conv_body(x_ref, prev_ref, g_ref, win_ref, wc_ref, wout_ref, y_ref, st_ref, u_ref,
               *, shift, halo, tm, d):
    t = pl.program_id(1)
    lo = halo - 2 * shift

    @pl.when(t == 0)
    def _():
        u_ref[lo:halo, :] = prev_ref[0]

    x = x_ref[0]
    h = _rms(x, g_ref[...]).astype(BF16)
    proj = _dot(h, win_ref[...])
    b_gate = proj[:, :d]
    u = proj[:, d:2 * d] * proj[:, 2 * d:]
    u_ref[halo:halo + tm, :] = u
    wc = wc_ref[...]
    y = (wc[0:1, :] * u_ref[lo:lo + tm, :]
         + wc[1:2, :] * u_ref[lo + shift:lo + shift + tm, :]
         + wc[2:3, :] * u)
    y_ref[0] = x + _dot((b_gate * y).astype(BF16), wout_ref[...])
    tail = u_ref[lo + tm:halo + tm, :]
    st_ref[0] = tail
    u_ref[lo:halo, :] = tail


def _conv_mixer(x, prev, g, w_in, w_conv, w_out, shift):
    nb, t, d = x.shape
    tm = min(t, 512)
    halo = -(-2 * shift // 8) * 8
    return pl.pallas_call(
        functools.partial(_conv_body, shift=shift, halo=halo, tm=tm, d=d),
        grid=(nb, t // tm),
        in_specs=[
            pl.BlockSpec((1, tm, d), lambda b, i: (b, i, 0)),
            pl.BlockSpec((1, 2 * shift, d), lambda b, i: (b, 0, 0)),
            pl.BlockSpec((1, d), lambda b, i: (0, 0)),
            pl.BlockSpec((d, 3 * d), lambda b, i: (0, 0)),
            pl.BlockSpec((CONV_W, d), lambda b, i: (0, 0)),
            pl.BlockSpec((d, d), lambda b, i: (0, 0)),
        ],
        out_specs=[
            pl.BlockSpec((1, tm, d), lambda b, i: (b, i, 0)),
            pl.BlockSpec((1, 2 * shift, d), lambda b, i: (b, 0, 0)),
        ],
        out_shape=[
            jax.ShapeDtypeStruct((nb, t, d), F32),
            jax.ShapeDtypeStruct((nb, 2 * shift, d), F32),
        ],
        scratch_shapes=[pltpu.VMEM((halo + tm, d), F32)],
        compiler_params=_cparams(2, 48),
        name="conv_mixer",
    )(x, prev, g.reshape(1, d), w_in, w_conv, w_out)


def _head_norm(xc, gain, gg):
    sq = xc * xc
    hi = sq.astype(BF16)
    lo = (sq - hi.astype(F32)).astype(BF16)
    ss = _dot(jnp.concatenate([hi, lo], axis=1), gg)
    return xc * lax.rsqrt(ss * (1.0 / HEAD_DIM) + RMS_EPS) * gain


def _nsa_proj_body(x_ref, g_ref, w_ref, qkg_ref, cos_ref, sin_ref, gg_ref,
                   qn_ref, qr_ref, kc_ref, vc_ref, ks_ref, vs_ref, kw_ref, vw_ref, gt_ref, *extra_refs,
                   tm, d, rows_last):
    h = _rms(x_ref[...], g_ref[...]).astype(BF16)
    proj = _dot(h, w_ref[...])
    cos = cos_ref[...]
    sin = sin_ref[...]
    gg = gg_ref[...]
    lane = lax.broadcasted_iota(jnp.int32, (tm, LANES), 1)
    first_half = (lane % HEAD_DIM) < (HEAD_DIM // 2)
    scale = HEAD_DIM ** -0.5 * LOG2E

    def rope(xc):
        swapped = jnp.where(first_half, pltpu.roll(xc, LANES - HEAD_DIM // 2, 1),
                            pltpu.roll(xc, HEAD_DIM // 2, 1))
        return xc * cos + swapped * sin

    def slab(c):
        return proj[:, c * LANES:(c + 1) * LANES]

    n_q = d // LANES
    for c in range(n_q):
        qc = _head_norm(slab(c), qkg_ref[0:1, :], gg)
        if rows_last:
            qn_ref[c * LANES:(c + 1) * LANES, :] = (qc * scale).T.astype(BF16)
            qr_ref[c * LANES:(c + 1) * LANES, :] = (rope(qc) * scale).T.astype(BF16)
        else:
            qn_ref[:, c * LANES:(c + 1) * LANES] = (qc * scale).astype(BF16)
            qr_ref[:, c * LANES:(c + 1) * LANES] = (rope(qc) * scale).astype(BF16)
    def put(ref, c, val):
        if rows_last:
            ref[c * LANES:(c + 1) * LANES, :] = val
        else:
            ref[:, c * LANES:(c + 1) * LANES] = val

    for c in range(2):
        lanes = slice(c * LANES, (c + 1) * LANES)
        k_sel = rope(_head_norm(slab(n_q + 4 + c), qkg_ref[2:3, :], gg))
        k_win = rope(_head_norm(slab(n_q + 8 + c), qkg_ref[3:4, :], gg))
        vals = (slab(n_q + c), slab(n_q + 2 + c), k_sel, slab(n_q + 6 + c), k_win, slab(n_q + 10 + c))
        if rows_last:
            vals = tuple(v.T for v in vals)
            ksb_ref, kwb_ref, vsb_ref, vwb_ref, kn_ref = extra_refs
            k16 = k_sel.astype(BF16)
            ksb_ref[:, lanes] = k16
            k32 = k16.astype(F32)
            sq = k32 * k32
            hi = sq.astype(BF16)
            norm2 = _dot(jnp.concatenate([hi, (sq - hi.astype(F32)).astype(BF16)], axis=1), gg)
            kn_ref[:, lanes] = jnp.broadcast_to(jnp.max(norm2, axis=0, keepdims=True), (8, LANES))
            kwb_ref[:, lanes] = k_win.astype(BF16)
            vsb_ref[lanes, :] = vals[3].astype(BF16)
            vwb_ref[lanes, :] = vals[5].astype(BF16)
        for ref, v in zip((kc_ref, vc_ref, ks_ref, vs_ref, kw_ref, vw_ref), vals):
            put(ref, c, v)
    gates = jax.nn.sigmoid(slab(n_q + 12))
    gt_ref[...] = gates.T if rows_last else gates


def _nsa_proj(x, g, w_in_pad, qkg128, cos128, sin128, gg, n_batch=None):
    m, d = x.shape
    n = w_in_pad.shape[1]
    tm = min(m, 512)
    row = lambda w: pl.BlockSpec((tm, w), lambda i: (i, 0))
    full = lambda a: pl.BlockSpec(a.shape, lambda i: (0,) * a.ndim)
    rows_last = n_batch is not None
    if rows_last:
        t = m // n_batch
        tiles = t // tm
        t_spec = lambda w: pl.BlockSpec((None, w, tm), lambda i: (i // tiles, 0, i % tiles))
        t_shape = lambda w, dt: jax.ShapeDtypeStruct((n_batch, w, t), dt)
        kv_spec, q_spec, gt_spec = t_spec(KV_W), t_spec(d), t_spec(LANES)
        kv_shape = lambda dt: t_shape(KV_W, dt)
        q_shape, gt_shape = t_shape(d, BF16), t_shape(LANES, F32)
        extra_specs = [row(KV_W), row(KV_W), kv_spec, kv_spec, pl.BlockSpec((8, KV_W), lambda i: (i, 0))]
        extra_shapes = ([jax.ShapeDtypeStruct((m, KV_W), BF16)] * 2 + [kv_shape(BF16)] * 2
                        + [jax.ShapeDtypeStruct((m // tm * 8, KV_W), F32)])
    else:
        kv_spec, kv_shape = row(KV_W), lambda dt: jax.ShapeDtypeStruct((m, KV_W), dt)
        q_spec, gt_spec = row(d), row(LANES)
        q_shape, gt_shape = jax.ShapeDtypeStruct((m, d), BF16), jax.ShapeDtypeStruct((m, LANES), F32)
        extra_specs, extra_shapes = [], []
    return pl.pallas_call(
        functools.partial(_nsa_proj_body, tm=tm, d=d, rows_last=rows_last),
        grid=(m // tm,),
        in_specs=[row(d), pl.BlockSpec((1, d), lambda i: (0, 0)), pl.BlockSpec((d, n), lambda i: (0, 0)),
                  full(qkg128), row(LANES), row(LANES), full(gg)],
        out_specs=[q_spec, q_spec] + [kv_spec] * 6 + [gt_spec] + extra_specs,
        out_shape=[q_shape] * 2 + [kv_shape(F32)] * 6 + [gt_shape] + extra_shapes,
        compiler_params=_cparams(1, 48),
        name="nsa_proj",
    )(x, g.reshape(1, d), w_in_pad, qkg128, cos128, sin128, gg)


def _page_copy(src_ref, buf_ref, sem, slot, paged, index, p, dst_page):
    src = src_ref.at[index] if paged else src_ref.at[index, :, pl.ds(p * PAGE_SIZE, PAGE_SIZE)]
    dst = buf_ref.at[slot, :, dst_page, :] if buf_ref.ndim == 4 else buf_ref.at[slot, pl.ds(dst_page * KV_W, KV_W)]
    return pltpu.make_async_copy(src, dst, sem.at[slot])


def _gather_start(pt_ref, src_ref, buf_ref, sem, step, slot, n_pages, n_seq, paged):
    for s in range(n_seq):
        seq = step * n_seq + s
        for p in range(n_pages):
            index = pt_ref[seq, p] if paged else seq
            _page_copy(src_ref, buf_ref, sem, slot, paged, index, p, s * n_pages + p).start()


def _gather_wait(src_ref, buf_ref, sem, slot, n_pages, n_seq, paged):
    for s in range(n_seq):
        for p in range(n_pages):
            _page_copy(src_ref, buf_ref, sem, slot, paged, 0, p, s * n_pages + p).wait()


def _compress_body(pt_ref, src_ref, pos_ref, wd_ref, bd2_ref, gain_ref, gg_ref, o_ref,
                   buf_ref, sem, acc_ref, *, n_pages, n_seq, paged, normalize):
    step = pl.program_id(0)
    n_steps = pl.num_programs(0)
    slot = step % 2
    m = n_seq * n_pages
    unroll = 8

    @pl.when(step == 0)
    def _():
        _gather_start(pt_ref, src_ref, buf_ref, sem, 0, 0, n_pages, n_seq, paged)

    @pl.when(step + 1 < n_steps)
    def _():
        _gather_start(pt_ref, src_ref, buf_ref, sem, step + 1, 1 - slot, n_pages, n_seq, paged)

    _gather_wait(src_ref, buf_ref, sem, slot, n_pages, n_seq, paged)
    acc_ref[...] = jnp.zeros_like(acc_ref)

    def chunk(c, carry):
        accs = [acc_ref[g] for g in range(N_KV)]
        for dd in range(unroll):
            d = c * unroll + dd
            w = wd_ref[d]
            pos = pos_ref[pl.ds(d, 1), :]
            for g in range(N_KV):
                rows = buf_ref[slot, g * HEAD_DIM + d] + pos
                accs[g] = accs[g] + _dot(rows.astype(BF16), w)
        for g in range(N_KV):
            acc_ref[g] = accs[g]
        return carry

    lax.fori_loop(0, HEAD_DIM // unroll, chunk, 0)

    for g in range(N_KV):
        y = _dot(jax.nn.gelu(acc_ref[g]).astype(BF16), bd2_ref[...])
        if normalize:
            y = _head_norm(y, gain_ref[...], gg_ref[...])
        for s in range(n_seq):
            o_ref[s, g] = y[s * n_pages:(s + 1) * n_pages].astype(BF16)


def _compress(page_table, src, pos_t, wd, bd2, gain128, gg, normalize, paged, n_seq):
    nb, n_pages = page_table.shape
    hid2 = wd.shape[2]
    full = lambda a: pl.BlockSpec(a.shape, lambda i, pt: (0,) * a.ndim)
    grid_spec = pltpu.PrefetchScalarGridSpec(
        num_scalar_prefetch=1,
        grid=(nb // n_seq,),
        in_specs=[pl.BlockSpec(memory_space=pl.ANY), full(pos_t), full(wd), full(bd2), full(gain128), full(gg)],
        out_specs=pl.BlockSpec((n_seq, N_KV, n_pages, LANES), lambda i, pt: (i, 0, 0, 0)),
        scratch_shapes=[
            pltpu.VMEM((2, KV_W, n_seq * n_pages, PAGE_SIZE), F32),
            pltpu.SemaphoreType.DMA((2,)),
            pltpu.VMEM((N_KV, n_seq * n_pages, hid2), F32),
        ],
    )
    return pl.pallas_call(
        functools.partial(_compress_body, n_pages=n_pages, n_seq=n_seq, paged=paged, normalize=normalize),
        grid_spec=grid_spec,
        out_shape=jax.ShapeDtypeStruct((nb, N_KV, n_pages, LANES), BF16),
        compiler_params=_cparams(1, 48),
        name="compress",
    )(page_table, src, pos_t, wd, bd2, gain128, gg)


def _softmax_cols(s):
    m = jnp.max(s, axis=0, keepdims=True)
    m = jnp.where(m == NEG_INF, 0.0, m)
    e = jnp.exp2(s - m)
    return e, jnp.sum(e, axis=0, keepdims=True)


def _select_bias(imp, cur):
    jj = lax.broadcasted_iota(jnp.int32, imp.shape, 0)
    forced = (jj == 0) | (jj == cur) | (jj == cur - 1)
    cand = (jj <= cur) & jnp.logical_not(forced)
    v = jnp.where(cand, imp, -1.0)
    bias = jnp.where(forced, 0.0, NEG_INF)
    for _ in range(TOP_N - N_FORCED):
        m = jnp.max(v, axis=0, keepdims=True)
        idx = jnp.min(jnp.where(v == m, jj, imp.shape[0]), axis=0, keepdims=True)
        idx = jnp.where(m >= 0.0, idx, -1)
        pick = jj == idx
        bias = jnp.where(pick, 0.0, bias)
        v = jnp.where(pick, -1.0, v)
    return bias


def _attn_prompt_body(qn_ref, qr_ref, gt_ref, kn_ref, kc_ref, vct_ref, ks_ref, blk_ref, vst_ref, kw_ref, vwt_ref,
                      o_ref, qsel_ref):
    i = pl.program_id(2)
    q_start = i * QBLK
    n_col = HPG * QBLK
    q_pos = q_start + lax.broadcasted_iota(jnp.int32, (1, n_col), 1) % QBLK

    in_half = lax.broadcasted_iota(jnp.int32, (LANES, 1), 0) // HEAD_DIM == pl.program_id(1) % 2
    place = lambda q64: jnp.where(in_half, jnp.concatenate([q64, q64], axis=0), jnp.zeros((), BF16))
    heads_on_lanes = lambda ref: jnp.concatenate(
        [ref[hh * HEAD_DIM:(hh + 1) * HEAD_DIM, :] for hh in range(HPG)], axis=1)
    qn = place(heads_on_lanes(qn_ref))
    qr = place(heads_on_lanes(qr_ref))

    n_cmp = kc_ref.shape[0]
    s = _dot(kc_ref[...], qn)
    cmp_end = lax.broadcasted_iota(jnp.int32, (n_cmp, 1), 0) * CMP_BLK + (CMP_BLK - 1)
    s = jnp.where(cmp_end <= q_pos, s, NEG_INF)
    e, l = _softmax_cols(s)
    p_cmp = e / jnp.maximum(l, 1e-30)
    o_cmp = _dot(vct_ref[...], p_cmp.astype(BF16))

    imp = p_cmp[:, 0:QBLK]
    for hh in range(1, HPG):
        imp = imp + p_cmp[:, hh * QBLK:(hh + 1) * QBLK]
    bias = jnp.maximum(_select_bias(imp, q_pos[:, 0:QBLK] // SEL_BLK), MASKED)
    bias = jnp.concatenate([bias] * HPG, axis=1)
    if n_cmp < LANES:
        bias = jnp.concatenate([bias, jnp.zeros((LANES - n_cmp, n_col), F32)], axis=0)

    n_win = WINDOW + QBLK
    w0 = pl.multiple_of(q_start, QBLK)
    s = _dot(kw_ref[pl.ds(w0, n_win), :], qr)
    r = lax.broadcasted_iota(jnp.int32, (n_win, QBLK), 0)
    q = lax.broadcasted_iota(jnp.int32, (n_win, QBLK), 1)
    valid = (r > q) & (r <= q + WINDOW) & (r + q_start >= WINDOW)
    wbias = jnp.where(valid, 0.0, NEG_INF)
    s = s + jnp.concatenate([wbias] * HPG, axis=1)
    e, l = _softmax_cols(s)
    o_win = _dot(vwt_ref[:, pl.ds(w0, n_win)], e.astype(BF16)) / jnp.maximum(l, 1e-30)

    own = lax.broadcasted_iota(jnp.int32, (LANES, 1), 0) >= q_start // SEL_BLK
    masked = jnp.full((), MASKED, BF16)
    own_keys = jnp.concatenate([ks_ref[pl.ds(w0, QBLK), :], blk_ref[pl.ds(w0, QBLK), :]], axis=1)
    tri = jnp.where(lax.broadcasted_iota(jnp.int32, (QBLK, QBLK), 0)
                    <= lax.broadcasted_iota(jnp.int32, (QBLK, QBLK), 1), 0.0, NEG_INF)
    tri = jnp.concatenate([tri] * HPG, axis=1)
    n_tiles = (q_start + KTILE - 1) // KTILE

    def tile_scores(k0):
        keys = jnp.concatenate([ks_ref[pl.ds(k0, KTILE), :], blk_ref[pl.ds(k0, KTILE), :]], axis=1)
        return _dot(keys, qsel_ref[...])

    q32 = qr.astype(F32)
    bound = jnp.sqrt(jnp.sum(q32 * q32, axis=0, keepdims=True)) * (kn_ref[0:1, 0:1] * BOUND_SLACK)
    shifted = (bias - bound).astype(BF16)
    qsel_ref[0:LANES, :] = qr
    qsel_ref[LANES:, :] = jnp.where(own, masked, shifted)

    def values_and_ones(k0, n):
        return jnp.concatenate([vst_ref[:, pl.ds(k0, n)], jnp.ones((ONES_ROWS, n), BF16)], axis=0)

    def fast_tile(t, acc):
        k0 = pl.multiple_of(t * KTILE, KTILE)
        return acc + _dot(values_and_ones(k0, KTILE), jnp.exp2(tile_scores(k0)).astype(BF16))

    s = _dot(own_keys, jnp.concatenate([qr, shifted], axis=0)) + tri
    acc = _dot(values_and_ones(w0, QBLK), jnp.exp2(s).astype(BF16))
    acc = lax.fori_loop(0, n_tiles // 2, lambda u, a: fast_tile(2 * u + 1, fast_tile(2 * u, a)), acc)
    acc = lax.fori_loop(0, n_tiles % 2, lambda _, a: fast_tile(n_tiles - 1, a), acc)
    l_fast = acc[HEAD_DIM:HEAD_DIM + 1]

    def exact(_):
        bias16 = bias.astype(BF16)
        qsel_ref[LANES:, :] = jnp.where(own, masked, bias16)
        s = _dot(own_keys, jnp.concatenate([qr, bias16], axis=0)) + tri
        m0 = jnp.max(s, axis=0, keepdims=True)
        p = jnp.exp2(s - m0)
        init = (m0, jnp.sum(p, axis=0, keepdims=True), _dot(vst_ref[:, pl.ds(w0, QBLK)], p.astype(BF16)))

        def tile(t, carry):
            m, l, a = carry
            k0 = pl.multiple_of(t * KTILE, KTILE)
            s = tile_scores(k0)
            m_new = jnp.maximum(m, jnp.max(s, axis=0, keepdims=True))
            alpha = jnp.exp2(m - m_new)
            p = jnp.exp2(s - m_new)
            l = alpha * l + jnp.sum(p, axis=0, keepdims=True)
            a = alpha * a + _dot(vst_ref[:, pl.ds(k0, KTILE)], p.astype(BF16))
            return m_new, l, a

        _, l, a = lax.fori_loop(0, n_tiles, tile, init)
        return a / l

    o_sel = lax.cond(jnp.min(l_fast) >= MIN_DENOM, lambda _: acc[:HEAD_DIM] / l_fast, exact, 0)

    def gate(branch):
        first = branch * N_KV * HPG + pl.program_id(1) * HPG
        return jnp.concatenate([gt_ref[pl.ds(first + hh, 1), :] for hh in range(HPG)], axis=1)

    o = gate(0) * o_cmp + gate(1) * o_sel + gate(2) * o_win
    o_ref[...] = jnp.concatenate([o[:, hh * QBLK:(hh + 1) * QBLK] for hh in range(HPG)], axis=0).astype(BF16)


def _attn_prompt(qn_t, qr_t, gt_t, k_norm, kc, vc_t, ks, blk_onehot, vs_t, kw_pad, vw_pad_t):
    nb, d, t = qn_t.shape
    ng = d // (HPG * HEAD_DIM)
    n_cmp = kc.shape[1]
    assert n_cmp <= LANES
    per_q = pl.BlockSpec((None, HPG * HEAD_DIM, QBLK), lambda b, g, i: (b, g, i))
    pair = lambda a: pl.BlockSpec((None, a.shape[1], LANES), lambda b, g, i: (b, 0, g // 2))
    group_rows = lambda a: pl.BlockSpec((None, HEAD_DIM, a.shape[2]), lambda b, g, i: (b, g, 0))
    return pl.pallas_call(
        _attn_prompt_body,
        grid=(nb, ng, t // QBLK),
        in_specs=[per_q, per_q, pl.BlockSpec((None, LANES, QBLK), lambda b, g, i: (b, 0, i)),
                  pl.BlockSpec((None, None, 8, LANES), lambda b, g, i: (b, g, 0, 0)),
                  pair(kc), group_rows(vc_t), pair(ks),
                  pl.BlockSpec(blk_onehot.shape, lambda b, g, i: (0, 0)), group_rows(vs_t),
                  pair(kw_pad), group_rows(vw_pad_t)],
        out_specs=per_q,
        out_shape=jax.ShapeDtypeStruct((nb, d, t), BF16),
        scratch_shapes=[pltpu.VMEM((2 * LANES, HPG * QBLK), BF16)],
        compiler_params=_cparams(3, 48),
        name="attn_prompt",
    )(qn_t, qr_t, gt_t, k_norm, kc, vc_t, ks, blk_onehot, vs_t, kw_pad, vw_pad_t)


def _attn_sample_body(pt_ref, qn_ref, qr_ref, gt_ref, kc_ref, vc_ref, kwin_ref, vwin_ref,
                      ksn_ref, vsn_ref, kwn_ref, vwn_ref, spread_ref, kpool_ref, vpool_ref, o_ref,
                      kbuf_ref, vbuf_ref, ksem, vsem, s_ref, *, n_pages, n_new, past_len):
    b = pl.program_id(0)
    nb = pl.num_programs(0)
    slot = b % 2

    @pl.when(b == 0)
    def _():
        _gather_start(pt_ref, kpool_ref, kbuf_ref, ksem, 0, 0, n_pages, 1, True)
        _gather_start(pt_ref, vpool_ref, vbuf_ref, vsem, 0, 0, n_pages, 1, True)

    @pl.when(b + 1 < nb)
    def _():
        _gather_start(pt_ref, kpool_ref, kbuf_ref, ksem, b + 1, 1 - slot, n_pages, 1, True)
        _gather_start(pt_ref, vpool_ref, vbuf_ref, vsem, b + 1, 1 - slot, n_pages, 1, True)

    qn = qn_ref[0]
    qr = qr_ref[0]
    n_col = qr.shape[0]
    n_past = n_pages * PAGE_SIZE
    n_blk = n_past // SEL_BLK
    n_gq = N_KV * n_new
    dot_nt = lambda a, bt: lax.dot_general(a, bt, (((1,), (1,)), ((), ())), preferred_element_type=F32)

    lane = lax.broadcasted_iota(jnp.int32, (1, LANES), 1)
    lane_pos = past_len + lane % n_new
    s = _dot(kc_ref[0], qn)
    cmp_end = lax.broadcasted_iota(jnp.int32, (n_blk, 1), 0) * CMP_BLK + (CMP_BLK - 1)
    s = jnp.where(cmp_end <= lane_pos, s, NEG_INF)
    e, l = _softmax_cols(s)
    p_cmp = e / jnp.maximum(l, 1e-30)
    o_cmp = _dot(p_cmp.T.astype(BF16), vc_ref[0])[:n_col]

    imp = p_cmp
    for hh in range(1, HPG):
        imp = imp + pltpu.roll(p_cmp, LANES - hh * n_gq, 1)
    sel = _select_bias(imp, lane_pos // SEL_BLK)
    sel01 = jnp.where((lane < n_gq) & (sel == 0.0), 1.0, 0.0)
    spread = sel01
    for hh in range(1, HPG):
        spread = spread + pltpu.roll(sel01, hh * n_gq, 1)
    sel_bias = jnp.where(spread.T[:n_col] > 0.5, 0.0, MASKED).astype(BF16)

    row = lax.broadcasted_iota(jnp.int32, (n_col, 1), 0)
    q_idx = row % n_new
    q_pos = past_len + q_idx
    n_pad = ksn_ref.shape[1]
    new_idx = lax.broadcasted_iota(jnp.int32, (1, n_pad), 1)
    new_ok = (new_idx <= q_idx) & (new_idx < n_new)

    _gather_wait(kpool_ref, kbuf_ref, ksem, slot, n_pages, 1, True)
    _gather_wait(vpool_ref, vbuf_ref, vsem, slot, n_pages, 1, True)
    pages_per_chunk = 8
    chunk = pages_per_chunk * PAGE_SIZE

    def page_rows(c, pp):
        return pl.ds(pl.multiple_of((c * pages_per_chunk + pp) * KV_W, KV_W), KV_W)

    def score_chunk(c, m):
        k0 = pl.multiple_of(c * chunk, chunk)
        s = jnp.concatenate([_dot(qr, kbuf_ref[slot, page_rows(c, pp), :].astype(BF16))
                             for pp in range(pages_per_chunk)], axis=1)
        s = s + _dot(sel_bias, spread_ref[:, pl.ds(k0, chunk)])
        s_ref[:, pl.ds(k0, chunk)] = s
        return jnp.maximum(m, jnp.max(s, axis=1, keepdims=True))

    m = lax.fori_loop(0, n_pages // pages_per_chunk, score_chunk, jnp.full((n_col, 1), NEG_INF, F32))
    s_new = jnp.where(new_ok, dot_nt(qr, ksn_ref[0].astype(BF16)), NEG_INF)
    m = jnp.maximum(m, jnp.max(s_new, axis=1, keepdims=True))
    m = jnp.where(m == NEG_INF, 0.0, m)

    def value_chunk(c, carry):
        l, acc = carry
        k0 = pl.multiple_of(c * chunk, chunk)
        p = jnp.exp2(s_ref[:, pl.ds(k0, chunk)] - m)
        l = l + jnp.sum(p, axis=1, keepdims=True)
        p = p.astype(BF16)
        for pp in range(pages_per_chunk):
            acc = acc + dot_nt(p[:, pp * PAGE_SIZE:(pp + 1) * PAGE_SIZE],
                               vbuf_ref[slot, page_rows(c, pp), :].astype(BF16))
        return l, acc

    p_new = jnp.exp2(s_new - m)
    l_sel, acc = lax.fori_loop(
        0, n_pages // pages_per_chunk, value_chunk,
        (jnp.sum(p_new, axis=1, keepdims=True), _dot(p_new.astype(BF16), vsn_ref[0].astype(BF16))))
    o_sel = acc / jnp.maximum(l_sel, 1e-30)

    n_buf = kwin_ref.shape[2]
    s_buf = _dot(qr, kwin_ref[0].astype(BF16))
    buf_pos = past_len - n_buf + lax.broadcasted_iota(jnp.int32, (1, n_buf), 1)
    diff = q_pos - buf_pos
    s_buf = jnp.where((diff >= 0) & (diff < WINDOW) & (buf_pos >= 0), s_buf, NEG_INF)
    s_nw = jnp.where(new_ok, dot_nt(qr, kwn_ref[0].astype(BF16)), NEG_INF)
    m = jnp.maximum(jnp.max(s_buf, axis=1, keepdims=True), jnp.max(s_nw, axis=1, keepdims=True))
    m = jnp.where(m == NEG_INF, 0.0, m)
    p_buf = jnp.exp2(s_buf - m)
    p_nw = jnp.exp2(s_nw - m)
    l_win = jnp.sum(p_buf, axis=1, keepdims=True) + jnp.sum(p_nw, axis=1, keepdims=True)
    o_win = (dot_nt(p_buf.astype(BF16), vwin_ref[0].astype(BF16))
             + _dot(p_nw.astype(BF16), vwn_ref[0].astype(BF16))) / jnp.maximum(l_win, 1e-30)

    gt = gt_ref[0]
    o = gt[:, 0:1] * o_cmp + gt[:, 1:2] * o_sel + gt[:, 2:3] * o_win
    row_g = (lax.broadcasted_iota(jnp.int32, (n_col, KV_W), 0) // n_new) % N_KV
    lane_g = lax.broadcasted_iota(jnp.int32, (n_col, KV_W), 1) // HEAD_DIM
    o = jnp.where(row_g == lane_g, o, 0.0)
    z = o[:, :LANES] + o[:, LANES:]
    o_ref[0] = z + pltpu.roll(z, HEAD_DIM, 1)


def _attn_sample(page_table, qn_bd, qr_bd_t, gt_col, kc, vc, kwin_t, vwin_t, ks_new, vs_new, kw_new, vw_new,
                 blk_spread, k_pool, v_pool, n_new, past_len):
    nb, n_pages = page_table.shape
    n_past = n_pages * PAGE_SIZE
    n_col = qr_bd_t.shape[1]
    per_b = lambda a: pl.BlockSpec((1,) + a.shape[1:], lambda b, pt: (b,) + (0,) * (a.ndim - 1))
    any_spec = pl.BlockSpec(memory_space=pl.ANY)
    ins = (qn_bd, qr_bd_t, gt_col, kc, vc, kwin_t, vwin_t, ks_new, vs_new, kw_new, vw_new)
    grid_spec = pltpu.PrefetchScalarGridSpec(
        num_scalar_prefetch=1,
        grid=(nb,),
        in_specs=[per_b(a) for a in ins]
        + [pl.BlockSpec(blk_spread.shape, lambda b, pt: (0, 0)), any_spec, any_spec],
        out_specs=pl.BlockSpec((1, n_col, LANES), lambda b, pt: (b, 0, 0)),
        scratch_shapes=[
            pltpu.VMEM((2, n_pages * KV_W, PAGE_SIZE), F32),
            pltpu.VMEM((2, n_pages * KV_W, PAGE_SIZE), F32),
            pltpu.SemaphoreType.DMA((2,)),
            pltpu.SemaphoreType.DMA((2,)),
            pltpu.VMEM((n_col, n_past), F32),
        ],
    )
    return pl.pallas_call(
        functools.partial(_attn_sample_body, n_pages=n_pages, n_new=n_new, past_len=past_len),
        grid_spec=grid_spec,
        out_shape=jax.ShapeDtypeStruct((nb, n_col, LANES), F32),
        compiler_params=_cparams(1, 56),
        name="attn_sample",
    )(page_table, *ins, blk_spread, k_pool, v_pool)


def _rope_tables(pos):
    half = HEAD_DIM // 2
    inv = ROPE_THETA ** (-jnp.arange(half, dtype=F32) / half)
    ang = pos.astype(F32)[:, None] * inv
    cos, sin = jnp.cos(ang), jnp.sin(ang)
    reps = LANES // HEAD_DIM
    return (jnp.tile(jnp.concatenate([cos, cos], axis=1), (1, reps)),
            jnp.tile(jnp.concatenate([-sin, sin], axis=1), (1, reps)))


def _group_sum_matrix():
    lane = jnp.arange(LANES)
    g = (lane[:, None] // HEAD_DIM == lane[None, :] // HEAD_DIM).astype(BF16)
    return jnp.concatenate([g, g], axis=0)


def _compress_weights(cmp_pos, cmp_w1, cmp_w2):
    n_pair = PAGE_SIZE // CMP_BLK
    eye = jnp.eye(n_pair, dtype=F32)
    hid = cmp_w1.shape[1]
    w1 = cmp_w1.reshape(CMP_BLK, HEAD_DIM, hid)
    wd = jnp.einsum("bk,rdh->dbrkh", eye, w1).reshape(HEAD_DIM, PAGE_SIZE, n_pair * hid).astype(BF16)
    bd2 = jnp.einsum("bk,hd->bhkd", eye, cmp_w2).reshape(n_pair * hid, n_pair * HEAD_DIM).astype(BF16)
    return jnp.tile(cmp_pos.T, (1, n_pair)), wd, bd2


def _to_rows_major(a_t):
    nb, _, rows = a_t.shape
    return a_t.reshape(nb, N_KV, HEAD_DIM, rows).transpose(0, 3, 1, 2)[None]


def _from_rows_major(a):
    n, rows = a.shape[:2]
    return a.transpose(0, 2, 3, 1).reshape(n, KV_W, rows)


def _layer_tail(x, p, norm_g, w_gu, w_down, w_ple_proj, w_ple_gate):
    x = _ffn(x, norm_g[2], w_gu[1], w_down[1])
    return _ple(x, norm_g[3], p, w_ple_gate, w_ple_proj)


def kernel(x_prompt, x_sample, state_conv, cache_k_cmp, cache_v_cmp, cache_k_sel, cache_v_sel,
           cache_k_win, cache_v_win, page_table, p_prompt, p_sample, norm_g, ffn_w_gu, ffn_w_down,
           ple_w_proj, ple_w_gate, conv_w_in, conv_w, conv_w_out, nsa_w_in, nsa_qk_g, nsa_cmp_pos,
           nsa_cmp_w1, nsa_cmp_w2, nsa_w_out):
    nb_p, t_p, d = x_prompt.shape
    nb_s, t_s, _ = x_sample.shape
    ple_dim = p_prompt.shape[-1]
    past_len = page_table.shape[1] * PAGE_SIZE
    assert t_p % KTILE == 0 and t_p % PAGE_SIZE == 0 and d % LANES == 0

    w_gu = ffn_w_gu.astype(BF16)
    w_down = ffn_w_down.astype(BF16)
    w_pp = ple_w_proj.astype(BF16)
    w_pg = ple_w_gate.astype(BF16)

    xp = x_prompt.reshape(nb_p * t_p, d)
    xs = x_sample.reshape(nb_s * t_s, d)
    pp = p_prompt.reshape(-1, nb_p * t_p, ple_dim)
    ps = p_sample.reshape(-1, nb_s * t_s, ple_dim)

    lw = (norm_g[0], w_gu[0], w_down[0], w_pp[0], w_pg[0])
    cw = (conv_w_in[0].astype(BF16), conv_w[0], conv_w_out[0].astype(BF16))
    xp = _ffn(xp, norm_g[0, 0], w_gu[0, 0], w_down[0, 0])
    xs = _ffn(xs, norm_g[0, 0], w_gu[0, 0], w_down[0, 0])
    xp3, conv_p = _conv_mixer(xp.reshape(nb_p, t_p, d), jnp.zeros((nb_p, CONV_W - 1, d), F32),
                              norm_g[0, 1], *cw, shift=1)
    xs_tb = xs.reshape(nb_s, t_s, d).transpose(1, 0, 2).reshape(1, t_s * nb_s, d)
    prev_tb = state_conv[0].transpose(1, 0, 2).reshape(1, (CONV_W - 1) * nb_s, d)
    xs3, conv_s = _conv_mixer(xs_tb, prev_tb, norm_g[0, 1], *cw, shift=nb_s)
    xs = xs3.reshape(t_s, nb_s, d).transpose(1, 0, 2).reshape(nb_s * t_s, d)
    conv_s = conv_s.reshape(CONV_W - 1, nb_s, d).transpose(1, 0, 2)
    xp = _layer_tail(xp3.reshape(nb_p * t_p, d), pp[0], *lw)
    xs = _layer_tail(xs, ps[0], *lw)

    lw = (norm_g[1], w_gu[1], w_down[1], w_pp[1], w_pg[1])
    xp = _ffn(xp, norm_g[1, 0], w_gu[1, 0], w_down[1, 0])
    xs = _ffn(xs, norm_g[1, 0], w_gu[1, 0], w_down[1, 0])

    n_in = nsa_w_in.shape[-1]
    n_in_pad = -(-n_in // LANES) * LANES
    n_gate = N_KV * HPG * 3
    gate_cols = nsa_w_in[0][:, n_in - n_gate:].reshape(d, N_KV * HPG, 3).transpose(0, 2, 1).reshape(d, n_gate)
    w_in = jnp.concatenate([nsa_w_in[0][:, :n_in - n_gate], gate_cols], axis=1)
    w_in = jnp.pad(w_in, ((0, 0), (0, n_in_pad - n_in))).astype(BF16)
    w_out = nsa_w_out[0].astype(BF16)
    qkg128 = jnp.tile(nsa_qk_g[0], (1, LANES // HEAD_DIM))
    gg = _group_sum_matrix()
    cos_p, sin_p = _rope_tables(jnp.arange(t_p))
    cos_s, sin_s = _rope_tables(past_len + jnp.arange(t_s))
    pos_k, wd_k, bd2_k = _compress_weights(nsa_cmp_pos[0, 0], nsa_cmp_w1[0, 0], nsa_cmp_w2[0, 0])
    pos_v, wd_v, bd2_v = _compress_weights(nsa_cmp_pos[0, 1], nsa_cmp_w1[0, 1], nsa_cmp_w2[0, 1])
    k_gain = qkg128[1:2]

    qn, qr, kc_t, vc_t, ks_t, vs_t, kw_t, vw_t, gt, ks_b, kw_b, vs_tb, vw_tb, k_norm2 = _nsa_proj(
        xp, norm_g[1, 1], w_in, qkg128, jnp.tile(cos_p, (nb_p, 1)), jnp.tile(sin_p, (nb_p, 1)), gg, n_batch=nb_p)
    pt_p = jnp.zeros((nb_p, t_p // PAGE_SIZE), jnp.int32)
    as_blocks = lambda c: c.reshape(c.shape[0], N_KV, -1, HEAD_DIM)
    by_block = lambda c: as_blocks(c).transpose(0, 2, 1, 3).reshape(c.shape[0], -1, KV_W)
    kc = _compress(pt_p, kc_t, pos_k, wd_k, bd2_k, k_gain, gg, True, False, nb_p)
    vc = _compress(pt_p, vc_t, pos_v, wd_v, bd2_v, k_gain, gg, False, False, nb_p)

    blk_onehot = (jnp.arange(t_p)[:, None] // SEL_BLK == jnp.arange(LANES)[None, :]).astype(BF16)
    k_norm = jnp.sqrt(k_norm2.reshape(nb_p, -1, N_KV, HEAD_DIM).max(axis=(1, 3)))
    k_norm = jnp.broadcast_to(k_norm[:, :, None, None], (nb_p, N_KV, 8, LANES))
    o_t = _attn_prompt(
        qn, qr, gt, k_norm,
        by_block(kc), as_blocks(vc).transpose(0, 1, 3, 2).reshape(nb_p, KV_W, -1),
        ks_b.reshape(nb_p, t_p, KV_W), blk_onehot, vs_tb,
        jnp.pad(kw_b.reshape(nb_p, t_p, KV_W), ((0, 0), (WINDOW, 0), (0, 0))),
        jnp.pad(vw_tb, ((0, 0), (0, 0), (WINDOW, 0))))
    xp = _proj_res(xp, o_t, w_out)
    keep = min(WINDOW, t_p)
    nsa_p = tuple(_to_rows_major(a) for a in (kc_t, vc_t, ks_t, vs_t, kw_t[:, :, -keep:], vw_t[:, :, -keep:]))

    qn, qr, kc_r, vc_r, ks, vs, kw, vw, gt = _nsa_proj(
        xs, norm_g[1, 1], w_in, qkg128, jnp.tile(cos_s, (nb_s, 1)), jnp.tile(sin_s, (nb_s, 1)), gg)
    pool_t = lambda c: _from_rows_major(c[0])
    n_seq = 2 if nb_s % 2 == 0 else 1
    kc = by_block(_compress(page_table, pool_t(cache_k_cmp), pos_k, wd_k, bd2_k, k_gain, gg, True, True, n_seq))
    vc = by_block(_compress(page_table, pool_t(cache_v_cmp), pos_v, wd_v, bd2_v, k_gain, gg, False, True, n_seq))

    n_used = HPG * N_KV * t_s

    def q_block_diag(a):
        a = a.reshape(nb_s, t_s, N_KV, HPG, HEAD_DIM).astype(F32)
        return jnp.einsum("bqghd,gk->bgdhkq", a, jnp.eye(N_KV, dtype=F32)).reshape(nb_s, KV_W, n_used)

    qn_bd = jnp.pad(q_block_diag(qn), ((0, 0), (0, 0), (0, LANES - n_used))).astype(BF16)
    qr_bd_t = q_block_diag(qr).transpose(0, 2, 1).astype(BF16)
    gt_col = gt[:, :n_gate].reshape(nb_s, t_s, 3, N_KV, HPG).transpose(0, 4, 3, 1, 2)
    gt_col = jnp.pad(gt_col.reshape(nb_s, n_used, 3), ((0, 0), (0, 0), (0, LANES - 3)))
    new_rows = lambda a: jnp.pad(a.reshape(nb_s, t_s, KV_W), ((0, 0), (0, 16 - t_s), (0, 0)))
    n_buf = cache_k_win.shape[2]
    kwin_t, vwin_t = _from_rows_major(cache_k_win[0]), _from_rows_major(cache_v_win[0])
    blk_spread = (jnp.arange(past_len // SEL_BLK)[:, None] == jnp.arange(past_len)[None, :] // SEL_BLK).astype(BF16)
    o_s = _attn_sample(
        page_table, qn_bd, qr_bd_t, gt_col, kc, vc, kwin_t, vwin_t,
        new_rows(ks), new_rows(vs), new_rows(kw), new_rows(vw), blk_spread,
        pool_t(cache_k_sel), pool_t(cache_v_sel), t_s, past_len)
    o = o_s[:, :, :HEAD_DIM].reshape(nb_s, HPG, N_KV, t_s, HEAD_DIM).transpose(0, 3, 2, 1, 4)
    xs = _proj_res(xs, o.reshape(nb_s * t_s, d).astype(BF16), w_out)
    st = lambda a: a.reshape(1, nb_s, t_s, N_KV, HEAD_DIM)
    new_t = lambda a: a.reshape(nb_s, t_s, KV_W).transpose(0, 2, 1)
    win = lambda buf_t, new: _to_rows_major(jnp.concatenate([buf_t, new_t(new)], axis=2)[:, :, -n_buf:])
    nsa_s = (st(kc_r), st(vc_r), st(ks), st(vs), win(kwin_t, kw), win(vwin_t, vw))

    xp = _layer_tail(xp, pp[1], *lw)
    xs = _layer_tail(xs, ps[1], *lw)

    return (xp.reshape(nb_p, t_p, d), xs.reshape(nb_s, t_s, d), conv_p[None], conv_s[None]) + nsa_p + nsa_s
```

```python
import functools

import jax
import jax.numpy as jnp
from jax import lax
from jax.experimental import pallas as pl
from jax.experimental.pallas import tpu as pltpu

F32 = jnp.float32
BF16 = jnp.bfloat16

HEAD_DIM = 64
N_KV = 4
HPG = 4
CMP_BLK = 64
SEL_BLK = 64
TOP_N = 16
N_FORCED = 3
WINDOW = 512
PAGE_SIZE = 128
CONV_W = 3
QBLK = 256
KTILE = 512
ROPE_THETA = 10000.0
RMS_EPS = 1e-6
NEG_INF = float("-inf")
MASKED = -1e30
LOG2E = 1.4426950408889634
BOUND_SLACK = 1.02
MIN_DENOM = 2.0 ** -100
ONES_ROWS = 16
KV_W = N_KV * HEAD_DIM
LANES = 128
MIB = 1024 * 1024


def _cparams(n_grid, vmem_mib):
    return pltpu.CompilerParams(
        dimension_semantics=("arbitrary",) * n_grid,
        vmem_limit_bytes=vmem_mib * MIB,
    )


def _rms(x, g):
    return x * lax.rsqrt(jnp.mean(x * x, axis=-1, keepdims=True) + RMS_EPS) * g


def _dot(a, b):
    return jnp.dot(a, b, preferred_element_type=F32)


def _ffn_body(x_ref, g_ref, wgu_ref, wd_ref, *rest, f, with_ple):
    x = x_ref[...]
    h = _rms(x, g_ref[...]).astype(BF16)
    gate_up = _dot(h, wgu_ref[...])
    act = (jax.nn.silu(gate_up[:, :f]) * gate_up[:, f:]).astype(BF16)
    y = x + 0.5 * _dot(act, wd_ref[...])
    if with_ple:
        g2_ref, p_ref, wg_ref, wp_ref, o_ref = rest
        gate = jax.nn.sigmoid(_dot(_rms(y, g2_ref[...]).astype(BF16), wg_ref[...]))
        y = y + gate * _dot(p_ref[...].astype(BF16), wp_ref[...])
    else:
        o_ref, = rest
    o_ref[...] = y


def _ffn(x, g, w_gu, w_down, ple=None):
    m, d = x.shape
    f = w_down.shape[0]
    tm = min(m, 512)
    row = lambda w: pl.BlockSpec((tm, w), lambda i: (i, 0))
    vec = pl.BlockSpec((1, d), lambda i: (0, 0))
    resident = lambda a: pl.BlockSpec(a.shape, lambda i: (0, 0), pipeline_mode=pl.Buffered(1))
    args = [x, g.reshape(1, d), w_gu, w_down]
    in_specs = [row(d), vec, resident(w_gu), resident(w_down)]
    if ple is not None:
        g2, p, w_gate, w_proj = ple
        args += [g2.reshape(1, d), p, w_gate, w_proj]
        in_specs += [vec, row(p.shape[1]), resident(w_gate), resident(w_proj)]
    return pl.pallas_call(
        functools.partial(_ffn_body, f=f, with_ple=ple is not None),
        grid=(m // tm,),
        in_specs=in_specs,
        out_specs=row(d),
        out_shape=jax.ShapeDtypeStruct((m, d), F32),
        compiler_params=_cparams(1, 56),
        name="ffn_ple" if ple is not None else "ffn",
    )(*args)


def _proj_res_body(x_ref, o_ref, w_ref, y_ref, *, rows_last):
    if rows_last:
        y = lax.dot_general(o_ref[...], w_ref[...], (((0,), (0,)), ((), ())), preferred_element_type=F32)
    else:
        y = _dot(o_ref[...], w_ref[...])
    y_ref[...] = x_ref[...] + y


def _proj_res(x, o, w):
    m, d = x.shape
    tm = min(m, 512)
    rows_last = o.ndim == 3
    if rows_last:
        k = o.shape[1]
        tiles = o.shape[2] // tm
        o_spec = pl.BlockSpec((None, k, tm), lambda i: (i // tiles, 0, i % tiles))
    else:
        k = o.shape[1]
        o_spec = pl.BlockSpec((tm, k), lambda i: (i, 0))
    return pl.pallas_call(
        functools.partial(_proj_res_body, rows_last=rows_last),
        grid=(m // tm,),
        in_specs=[
            pl.BlockSpec((tm, d), lambda i: (i, 0)),
            o_spec,
            pl.BlockSpec((k, d), lambda i: (0, 0)),
        ],
        out_specs=pl.BlockSpec((tm, d), lambda i: (i, 0)),
        out_shape=jax.ShapeDtypeStruct((m, d), F32),
        compiler_params=_cparams(1, 32),
        name="proj_res",
    )(x, o, w)


def _conv_body(x_ref, prev_ref, g_ref, win_ref, wc_ref, wout_ref, y_ref, st_ref, u_ref,
               *, shift, halo, tm, d):
    t = pl.program_id(1)
    lo = halo - 2 * shift

    @pl.when(t == 0)
    def _():
        u_ref[lo:halo, :] = prev_ref[0]

    x = x_ref[0]
    h = _rms(x, g_ref[...]).astype(BF16)
    proj = _dot(h, win_ref[...])
    b_gate = proj[:, :d]
    u = proj[:, d:2 * d] * proj[:, 2 * d:]
    u_ref[halo:halo + tm, :] = u
    wc = wc_ref[...]
    y = (wc[0:1, :] * u_ref[lo:lo + tm, :]
         + wc[1:2, :] * u_ref[lo + shift:lo + shift + tm, :]
         + wc[2:3, :] * u)
    y_ref[0] = x + _dot((b_gate * y).astype(BF16), wout_ref[...])
    tail = u_ref[lo + tm:halo + tm, :]
    st_ref[0] = tail
    u_ref[lo:halo, :] = tail


def _conv_mixer(x, prev, g, w_in, w_conv, w_out, shift):
    nb, t, d = x.shape
    tm = min(t, 512)
    halo = -(-2 * shift // 8) * 8
    return pl.pallas_call(
        functools.partial(_conv_body, shift=shift, halo=halo, tm=tm, d=d),
        grid=(nb, t // tm),
        in_specs=[
            pl.BlockSpec((1, tm, d), lambda b, i: (b, i, 0)),
            pl.BlockSpec((1, 2 * shift, d), lambda b, i: (b, 0, 0)),
            pl.BlockSpec((1, d), lambda b, i: (0, 0)),
            pl.BlockSpec((d, 3 * d), lambda b, i: (0, 0)),
            pl.BlockSpec((CONV_W, d), lambda b, i: (0, 0)),
            pl.BlockSpec((d, d), lambda b, i: (0, 0)),
        ],
        out_specs=[
            pl.BlockSpec((1, tm, d), lambda b, i: (b, i, 0)),
            pl.BlockSpec((1, 2 * shift, d), lambda b, i: (b, 0, 0)),
        ],
        out_shape=[
            jax.ShapeDtypeStruct((nb, t, d), F32),
            jax.ShapeDtypeStruct((nb, 2 * shift, d), F32),
        ],
        scratch_shapes=[pltpu.VMEM((halo + tm, d), F32)],
        compiler_params=_cparams(2, 48),
        name="conv_mixer",
    )(x, prev, g.reshape(1, d), w_in, w_conv, w_out)


def _head_norm(xc, gain, gg):
    sq = xc * xc
    hi = sq.astype(BF16)
    lo = (sq - hi.astype(F32)).astype(BF16)
    ss = _dot(jnp.concatenate([hi, lo], axis=1), gg)
    return xc * lax.rsqrt(ss * (1.0 / HEAD_DIM) + RMS_EPS) * gain


def _nsa_proj_body(x_ref, g_ref, w_ref, qkg_ref, cos_ref, sin_ref, gg_ref,
                   qn_ref, qr_ref, kc_ref, vc_ref, ks_ref, vs_ref, kw_ref, vw_ref, gt_ref, *extra_refs,
                   tm, d, rows_last):
    h = _rms(x_ref[...], g_ref[...]).astype(BF16)
    proj = _dot(h, w_ref[...])
    cos = cos_ref[...]
    sin = sin_ref[...]
    gg = gg_ref[...]
    lane = lax.broadcasted_iota(jnp.int32, (tm, LANES), 1)
    first_half = (lane % HEAD_DIM) < (HEAD_DIM // 2)
    scale = HEAD_DIM ** -0.5 * LOG2E

    def rope(xc):
        swapped = jnp.where(first_half, pltpu.roll(xc, LANES - HEAD_DIM // 2, 1),
                            pltpu.roll(xc, HEAD_DIM // 2, 1))
        return xc * cos + swapped * sin

    def slab(c):
        return proj[:, c * LANES:(c + 1) * LANES]

    n_q = d // LANES
    for c in range(n_q):
        qc = _head_norm(slab(c), qkg_ref[0:1, :], gg)
        if rows_last:
            qn_ref[c * LANES:(c + 1) * LANES, :] = (qc * scale).T.astype(BF16)
            qr_ref[c * LANES:(c + 1) * LANES, :] = (rope(qc) * scale).T.astype(BF16)
        else:
            qn_ref[:, c * LANES:(c + 1) * LANES] = (qc * scale).astype(BF16)
            qr_ref[:, c * LANES:(c + 1) * LANES] = (rope(qc) * scale).astype(BF16)
    def put(ref, c, val):
        if rows_last:
            ref[c * LANES:(c + 1) * LANES, :] = val
        else:
            ref[:, c * LANES:(c + 1) * LANES] = val

    for c in range(2):
        lanes = slice(c * LANES, (c + 1) * LANES)
        k_sel = rope(_head_norm(slab(n_q + 4 + c), qkg_ref[2:3, :], gg))
        k_win = rope(_head_norm(slab(n_q + 8 + c), qkg_ref[3:4, :], gg))
        vals = (slab(n_q + c), slab(n_q + 2 + c), k_sel, slab(n_q + 6 + c), k_win, slab(n_q + 10 + c))
        if rows_last:
            vals = tuple(v.T for v in vals)
            ksb_ref, kwb_ref, vsb_ref, vwb_ref, kn_ref = extra_refs
            k16 = k_sel.astype(BF16)
            ksb_ref[:, lanes] = k16
            k32 = k16.astype(F32)
            sq = k32 * k32
            hi = sq.astype(BF16)
            norm2 = _dot(jnp.concatenate([hi, (sq - hi.astype(F32)).astype(BF16)], axis=1), gg)
            kn_ref[:, lanes] = jnp.broadcast_to(jnp.max(norm2, axis=0, keepdims=True), (8, LANES))
            kwb_ref[:, lanes] = k_win.astype(BF16)
            vsb_ref[lanes, :] = vals[3].astype(BF16)
            vwb_ref[lanes, :] = vals[5].astype(BF16)
        for ref, v in zip((kc_ref, vc_ref, ks_ref, vs_ref, kw_ref, vw_ref), vals):
            put(ref, c, v)
    gates = jax.nn.sigmoid(slab(n_q + 12))
    gt_ref[...] = gates.T if rows_last else gates


def _nsa_proj(x, g, w_in_pad, qkg128, cos128, sin128, gg, n_batch=None):
    m, d = x.shape
    n = w_in_pad.shape[1]
    tm = min(m, 512)
    row = lambda w: pl.BlockSpec((tm, w), lambda i: (i, 0))
    full = lambda a: pl.BlockSpec(a.shape, lambda i: (0,) * a.ndim)
    rows_last = n_batch is not None
    if rows_last:
        t = m // n_batch
        tiles = t // tm
        t_spec = lambda w: pl.BlockSpec((None, w, tm), lambda i: (i // tiles, 0, i % tiles))
        t_shape = lambda w, dt: jax.ShapeDtypeStruct((n_batch, w, t), dt)
        kv_spec, q_spec, gt_spec = t_spec(KV_W), t_spec(d), t_spec(LANES)
        kv_shape = lambda dt: t_shape(KV_W, dt)
        q_shape, gt_shape = t_shape(d, BF16), t_shape(LANES, F32)
        extra_specs = [row(KV_W), row(KV_W), kv_spec, kv_spec, pl.BlockSpec((8, KV_W), lambda i: (i, 0))]
        extra_shapes = ([jax.ShapeDtypeStruct((m, KV_W), BF16)] * 2 + [kv_shape(BF16)] * 2
                        + [jax.ShapeDtypeStruct((m // tm * 8, KV_W), F32)])
    else:
        kv_spec, kv_shape = row(KV_W), lambda dt: jax.ShapeDtypeStruct((m, KV_W), dt)
        q_spec, gt_spec = row(d), row(LANES)
        q_shape, gt_shape = jax.ShapeDtypeStruct((m, d), BF16), jax.ShapeDtypeStruct((m, LANES), F32)
        extra_specs, extra_shapes = [], []
    return pl.pallas_call(
        functools.partial(_nsa_proj_body, tm=tm, d=d, rows_last=rows_last),
        grid=(m // tm,),
        in_specs=[row(d), pl.BlockSpec((1, d), lambda i: (0, 0)), pl.BlockSpec((d, n), lambda i: (0, 0)),
                  full(qkg128), row(LANES), row(LANES), full(gg)],
        out_specs=[q_spec, q_spec] + [kv_spec] * 6 + [gt_spec] + extra_specs,
        out_shape=[q_shape] * 2 + [kv_shape(F32)] * 6 + [gt_shape] + extra_shapes,
        compiler_params=_cparams(1, 48),
        name="nsa_proj",
    )(x, g.reshape(1, d), w_in_pad, qkg128, cos128, sin128, gg)


def _page_copy(src_ref, buf_ref, sem, slot, paged, index, p, dst_page):
    src = src_ref.at[index] if paged else src_ref.at[index, :, pl.ds(p * PAGE_SIZE, PAGE_SIZE)]
    dst = buf_ref.at[slot, :, dst_page, :] if buf_ref.ndim == 4 else buf_ref.at[slot, pl.ds(dst_page * KV_W, KV_W)]
    return pltpu.make_async_copy(src, dst, sem.at[slot])


def _gather_start(pt_ref, src_ref, buf_ref, sem, step, slot, n_pages, n_seq, paged):
    for s in range(n_seq):
        seq = step * n_seq + s
        for p in range(n_pages):
            index = pt_ref[seq, p] if paged else seq
            _page_copy(src_ref, buf_ref, sem, slot, paged, index, p, s * n_pages + p).start()


def _gather_wait(src_ref, buf_ref, sem, slot, n_pages, n_seq, paged):
    for s in range(n_seq):
        for p in range(n_pages):
            _page_copy(src_ref, buf_ref, sem, slot, paged, 0, p, s * n_pages + p).wait()


def _compress_body(pt_ref, src_ref, pos_ref, wd_ref, bd2_ref, gain_ref, gg_ref, o_ref,
                   buf_ref, sem, acc_ref, *, n_pages, n_seq, paged, normalize):
    step = pl.program_id(0)
    n_steps = pl.num_programs(0)
    slot = step % 2
    m = n_seq * n_pages
    unroll = 8

    @pl.when(step == 0)
    def _():
        _gather_start(pt_ref, src_ref, buf_ref, sem, 0, 0, n_pages, n_seq, paged)

    @pl.when(step + 1 < n_steps)
    def _():
        _gather_start(pt_ref, src_ref, buf_ref, sem, step + 1, 1 - slot, n_pages, n_seq, paged)

    _gather_wait(src_ref, buf_ref, sem, slot, n_pages, n_seq, paged)
    acc_ref[...] = jnp.zeros_like(acc_ref)

    def chunk(c, carry):
        accs = [acc_ref[g] for g in range(N_KV)]
        for dd in range(unroll):
            d = c * unroll + dd
            w = wd_ref[d]
            pos = pos_ref[pl.ds(d, 1), :]
            for g in range(N_KV):
                rows = buf_ref[slot, g * HEAD_DIM + d] + pos
                accs[g] = accs[g] + _dot(rows.astype(BF16), w)
        for g in range(N_KV):
            acc_ref[g] = accs[g]
        return carry

    lax.fori_loop(0, HEAD_DIM // unroll, chunk, 0)

    for g in range(N_KV):
        y = _dot(jax.nn.gelu(acc_ref[g]).astype(BF16), bd2_ref[...])
        if normalize:
            y = _head_norm(y, gain_ref[...], gg_ref[...])
        for s in range(n_seq):
            o_ref[s, g] = y[s * n_pages:(s + 1) * n_pages].astype(BF16)


def _compress(page_table, src, pos_t, wd, bd2, gain128, gg, normalize, paged, n_seq):
    nb, n_pages = page_table.shape
    hid2 = wd.shape[2]
    full = lambda a: pl.BlockSpec(a.shape, lambda i, pt: (0,) * a.ndim)
    grid_spec = pltpu.PrefetchScalarGridSpec(
        num_scalar_prefetch=1,
        grid=(nb // n_seq,),
        in_specs=[pl.BlockSpec(memory_space=pl.ANY), full(pos_t), full(wd), full(bd2), full(gain128), full(gg)],
        out_specs=pl.BlockSpec((n_seq, N_KV, n_pages, LANES), lambda i, pt: (i, 0, 0, 0)),
        scratch_shapes=[
            pltpu.VMEM((2, KV_W, n_seq * n_pages, PAGE_SIZE), F32),
            pltpu.SemaphoreType.DMA((2,)),
            pltpu.VMEM((N_KV, n_seq * n_pages, hid2), F32),
        ],
    )
    return pl.pallas_call(
        functools.partial(_compress_body, n_pages=n_pages, n_seq=n_seq, paged=paged, normalize=normalize),
        grid_spec=grid_spec,
        out_shape=jax.ShapeDtypeStruct((nb, N_KV, n_pages, LANES), BF16),
        compiler_params=_cparams(1, 48),
        name="compress",
    )(page_table, src, pos_t, wd, bd2, gain128, gg)


def _softmax_cols(s):
    m = jnp.max(s, axis=0, keepdims=True)
    m = jnp.where(m == NEG_INF, 0.0, m)
    e = jnp.exp2(s - m)
    return e, jnp.sum(e, axis=0, keepdims=True)


def _select_bias(imp, cur):
    jj = lax.broadcasted_iota(jnp.int32, imp.shape, 0)
    forced = (jj == 0) | (jj == cur) | (jj == cur - 1)
    cand = (jj <= cur) & jnp.logical_not(forced)
    v = jnp.where(cand, imp, -1.0)
    bias = jnp.where(forced, 0.0, NEG_INF)
    for _ in range(TOP_N - N_FORCED):
        m = jnp.max(v, axis=0, keepdims=True)
        idx = jnp.min(jnp.where(v == m, jj, imp.shape[0]), axis=0, keepdims=True)
        idx = jnp.where(m >= 0.0, idx, -1)
        pick = jj == idx
        bias = jnp.where(pick, 0.0, bias)
        v = jnp.where(pick, -1.0, v)
    return bias


def _attn_prompt_body(qn_ref, qr_ref, gt_ref, kn_ref, kc_ref, vct_ref, ks_ref, blk_ref, vst_ref, kw_ref, vwt_ref,
                      o_ref, qsel_ref):
    i = pl.program_id(2)
    q_start = i * QBLK
    n_col = HPG * QBLK
    q_pos = q_start + lax.broadcasted_iota(jnp.int32, (1, n_col), 1) % QBLK

    in_half = lax.broadcasted_iota(jnp.int32, (LANES, 1), 0) // HEAD_DIM == pl.program_id(1) % 2
    place = lambda q64: jnp.where(in_half, jnp.concatenate([q64, q64], axis=0), jnp.zeros((), BF16))
    heads_on_lanes = lambda ref: jnp.concatenate(
        [ref[hh * HEAD_DIM:(hh + 1) * HEAD_DIM, :] for hh in range(HPG)], axis=1)
    qn = place(heads_on_lanes(qn_ref))
    qr = place(heads_on_lanes(qr_ref))

    n_cmp = kc_ref.shape[0]
    s = _dot(kc_ref[...], qn)
    cmp_end = lax.broadcasted_iota(jnp.int32, (n_cmp, 1), 0) * CMP_BLK + (CMP_BLK - 1)
    s = jnp.where(cmp_end <= q_pos, s, NEG_INF)
    e, l = _softmax_cols(s)
    p_cmp = e / jnp.maximum(l, 1e-30)
    o_cmp = _dot(vct_ref[...], p_cmp.astype(BF16))

    imp = p_cmp[:, 0:QBLK]
    for hh in range(1, HPG):
        imp = imp + p_cmp[:, hh * QBLK:(hh + 1) * QBLK]
    bias = jnp.maximum(_select_bias(imp, q_pos[:, 0:QBLK] // SEL_BLK), MASKED)
    bias = jnp.concatenate([bias] * HPG, axis=1)
    if n_cmp < LANES:
        bias = jnp.concatenate([bias, jnp.zeros((LANES - n_cmp, n_col), F32)], axis=0)

    n_win = WINDOW + QBLK
    w0 = pl.multiple_of(q_start, QBLK)
    s = _dot(kw_ref[pl.ds(w0, n_win), :], qr)
    r = lax.broadcasted_iota(jnp.int32, (n_win, QBLK), 0)
    q = lax.broadcasted_iota(jnp.int32, (n_win, QBLK), 1)
    valid = (r > q) & (r <= q + WINDOW) & (r + q_start >= WINDOW)
    wbias = jnp.where(valid, 0.0, NEG_INF)
    s = s + jnp.concatenate([wbias] * HPG, axis=1)
    e, l = _softmax_cols(s)
    o_win = _dot(vwt_ref[:, pl.ds(w0, n_win)], e.astype(BF16)) / jnp.maximum(l, 1e-30)

    own = lax.broadcasted_iota(jnp.int32, (LANES, 1), 0) >= q_start // SEL_BLK
    masked = jnp.full((), MASKED, BF16)
    own_keys = jnp.concatenate([ks_ref[pl.ds(w0, QBLK), :], blk_ref[pl.ds(w0, QBLK), :]], axis=1)
    tri = jnp.where(lax.broadcasted_iota(jnp.int32, (QBLK, QBLK), 0)
                    <= lax.broadcasted_iota(jnp.int32, (QBLK, QBLK), 1), 0.0, NEG_INF)
    tri = jnp.concatenate([tri] * HPG, axis=1)
    n_tiles = (q_start + KTILE - 1) // KTILE

    def tile_scores(k0):
        keys = jnp.concatenate([ks_ref[pl.ds(k0, KTILE), :], blk_ref[pl.ds(k0, KTILE), :]], axis=1)
        return _dot(keys, qsel_ref[...])

    q32 = qr.astype(F32)
    bound = jnp.sqrt(jnp.sum(q32 * q32, axis=0, keepdims=True)) * (kn_ref[0:1, 0:1] * BOUND_SLACK)
    shifted = (bias - bound).astype(BF16)
    qsel_ref[0:LANES, :] = qr
    qsel_ref[LANES:, :] = jnp.where(own, masked, shifted)

    def values_and_ones(k0, n):
        return jnp.concatenate([vst_ref[:, pl.ds(k0, n)], jnp.ones((ONES_ROWS, n), BF16)], axis=0)

    def fast_tile(t, acc):
        k0 = pl.multiple_of(t * KTILE, KTILE)
        return acc + _dot(values_and_ones(k0, KTILE), jnp.exp2(tile_scores(k0)).astype(BF16))

    s = _dot(own_keys, jnp.concatenate([qr, shifted], axis=0)) + tri
    acc = _dot(values_and_ones(w0, QBLK), jnp.exp2(s).astype(BF16))
    acc = lax.fori_loop(0, n_tiles // 2, lambda u, a: fast_tile(2 * u + 1, fast_tile(2 * u, a)), acc)
    acc = lax.fori_loop(0, n_tiles % 2, lambda _, a: fast_tile(n_tiles - 1, a), acc)
    l_fast = acc[HEAD_DIM:HEAD_DIM + 1]

    def exact(_):
        bias16 = bias.astype(BF16)
        qsel_ref[LANES:, :] = jnp.where(own, masked, bias16)
        s = _dot(own_keys, jnp.concatenate([qr, bias16], axis=0)) + tri
        m0 = jnp.max(s, axis=0, keepdims=True)
        p = jnp.exp2(s - m0)
        init = (m0, jnp.sum(p, axis=0, keepdims=True), _dot(vst_ref[:, pl.ds(w0, QBLK)], p.astype(BF16)))

        def tile(t, carry):
            m, l, a = carry
            k0 = pl.multiple_of(t * KTILE, KTILE)
            s = tile_scores(k0)
            m_new = jnp.maximum(m, jnp.max(s, axis=0, keepdims=True))
            alpha = jnp.exp2(m - m_new)
            p = jnp.exp2(s - m_new)
            l = alpha * l + jnp.sum(p, axis=0, keepdims=True)
            a = alpha * a + _dot(vst_ref[:, pl.ds(k0, KTILE)], p.astype(BF16))
            return m_new, l, a

        _, l, a = lax.fori_loop(0, n_tiles, tile, init)
        return a / l

    o_sel = lax.cond(jnp.min(l_fast) >= MIN_DENOM, lambda _: acc[:HEAD_DIM] / l_fast, exact, 0)

    def gate(branch):
        first = branch * N_KV * HPG + pl.program_id(1) * HPG
        return jnp.concatenate([gt_ref[pl.ds(first + hh, 1), :] for hh in range(HPG)], axis=1)

    o = gate(0) * o_cmp + gate(1) * o_sel + gate(2) * o_win
    o_ref[...] = jnp.concatenate([o[:, hh * QBLK:(hh + 1) * QBLK] for hh in range(HPG)], axis=0).astype(BF16)


def _attn_prompt(qn_t, qr_t, gt_t, k_norm, kc, vc_t, ks, blk_onehot, vs_t, kw_pad, vw_pad_t):
    nb, d, t = qn_t.shape
    ng = d // (HPG * HEAD_DIM)
    n_cmp = kc.shape[1]
    assert n_cmp <= LANES
    per_q = pl.BlockSpec((None, HPG * HEAD_DIM, QBLK), lambda b, g, i: (b, g, i))
    pair = lambda a: pl.BlockSpec((None, a.shape[1], LANES), lambda b, g, i: (b, 0, g // 2))
    group_rows = lambda a: pl.BlockSpec((None, HEAD_DIM, a.shape[2]), lambda b, g, i: (b, g, 0))
    return pl.pallas_call(
        _attn_prompt_body,
        grid=(nb, ng, t // QBLK),
        in_specs=[per_q, per_q, pl.BlockSpec((None, LANES, QBLK), lambda b, g, i: (b, 0, i)),
                  pl.BlockSpec((None, None, 8, LANES), lambda b, g, i: (b, g, 0, 0)),
                  pair(kc), group_rows(vc_t), pair(ks),
                  pl.BlockSpec(blk_onehot.shape, lambda b, g, i: (0, 0)), group_rows(vs_t),
                  pair(kw_pad), group_rows(vw_pad_t)],
        out_specs=per_q,
        out_shape=jax.ShapeDtypeStruct((nb, d, t), BF16),
        scratch_shapes=[pltpu.VMEM((2 * LANES, HPG * QBLK), BF16)],
        compiler_params=_cparams(3, 48),
        name="attn_prompt",
    )(qn_t, qr_t, gt_t, k_norm, kc, vc_t, ks, blk_onehot, vs_t, kw_pad, vw_pad_t)


def _attn_sample_body(pt_ref, qn_ref, qr_ref, gt_ref, kc_ref, vc_ref, kwin_ref, vwin_ref,
                      ksn_ref, vsn_ref, kwn_ref, vwn_ref, spread_ref, kpool_ref, vpool_ref, o_ref,
                      kbuf_ref, vbuf_ref, ksem, vsem, s_ref, *, n_pages, n_new, past_len):
    b = pl.program_id(0)
    nb = pl.num_programs(0)
    slot = b % 2

    @pl.when(b == 0)
    def _():
        _gather_start(pt_ref, kpool_ref, kbuf_ref, ksem, 0, 0, n_pages, 1, True)
        _gather_start(pt_ref, vpool_ref, vbuf_ref, vsem, 0, 0, n_pages, 1, True)

    @pl.when(b + 1 < nb)
    def _():
        _gather_start(pt_ref, kpool_ref, kbuf_ref, ksem, b + 1, 1 - slot, n_pages, 1, True)
        _gather_start(pt_ref, vpool_ref, vbuf_ref, vsem, b + 1, 1 - slot, n_pages, 1, True)

    qn = qn_ref[0]
    qr = qr_ref[0]
    n_col = qr.shape[0]
    n_past = n_pages * PAGE_SIZE
    n_blk = n_past // SEL_BLK
    n_gq = N_KV * n_new
    dot_nt = lambda a, bt: lax.dot_general(a, bt, (((1,), (1,)), ((), ())), preferred_element_type=F32)

    lane = lax.broadcasted_iota(jnp.int32, (1, LANES), 1)
    lane_pos = past_len + lane % n_new
    s = _dot(kc_ref[0], qn)
    cmp_end = lax.broadcasted_iota(jnp.int32, (n_blk, 1), 0) * CMP_BLK + (CMP_BLK - 1)
    s = jnp.where(cmp_end <= lane_pos, s, NEG_INF)
    e, l = _softmax_cols(s)
    p_cmp = e / jnp.maximum(l, 1e-30)
    o_cmp = _dot(p_cmp.T.astype(BF16), vc_ref[0])[:n_col]

    imp = p_cmp
    for hh in range(1, HPG):
        imp = imp + pltpu.roll(p_cmp, LANES - hh * n_gq, 1)
    sel = _select_bias(imp, lane_pos // SEL_BLK)
    sel01 = jnp.where((lane < n_gq) & (sel == 0.0), 1.0, 0.0)
    spread = sel01
    for hh in range(1, HPG):
        spread = spread + pltpu.roll(sel01, hh * n_gq, 1)
    sel_bias = jnp.where(spread.T[:n_col] > 0.5, 0.0, MASKED).astype(BF16)

    row = lax.broadcasted_iota(jnp.int32, (n_col, 1), 0)
    q_idx = row % n_new
    q_pos = past_len + q_idx
    n_pad = ksn_ref.shape[1]
    new_idx = lax.broadcasted_iota(jnp.int32, (1, n_pad), 1)
    new_ok = (new_idx <= q_idx) & (new_idx < n_new)

    _gather_wait(kpool_ref, kbuf_ref, ksem, slot, n_pages, 1, True)
    _gather_wait(vpool_ref, vbuf_ref, vsem, slot, n_pages, 1, True)
    pages_per_chunk = 8
    chunk = pages_per_chunk * PAGE_SIZE

    def page_rows(c, pp):
        return pl.ds(pl.multiple_of((c * pages_per_chunk + pp) * KV_W, KV_W), KV_W)

    def score_chunk(c, m):
        k0 = pl.multiple_of(c * chunk, chunk)
        s = jnp.concatenate([_dot(qr, kbuf_ref[slot, page_rows(c, pp), :].astype(BF16))
                             for pp in range(pages_per_chunk)], axis=1)
        s = s + _dot(sel_bias, spread_ref[:, pl.ds(k0, chunk)])
        s_ref[:, pl.ds(k0, chunk)] = s
        return jnp.maximum(m, jnp.max(s, axis=1, keepdims=True))

    m = lax.fori_loop(0, n_pages // pages_per_chunk, score_chunk, jnp.full((n_col, 1), NEG_INF, F32))
    s_new = jnp.where(new_ok, dot_nt(qr, ksn_ref[0].astype(BF16)), NEG_INF)
    m = jnp.maximum(m, jnp.max(s_new, axis=1, keepdims=True))
    m = jnp.where(m == NEG_INF, 0.0, m)

    def value_chunk(c, carry):
        l, acc = carry
        k0 = pl.multiple_of(c * chunk, chunk)
        p = jnp.exp2(s_ref[:, pl.ds(k0, chunk)] - m)
        l = l + jnp.sum(p, axis=1, keepdims=True)
        p = p.astype(BF16)
        for pp in range(pages_per_chunk):
            acc = acc + dot_nt(p[:, pp * PAGE_SIZE:(pp + 1) * PAGE_SIZE],
                               vbuf_ref[slot, page_rows(c, pp), :].astype(BF16))
        return l, acc

    p_new = jnp.exp2(s_new - m)
    l_sel, acc = lax.fori_loop(
        0, n_pages // pages_per_chunk, value_chunk,
        (jnp.sum(p_new, axis=1, keepdims=True), _dot(p_new.astype(BF16), vsn_ref[0].astype(BF16))))
    o_sel = acc / jnp.maximum(l_sel, 1e-30)

    n_buf = kwin_ref.shape[2]
    s_buf = _dot(qr, kwin_ref[0].astype(BF16))
    buf_pos = past_len - n_buf + lax.broadcasted_iota(jnp.int32, (1, n_buf), 1)
    diff = q_pos - buf_pos
    s_buf = jnp.where((diff >= 0) & (diff < WINDOW) & (buf_pos >= 0), s_buf, NEG_INF)
    s_nw = jnp.where(new_ok, dot_nt(qr, kwn_ref[0].astype(BF16)), NEG_INF)
    m = jnp.maximum(jnp.max(s_buf, axis=1, keepdims=True), jnp.max(s_nw, axis=1, keepdims=True))
    m = jnp.where(m == NEG_INF, 0.0, m)
    p_buf = jnp.exp2(s_buf - m)
    p_nw = jnp.exp2(s_nw - m)
    l_win = jnp.sum(p_buf, axis=1, keepdims=True) + jnp.sum(p_nw, axis=1, keepdims=True)
    o_win = (dot_nt(p_buf.astype(BF16), vwin_ref[0].astype(BF16))
             + _dot(p_nw.astype(BF16), vwn_ref[0].astype(BF16))) / jnp.maximum(l_win, 1e-30)

    gt = gt_ref[0]
    o = gt[:, 0:1] * o_cmp + gt[:, 1:2] * o_sel + gt[:, 2:3] * o_win
    row_g = (lax.broadcasted_iota(jnp.int32, (n_col, KV_W), 0) // n_new) % N_KV
    lane_g = lax.broadcasted_iota(jnp.int32, (n_col, KV_W), 1) // HEAD_DIM
    o = jnp.where(row_g == lane_g, o, 0.0)
    z = o[:, :LANES] + o[:, LANES:]
    o_ref[0] = z + pltpu.roll(z, HEAD_DIM, 1)


def _attn_sample(page_table, qn_bd, qr_bd_t, gt_col, kc, vc, kwin_t, vwin_t, ks_new, vs_new, kw_new, vw_new,
                 blk_spread, k_pool, v_pool, n_new, past_len):
    nb, n_pages = page_table.shape
    n_past = n_pages * PAGE_SIZE
    n_col = qr_bd_t.shape[1]
    per_b = lambda a: pl.BlockSpec((1,) + a.shape[1:], lambda b, pt: (b,) + (0,) * (a.ndim - 1))
    any_spec = pl.BlockSpec(memory_space=pl.ANY)
    ins = (qn_bd, qr_bd_t, gt_col, kc, vc, kwin_t, vwin_t, ks_new, vs_new, kw_new, vw_new)
    grid_spec = pltpu.PrefetchScalarGridSpec(
        num_scalar_prefetch=1,
        grid=(nb,),
        in_specs=[per_b(a) for a in ins]
        + [pl.BlockSpec(blk_spread.shape, lambda b, pt: (0, 0)), any_spec, any_spec],
        out_specs=pl.BlockSpec((1, n_col, LANES), lambda b, pt: (b, 0, 0)),
        scratch_shapes=[
            pltpu.VMEM((2, n_pages * KV_W, PAGE_SIZE), F32),
            pltpu.VMEM((2, n_pages * KV_W, PAGE_SIZE), F32),
            pltpu.SemaphoreType.DMA((2,)),
            pltpu.SemaphoreType.DMA((2,)),
            pltpu.VMEM((n_col, n_past), F32),
        ],
    )
    return pl.pallas_call(
        functools.partial(_attn_sample_body, n_pages=n_pages, n_new=n_new, past_len=past_len),
        grid_spec=grid_spec,
        out_shape=jax.ShapeDtypeStruct((nb, n_col, LANES), F32),
        compiler_params=_cparams(1, 56),
        name="attn_sample",
    )(page_table, *ins, blk_spread, k_pool, v_pool)


def _rope_tables(pos):
    half = HEAD_DIM // 2
    inv = ROPE_THETA ** (-jnp.arange(half, dtype=F32) / half)
    ang = pos.astype(F32)[:, None] * inv
    cos, sin = jnp.cos(ang), jnp.sin(ang)
    reps = LANES // HEAD_DIM
    return (jnp.tile(jnp.concatenate([cos, cos], axis=1), (1, reps)),
            jnp.tile(jnp.concatenate([-sin, sin], axis=1), (1, reps)))


def _group_sum_matrix():
    lane = jnp.arange(LANES)
    g = (lane[:, None] // HEAD_DIM == lane[None, :] // HEAD_DIM).astype(BF16)
    return jnp.concatenate([g, g], axis=0)


def _compress_weights(cmp_pos, cmp_w1, cmp_w2):
    n_pair = PAGE_SIZE // CMP_BLK
    eye = jnp.eye(n_pair, dtype=F32)
    hid = cmp_w1.shape[1]
    w1 = cmp_w1.reshape(CMP_BLK, HEAD_DIM, hid)
    wd = jnp.einsum("bk,rdh->dbrkh", eye, w1).reshape(HEAD_DIM, PAGE_SIZE, n_pair * hid).astype(BF16)
    bd2 = jnp.einsum("bk,hd->bhkd", eye, cmp_w2).reshape(n_pair * hid, n_pair * HEAD_DIM).astype(BF16)
    return jnp.tile(cmp_pos.T, (1, n_pair)), wd, bd2


def _to_rows_major(a_t):
    nb, _, rows = a_t.shape
    return a_t.reshape(nb, N_KV, HEAD_DIM, rows).transpose(0, 3, 1, 2)[None]


def _from_rows_major(a):
    n, rows = a.shape[:2]
    return a.transpose(0, 2, 3, 1).reshape(n, KV_W, rows)


def _layer_tail(x, p, norm_g, w_gu, w_down, w_ple_proj, w_ple_gate):
    return _ffn(x, norm_g[2], w_gu[1], w_down[1], ple=(norm_g[3], p, w_ple_gate, w_ple_proj))


def kernel(x_prompt, x_sample, state_conv, cache_k_cmp, cache_v_cmp, cache_k_sel, cache_v_sel,
           cache_k_win, cache_v_win, page_table, p_prompt, p_sample, norm_g, ffn_w_gu, ffn_w_down,
           ple_w_proj, ple_w_gate, conv_w_in, conv_w, conv_w_out, nsa_w_in, nsa_qk_g, nsa_cmp_pos,
           nsa_cmp_w1, nsa_cmp_w2, nsa_w_out):
    nb_p, t_p, d = x_prompt.shape
    nb_s, t_s, _ = x_sample.shape
    ple_dim = p_prompt.shape[-1]
    past_len = page_table.shape[1] * PAGE_SIZE
    assert t_p % KTILE == 0 and t_p % PAGE_SIZE == 0 and d % LANES == 0

    w_gu = ffn_w_gu.astype(BF16)
    w_down = ffn_w_down.astype(BF16)
    w_pp = ple_w_proj.astype(BF16)
    w_pg = ple_w_gate.astype(BF16)

    xp = x_prompt.reshape(nb_p * t_p, d)
    xs = x_sample.reshape(nb_s * t_s, d)
    pp = p_prompt.reshape(-1, nb_p * t_p, ple_dim)
    ps = p_sample.reshape(-1, nb_s * t_s, ple_dim)

    lw = (norm_g[0], w_gu[0], w_down[0], w_pp[0], w_pg[0])
    cw = (conv_w_in[0].astype(BF16), conv_w[0], conv_w_out[0].astype(BF16))
    xp = _ffn(xp, norm_g[0, 0], w_gu[0, 0], w_down[0, 0])
    xs = _ffn(xs, norm_g[0, 0], w_gu[0, 0], w_down[0, 0])
    xp3, conv_p = _conv_mixer(xp.reshape(nb_p, t_p, d), jnp.zeros((nb_p, CONV_W - 1, d), F32),
                              norm_g[0, 1], *cw, shift=1)
    xs_tb = xs.reshape(nb_s, t_s, d).transpose(1, 0, 2).reshape(1, t_s * nb_s, d)
    prev_tb = state_conv[0].transpose(1, 0, 2).reshape(1, (CONV_W - 1) * nb_s, d)
    xs3, conv_s = _conv_mixer(xs_tb, prev_tb, norm_g[0, 1], *cw, shift=nb_s)
    xs = xs3.reshape(t_s, nb_s, d).transpose(1, 0, 2).reshape(nb_s * t_s, d)
    conv_s = conv_s.reshape(CONV_W - 1, nb_s, d).transpose(1, 0, 2)
    xp = _layer_tail(xp3.reshape(nb_p * t_p, d), pp[0], *lw)
    xs = _layer_tail(xs, ps[0], *lw)

    lw = (norm_g[1], w_gu[1], w_down[1], w_pp[1], w_pg[1])
    xp = _ffn(xp, norm_g[1, 0], w_gu[1, 0], w_down[1, 0])
    xs = _ffn(xs, norm_g[1, 0], w_gu[1, 0], w_down[1, 0])

    n_in = nsa_w_in.shape[-1]
    n_in_pad = -(-n_in // LANES) * LANES
    n_gate = N_KV * HPG * 3
    gate_cols = nsa_w_in[0][:, n_in - n_gate:].reshape(d, N_KV * HPG, 3).transpose(0, 2, 1).reshape(d, n_gate)
    w_in = jnp.concatenate([nsa_w_in[0][:, :n_in - n_gate], gate_cols], axis=1)
    w_in = jnp.pad(w_in, ((0, 0), (0, n_in_pad - n_in))).astype(BF16)
    w_out = nsa_w_out[0].astype(BF16)
    qkg128 = jnp.tile(nsa_qk_g[0], (1, LANES // HEAD_DIM))
    gg = _group_sum_matrix()
    cos_p, sin_p = _rope_tables(jnp.arange(t_p))
    cos_s, sin_s = _rope_tables(past_len + jnp.arange(t_s))
    pos_k, wd_k, bd2_k = _compress_weights(nsa_cmp_pos[0, 0], nsa_cmp_w1[0, 0], nsa_cmp_w2[0, 0])
    pos_v, wd_v, bd2_v = _compress_weights(nsa_cmp_pos[0, 1], nsa_cmp_w1[0, 1], nsa_cmp_w2[0, 1])
    k_gain = qkg128[1:2]

    qn, qr, kc_t, vc_t, ks_t, vs_t, kw_t, vw_t, gt, ks_b, kw_b, vs_tb, vw_tb, k_norm2 = _nsa_proj(
        xp, norm_g[1, 1], w_in, qkg128, jnp.tile(cos_p, (nb_p, 1)), jnp.tile(sin_p, (nb_p, 1)), gg, n_batch=nb_p)
    pt_p = jnp.zeros((nb_p, t_p // PAGE_SIZE), jnp.int32)
    as_blocks = lambda c: c.reshape(c.shape[0], N_KV, -1, HEAD_DIM)
    by_block = lambda c: as_blocks(c).transpose(0, 2, 1, 3).reshape(c.shape[0], -1, KV_W)
    kc = _compress(pt_p, kc_t, pos_k, wd_k, bd2_k, k_gain, gg, True, False, nb_p)
    vc = _compress(pt_p, vc_t, pos_v, wd_v, bd2_v, k_gain, gg, False, False, nb_p)

    blk_onehot = (jnp.arange(t_p)[:, None] // SEL_BLK == jnp.arange(LANES)[None, :]).astype(BF16)
    k_norm = jnp.sqrt(k_norm2.reshape(nb_p, -1, N_KV, HEAD_DIM).max(axis=(1, 3)))
    k_norm = jnp.broadcast_to(k_norm[:, :, None, None], (nb_p, N_KV, 8, LANES))
    o_t = _attn_prompt(
        qn, qr, gt, k_norm,
        by_block(kc), as_blocks(vc).transpose(0, 1, 3, 2).reshape(nb_p, KV_W, -1),
        ks_b.reshape(nb_p, t_p, KV_W), blk_onehot, vs_tb,
        jnp.pad(kw_b.reshape(nb_p, t_p, KV_W), ((0, 0), (WINDOW, 0), (0, 0))),
        jnp.pad(vw_tb, ((0, 0), (0, 0), (WINDOW, 0))))
    xp = _proj_res(xp, o_t, w_out)
    keep = min(WINDOW, t_p)
    nsa_p = tuple(_to_rows_major(a) for a in (kc_t, vc_t, ks_t, vs_t, kw_t[:, :, -keep:], vw_t[:, :, -keep:]))

    qn, qr, kc_r, vc_r, ks, vs, kw, vw, gt = _nsa_proj(
        xs, norm_g[1, 1], w_in, qkg128, jnp.tile(cos_s, (nb_s, 1)), jnp.tile(sin_s, (nb_s, 1)), gg)
    pool_t = lambda c: _from_rows_major(c[0])
    n_seq = 2 if nb_s % 2 == 0 else 1
    kc = by_block(_compress(page_table, pool_t(cache_k_cmp), pos_k, wd_k, bd2_k, k_gain, gg, True, True, n_seq))
    vc = by_block(_compress(page_table, pool_t(cache_v_cmp), pos_v, wd_v, bd2_v, k_gain, gg, False, True, n_seq))

    n_used = HPG * N_KV * t_s

    def q_block_diag(a):
        a = a.reshape(nb_s, t_s, N_KV, HPG, HEAD_DIM).astype(F32)
        return jnp.einsum("bqghd,gk->bgdhkq", a, jnp.eye(N_KV, dtype=F32)).reshape(nb_s, KV_W, n_used)

    qn_bd = jnp.pad(q_block_diag(qn), ((0, 0), (0, 0), (0, LANES - n_used))).astype(BF16)
    qr_bd_t = q_block_diag(qr).transpose(0, 2, 1).astype(BF16)
    gt_col = gt[:, :n_gate].reshape(nb_s, t_s, 3, N_KV, HPG).transpose(0, 4, 3, 1, 2)
    gt_col = jnp.pad(gt_col.reshape(nb_s, n_used, 3), ((0, 0), (0, 0), (0, LANES - 3)))
    new_rows = lambda a: jnp.pad(a.reshape(nb_s, t_s, KV_W), ((0, 0), (0, 16 - t_s), (0, 0)))
    n_buf = cache_k_win.shape[2]
    kwin_t, vwin_t = _from_rows_major(cache_k_win[0]), _from_rows_major(cache_v_win[0])
    blk_spread = (jnp.arange(past_len // SEL_BLK)[:, None] == jnp.arange(past_len)[None, :] // SEL_BLK).astype(BF16)
    o_s = _attn_sample(
        page_table, qn_bd, qr_bd_t, gt_col, kc, vc, kwin_t, vwin_t,
        new_rows(ks), new_rows(vs), new_rows(kw), new_rows(vw), blk_spread,
        pool_t(cache_k_sel), pool_t(cache_v_sel), t_s, past_len)
    o = o_s[:, :, :HEAD_DIM].reshape(nb_s, HPG, N_KV, t_s, HEAD_DIM).transpose(0, 3, 2, 1, 4)
    xs = _proj_res(xs, o.reshape(nb_s * t_s, d).astype(BF16), w_out)
    st = lambda a: a.reshape(1, nb_s, t_s, N_KV, HEAD_DIM)
    new_t = lambda a: a.reshape(nb_s, t_s, KV_W).transpose(0, 2, 1)
    win = lambda buf_t, new: _to_rows_major(jnp.concatenate([buf_t, new_t(new)], axis=2)[:, :, -n_buf:])
    nsa_s = (st(kc_r), st(vc_r), st(ks), st(vs), win(kwin_t, kw), win(vwin_t, vw))

    xp = _layer_tail(xp, pp[1], *lw)
    xs = _layer_tail(xs, ps[1], *lw)

    return (xp.reshape(nb_p, t_p, d), xs.reshape(nb_s, t_s, d), conv_p[None], conv_s[None]) + nsa_p + nsa_s
```

```python
import functools

import jax
import jax.numpy as jnp
from jax import lax
from jax.experimental import pallas as pl
from jax.experimental.pallas import tpu as pltpu

F32 = jnp.float32
BF16 = jnp.bfloat16

HEAD_DIM = 64
N_KV = 4
HPG = 4
CMP_BLK = 64
SEL_BLK = 64
TOP_N = 16
N_FORCED = 3
WINDOW = 512
PAGE_SIZE = 128
CONV_W = 3
QBLK = 256
KTILE = 512
ROPE_THETA = 10000.0
RMS_EPS = 1e-6
NEG_INF = float("-inf")
MASKED = -1e30
LOG2E = 1.4426950408889634
BOUND_SLACK = 1.02
MIN_DENOM = 2.0 ** -100
ONES_ROWS = 16
KV_W = N_KV * HEAD_DIM
ROW_TILE = 512
LANES = 128
MIB = 1024 * 1024


def _cparams(n_grid, vmem_mib):
    return pltpu.CompilerParams(
        dimension_semantics=("arbitrary",) * n_grid,
        vmem_limit_bytes=vmem_mib * MIB,
    )


def _rms(x, g):
    return x * lax.rsqrt(jnp.mean(x * x, axis=-1, keepdims=True) + RMS_EPS) * g


def _dot(a, b):
    return jnp.dot(a, b, preferred_element_type=F32)


def _ffn_body(x_ref, g_ref, wgu_ref, wd_ref, *rest, f, mixer, with_ple):
    x = x_ref[...]
    if mixer is not None:
        o_ref, wo_ref, *rest = rest
        if mixer == "rows_last":
            x = x + lax.dot_general(o_ref[...], wo_ref[...], (((0,), (0,)), ((), ())),
                                    preferred_element_type=F32)
        else:
            x = x + _dot(o_ref[...], wo_ref[...])
    h = _rms(x, g_ref[...]).astype(BF16)
    gate_up = _dot(h, wgu_ref[...])
    act = (jax.nn.silu(gate_up[:, :f]) * gate_up[:, f:]).astype(BF16)
    y = x + 0.5 * _dot(act, wd_ref[...])
    if with_ple:
        g2_ref, p_ref, wg_ref, wp_ref, out_ref = rest
        gate = jax.nn.sigmoid(_dot(_rms(y, g2_ref[...]).astype(BF16), wg_ref[...]))
        y = y + gate * _dot(p_ref[...].astype(BF16), wp_ref[...])
    else:
        out_ref, = rest
    out_ref[...] = y


def _ffn(x, g, w_gu, w_down, mixer_out=None, ple=None):
    m, d = x.shape
    f = w_down.shape[0]
    tm = min(m, ROW_TILE)
    row = lambda w: pl.BlockSpec((tm, w), lambda i: (i, 0))
    vec = pl.BlockSpec((1, d), lambda i: (0, 0))
    resident = lambda a: pl.BlockSpec(a.shape, lambda i: (0, 0), pipeline_mode=pl.Buffered(1))
    args = [x, g.reshape(1, d), w_gu, w_down]
    in_specs = [row(d), vec, resident(w_gu), resident(w_down)]
    mixer = None
    if mixer_out is not None:
        o, w_o = mixer_out
        if o.ndim == 3:
            mixer = "rows_last"
            tiles = o.shape[2] // tm
            o_spec = pl.BlockSpec((None, o.shape[1], tm), lambda i: (i // tiles, 0, i % tiles))
        else:
            mixer = "rows_first"
            o_spec = row(o.shape[1])
        args += [o, w_o]
        in_specs += [o_spec, resident(w_o)]
    if ple is not None:
        g2, p, w_gate, w_proj = ple
        args += [g2.reshape(1, d), p, w_gate, w_proj]
        in_specs += [vec, row(p.shape[1]), resident(w_gate), resident(w_proj)]
    return pl.pallas_call(
        functools.partial(_ffn_body, f=f, mixer=mixer, with_ple=ple is not None),
        grid=(m // tm,),
        in_specs=in_specs,
        out_specs=row(d),
        out_shape=jax.ShapeDtypeStruct((m, d), F32),
        compiler_params=_cparams(1, 56),
        name="ffn",
    )(*args)


def _conv_body(x_ref, prev_ref, g_ref, win_ref, wc_ref, wout_ref, y_ref, st_ref, u_ref,
               *, shift, halo, tm, d):
    t = pl.program_id(1)
    lo = halo - 2 * shift

    @pl.when(t == 0)
    def _():
        u_ref[lo:halo, :] = prev_ref[0]

    x = x_ref[0]
    h = _rms(x, g_ref[...]).astype(BF16)
    proj = _dot(h, win_ref[...])
    b_gate = proj[:, :d]
    u = proj[:, d:2 * d] * proj[:, 2 * d:]
    u_ref[halo:halo + tm, :] = u
    wc = wc_ref[...]
    y = (wc[0:1, :] * u_ref[lo:lo + tm, :]
         + wc[1:2, :] * u_ref[lo + shift:lo + shift + tm, :]
         + wc[2:3, :] * u)
    y_ref[0] = x + _dot((b_gate * y).astype(BF16), wout_ref[...])
    tail = u_ref[lo + tm:halo + tm, :]
    st_ref[0] = tail
    u_ref[lo:halo, :] = tail


def _conv_mixer(x, prev, g, w_in, w_conv, w_out, shift):
    nb, t, d = x.shape
    tm = min(t, ROW_TILE)
    halo = -(-2 * shift // 8) * 8
    return pl.pallas_call(
        functools.partial(_conv_body, shift=shift, halo=halo, tm=tm, d=d),
        grid=(nb, t // tm),
        in_specs=[
            pl.BlockSpec((1, tm, d), lambda b, i: (b, i, 0)),
            pl.BlockSpec((1, 2 * shift, d), lambda b, i: (b, 0, 0)),
            pl.BlockSpec((1, d), lambda b, i: (0, 0)),
            pl.BlockSpec((d, 3 * d), lambda b, i: (0, 0)),
            pl.BlockSpec((CONV_W, d), lambda b, i: (0, 0)),
            pl.BlockSpec((d, d), lambda b, i: (0, 0)),
        ],
        out_specs=[
            pl.BlockSpec((1, tm, d), lambda b, i: (b, i, 0)),
            pl.BlockSpec((1, 2 * shift, d), lambda b, i: (b, 0, 0)),
        ],
        out_shape=[
            jax.ShapeDtypeStruct((nb, t, d), F32),
            jax.ShapeDtypeStruct((nb, 2 * shift, d), F32),
        ],
        scratch_shapes=[pltpu.VMEM((halo + tm, d), F32)],
        compiler_params=_cparams(2, 48),
        name="conv_mixer",
    )(x, prev, g.reshape(1, d), w_in, w_conv, w_out)


def _head_norm(xc, gain, gg):
    sq = xc * xc
    hi = sq.astype(BF16)
    lo = (sq - hi.astype(F32)).astype(BF16)
    ss = _dot(jnp.concatenate([hi, lo], axis=1), gg)
    return xc * lax.rsqrt(ss * (1.0 / HEAD_DIM) + RMS_EPS) * gain


def _nsa_proj_body(x_ref, g_ref, w_ref, qkg_ref, cos_ref, sin_ref, gg_ref,
                   qn_ref, qr_ref, kc_ref, vc_ref, ks_ref, vs_ref, kw_ref, vw_ref, gt_ref, *extra_refs,
                   tm, d, rows_last):
    h = _rms(x_ref[...], g_ref[...]).astype(BF16)
    proj = _dot(h, w_ref[...])
    cos = cos_ref[...]
    sin = sin_ref[...]
    gg = gg_ref[...]
    lane = lax.broadcasted_iota(jnp.int32, (tm, LANES), 1)
    first_half = (lane % HEAD_DIM) < (HEAD_DIM // 2)
    scale = HEAD_DIM ** -0.5 * LOG2E

    def rope(xc):
        swapped = jnp.where(first_half, pltpu.roll(xc, LANES - HEAD_DIM // 2, 1),
                            pltpu.roll(xc, HEAD_DIM // 2, 1))
        return xc * cos + swapped * sin

    def slab(c):
        return proj[:, c * LANES:(c + 1) * LANES]

    n_q = d // LANES
    for c in range(n_q):
        qc = _head_norm(slab(c), qkg_ref[0:1, :], gg)
        if rows_last:
            qn_ref[c * LANES:(c + 1) * LANES, :] = (qc * scale).T.astype(BF16)
            qr_ref[c * LANES:(c + 1) * LANES, :] = (rope(qc) * scale).T.astype(BF16)
        else:
            qn_ref[:, c * LANES:(c + 1) * LANES] = (qc * scale).astype(BF16)
            qr_ref[:, c * LANES:(c + 1) * LANES] = (rope(qc) * scale).astype(BF16)
    def put(ref, c, val):
        if rows_last:
            ref[c * LANES:(c + 1) * LANES, :] = val
        else:
            ref[:, c * LANES:(c + 1) * LANES] = val

    for c in range(2):
        lanes = slice(c * LANES, (c + 1) * LANES)
        k_sel = rope(_head_norm(slab(n_q + 4 + c), qkg_ref[2:3, :], gg))
        k_win = rope(_head_norm(slab(n_q + 8 + c), qkg_ref[3:4, :], gg))
        vals = (slab(n_q + c), slab(n_q + 2 + c), k_sel, slab(n_q + 6 + c), k_win, slab(n_q + 10 + c))
        if rows_last:
            vals = tuple(v.T for v in vals)
            ksb_ref, kwb_ref, vsb_ref, vwb_ref, kn_ref = extra_refs
            k16 = k_sel.astype(BF16)
            ksb_ref[:, lanes] = k16
            k32 = k16.astype(F32)
            sq = k32 * k32
            hi = sq.astype(BF16)
            norm2 = _dot(jnp.concatenate([hi, (sq - hi.astype(F32)).astype(BF16)], axis=1), gg)
            kn_ref[:, lanes] = jnp.broadcast_to(jnp.max(norm2, axis=0, keepdims=True), (8, LANES))
            kwb_ref[:, lanes] = k_win.astype(BF16)
            vsb_ref[lanes, :] = vals[3].astype(BF16)
            vwb_ref[lanes, :] = vals[5].astype(BF16)
        for ref, v in zip((kc_ref, vc_ref, ks_ref, vs_ref, kw_ref, vw_ref), vals):
            put(ref, c, v)
    gates = jax.nn.sigmoid(slab(n_q + 12))
    gt_ref[...] = gates.T if rows_last else gates


def _nsa_proj(x, g, w_in_pad, qkg128, cos128, sin128, gg, n_batch=None):
    m, d = x.shape
    n = w_in_pad.shape[1]
    tm = min(m, ROW_TILE)
    row = lambda w: pl.BlockSpec((tm, w), lambda i: (i, 0))
    full = lambda a: pl.BlockSpec(a.shape, lambda i: (0,) * a.ndim)
    rows_last = n_batch is not None
    if rows_last:
        t = m // n_batch
        tiles = t // tm
        t_spec = lambda w: pl.BlockSpec((None, w, tm), lambda i: (i // tiles, 0, i % tiles))
        t_shape = lambda w, dt: jax.ShapeDtypeStruct((n_batch, w, t), dt)
        kv_spec, q_spec, gt_spec = t_spec(KV_W), t_spec(d), t_spec(LANES)
        kv_shape = lambda dt: t_shape(KV_W, dt)
        q_shape, gt_shape = t_shape(d, BF16), t_shape(LANES, F32)
        extra_specs = [row(KV_W), row(KV_W), kv_spec, kv_spec, pl.BlockSpec((8, KV_W), lambda i: (i, 0))]
        extra_shapes = ([jax.ShapeDtypeStruct((m, KV_W), BF16)] * 2 + [kv_shape(BF16)] * 2
                        + [jax.ShapeDtypeStruct((m // tm * 8, KV_W), F32)])
    else:
        kv_spec, kv_shape = row(KV_W), lambda dt: jax.ShapeDtypeStruct((m, KV_W), dt)
        q_spec, gt_spec = row(d), row(LANES)
        q_shape, gt_shape = jax.ShapeDtypeStruct((m, d), BF16), jax.ShapeDtypeStruct((m, LANES), F32)
        extra_specs, extra_shapes = [], []
    return pl.pallas_call(
        functools.partial(_nsa_proj_body, tm=tm, d=d, rows_last=rows_last),
        grid=(m // tm,),
        in_specs=[row(d), pl.BlockSpec((1, d), lambda i: (0, 0)), pl.BlockSpec((d, n), lambda i: (0, 0)),
                  full(qkg128), row(LANES), row(LANES), full(gg)],
        out_specs=[q_spec, q_spec] + [kv_spec] * 6 + [gt_spec] + extra_specs,
        out_shape=[q_shape] * 2 + [kv_shape(F32)] * 6 + [gt_shape] + extra_shapes,
        compiler_params=_cparams(1, 48),
        name="nsa_proj",
    )(x, g.reshape(1, d), w_in_pad, qkg128, cos128, sin128, gg)


def _page_copy(src_ref, buf_ref, sem, slot, paged, index, p, dst_page):
    src = src_ref.at[index] if paged else src_ref.at[index, :, pl.ds(p * PAGE_SIZE, PAGE_SIZE)]
    dst = buf_ref.at[slot, :, dst_page, :] if buf_ref.ndim == 4 else buf_ref.at[slot, pl.ds(dst_page * KV_W, KV_W)]
    return pltpu.make_async_copy(src, dst, sem.at[slot])


def _gather_start(pt_ref, src_ref, buf_ref, sem, step, slot, n_pages, n_seq, paged):
    for s in range(n_seq):
        seq = step * n_seq + s
        for p in range(n_pages):
            index = pt_ref[seq, p] if paged else seq
            _page_copy(src_ref, buf_ref, sem, slot, paged, index, p, s * n_pages + p).start()


def _gather_wait(src_ref, buf_ref, sem, slot, n_pages, n_seq, paged):
    for s in range(n_seq):
        for p in range(n_pages):
            _page_copy(src_ref, buf_ref, sem, slot, paged, 0, p, s * n_pages + p).wait()


def _compress_body(pt_ref, src_ref, pos_ref, wd_ref, bd2_ref, gain_ref, gg_ref, o_ref,
                   buf_ref, sem, acc_ref, *, n_pages, n_seq, paged, normalize):
    step = pl.program_id(0)
    n_steps = pl.num_programs(0)
    slot = step % 2
    m = n_seq * n_pages
    unroll = 8

    @pl.when(step == 0)
    def _():
        _gather_start(pt_ref, src_ref, buf_ref, sem, 0, 0, n_pages, n_seq, paged)

    @pl.when(step + 1 < n_steps)
    def _():
        _gather_start(pt_ref, src_ref, buf_ref, sem, step + 1, 1 - slot, n_pages, n_seq, paged)

    _gather_wait(src_ref, buf_ref, sem, slot, n_pages, n_seq, paged)
    acc_ref[...] = jnp.zeros_like(acc_ref)

    def chunk(c, carry):
        accs = [acc_ref[g] for g in range(N_KV)]
        for dd in range(unroll):
            d = c * unroll + dd
            w = wd_ref[d]
            pos = pos_ref[pl.ds(d, 1), :]
            for g in range(N_KV):
                rows = buf_ref[slot, g * HEAD_DIM + d] + pos
                accs[g] = accs[g] + _dot(rows.astype(BF16), w)
        for g in range(N_KV):
            acc_ref[g] = accs[g]
        return carry

    lax.fori_loop(0, HEAD_DIM // unroll, chunk, 0)

    for g in range(N_KV):
        y = _dot(jax.nn.gelu(acc_ref[g]).astype(BF16), bd2_ref[...])
        if normalize:
            y = _head_norm(y, gain_ref[...], gg_ref[...])
        for s in range(n_seq):
            o_ref[s, g] = y[s * n_pages:(s + 1) * n_pages].astype(BF16)


def _compress(page_table, src, pos_t, wd, bd2, gain128, gg, normalize, paged, n_seq):
    nb, n_pages = page_table.shape
    hid2 = wd.shape[2]
    full = lambda a: pl.BlockSpec(a.shape, lambda i, pt: (0,) * a.ndim)
    grid_spec = pltpu.PrefetchScalarGridSpec(
        num_scalar_prefetch=1,
        grid=(nb // n_seq,),
        in_specs=[pl.BlockSpec(memory_space=pl.ANY), full(pos_t), full(wd), full(bd2), full(gain128), full(gg)],
        out_specs=pl.BlockSpec((n_seq, N_KV, n_pages, LANES), lambda i, pt: (i, 0, 0, 0)),
        scratch_shapes=[
            pltpu.VMEM((2, KV_W, n_seq * n_pages, PAGE_SIZE), F32),
            pltpu.SemaphoreType.DMA((2,)),
            pltpu.VMEM((N_KV, n_seq * n_pages, hid2), F32),
        ],
    )
    return pl.pallas_call(
        functools.partial(_compress_body, n_pages=n_pages, n_seq=n_seq, paged=paged, normalize=normalize),
        grid_spec=grid_spec,
        out_shape=jax.ShapeDtypeStruct((nb, N_KV, n_pages, LANES), BF16),
        compiler_params=_cparams(1, 48),
        name="compress",
    )(page_table, src, pos_t, wd, bd2, gain128, gg)


def _softmax_cols(s):
    m = jnp.max(s, axis=0, keepdims=True)
    m = jnp.where(m == NEG_INF, 0.0, m)
    e = jnp.exp2(s - m)
    return e, jnp.sum(e, axis=0, keepdims=True)


def _select_bias(imp, cur):
    jj = lax.broadcasted_iota(jnp.int32, imp.shape, 0)
    forced = (jj == 0) | (jj == cur) | (jj == cur - 1)
    cand = (jj <= cur) & jnp.logical_not(forced)
    v = jnp.where(cand, imp, -1.0)
    for _ in range(TOP_N - N_FORCED):
        m = jnp.max(v, axis=0, keepdims=True)
        idx = jnp.min(jnp.where(v == m, jj, imp.shape[0]), axis=0, keepdims=True)
        idx = jnp.where(m >= 0.0, idx, -1)
        v = jnp.where(jj == idx, -1.0, v)
    picked = cand & (v < 0.0)
    return jnp.where(forced | picked, 0.0, NEG_INF)


def _attn_prompt_body(qn_ref, qr_ref, gt_ref, kn_ref, kc_ref, vct_ref, ks_ref, blk_ref, vst_ref, kw_ref, vwt_ref,
                      o_ref, qsel_ref):
    i = pl.program_id(2)
    q_start = i * QBLK
    n_col = HPG * QBLK
    q_pos = q_start + lax.broadcasted_iota(jnp.int32, (1, n_col), 1) % QBLK

    in_half = lax.broadcasted_iota(jnp.int32, (LANES, 1), 0) // HEAD_DIM == pl.program_id(1) % 2
    place = lambda q64: jnp.where(in_half, jnp.concatenate([q64, q64], axis=0), jnp.zeros((), BF16))
    heads_on_lanes = lambda ref: jnp.concatenate(
        [ref[hh * HEAD_DIM:(hh + 1) * HEAD_DIM, :] for hh in range(HPG)], axis=1)
    qn = place(heads_on_lanes(qn_ref))
    qr = place(heads_on_lanes(qr_ref))

    n_cmp = kc_ref.shape[0]
    s = _dot(kc_ref[...], qn)
    cmp_end = lax.broadcasted_iota(jnp.int32, (n_cmp, 1), 0) * CMP_BLK + (CMP_BLK - 1)
    s = jnp.where(cmp_end <= q_pos, s, NEG_INF)
    e, l = _softmax_cols(s)
    p_cmp = e / jnp.maximum(l, 1e-30)
    o_cmp = _dot(vct_ref[...], p_cmp.astype(BF16))

    imp = p_cmp[:, 0:QBLK]
    for hh in range(1, HPG):
        imp = imp + p_cmp[:, hh * QBLK:(hh + 1) * QBLK]
    bias = jnp.maximum(_select_bias(imp, q_pos[:, 0:QBLK] // SEL_BLK), MASKED)
    bias = jnp.concatenate([bias] * HPG, axis=1)
    if n_cmp < LANES:
        bias = jnp.concatenate([bias, jnp.zeros((LANES - n_cmp, n_col), F32)], axis=0)

    n_win = WINDOW + QBLK
    w0 = pl.multiple_of(q_start, QBLK)
    s = _dot(kw_ref[pl.ds(w0, n_win), :], qr)
    r = lax.broadcasted_iota(jnp.int32, (n_win, QBLK), 0)
    q = lax.broadcasted_iota(jnp.int32, (n_win, QBLK), 1)
    valid = (r > q) & (r <= q + WINDOW) & (r + q_start >= WINDOW)
    wbias = jnp.where(valid, 0.0, NEG_INF)
    s = s + jnp.concatenate([wbias] * HPG, axis=1)
    e, l = _softmax_cols(s)
    o_win = _dot(vwt_ref[:, pl.ds(w0, n_win)], e.astype(BF16)) / jnp.maximum(l, 1e-30)

    own = lax.broadcasted_iota(jnp.int32, (LANES, 1), 0) >= q_start // SEL_BLK
    masked = jnp.full((), MASKED, BF16)
    own_keys = jnp.concatenate([ks_ref[pl.ds(w0, QBLK), :], blk_ref[pl.ds(w0, QBLK), :]], axis=1)
    tri = jnp.where(lax.broadcasted_iota(jnp.int32, (QBLK, QBLK), 0)
                    <= lax.broadcasted_iota(jnp.int32, (QBLK, QBLK), 1), 0.0, NEG_INF)
    tri = jnp.concatenate([tri] * HPG, axis=1)
    n_tiles = (q_start + KTILE - 1) // KTILE

    def tile_scores(k0):
        keys = jnp.concatenate([ks_ref[pl.ds(k0, KTILE), :], blk_ref[pl.ds(k0, KTILE), :]], axis=1)
        return _dot(keys, qsel_ref[...])

    q32 = qr.astype(F32)
    bound = jnp.sqrt(jnp.sum(q32 * q32, axis=0, keepdims=True)) * (kn_ref[0:1, 0:1] * BOUND_SLACK)
    shifted = (bias - bound).astype(BF16)
    qsel_ref[0:LANES, :] = qr
    qsel_ref[LANES:, :] = jnp.where(own, masked, shifted)

    def values_and_ones(k0, n):
        return jnp.concatenate([vst_ref[:, pl.ds(k0, n)], jnp.ones((ONES_ROWS, n), BF16)], axis=0)

    def fast_tile(t, acc):
        k0 = pl.multiple_of(t * KTILE, KTILE)
        return acc + _dot(values_and_ones(k0, KTILE), jnp.exp2(tile_scores(k0)).astype(BF16))

    s = _dot(own_keys, jnp.concatenate([qr, shifted], axis=0)) + tri
    acc = _dot(values_and_ones(w0, QBLK), jnp.exp2(s).astype(BF16))
    acc = lax.fori_loop(0, n_tiles // 2, lambda u, a: fast_tile(2 * u + 1, fast_tile(2 * u, a)), acc)
    acc = lax.fori_loop(0, n_tiles % 2, lambda _, a: fast_tile(n_tiles - 1, a), acc)
    l_fast = acc[HEAD_DIM:HEAD_DIM + 1]

    def exact(_):
        bias16 = bias.astype(BF16)
        qsel_ref[LANES:, :] = jnp.where(own, masked, bias16)
        s = _dot(own_keys, jnp.concatenate([qr, bias16], axis=0)) + tri
        m0 = jnp.max(s, axis=0, keepdims=True)
        p = jnp.exp2(s - m0)
        init = (m0, jnp.sum(p, axis=0, keepdims=True), _dot(vst_ref[:, pl.ds(w0, QBLK)], p.astype(BF16)))

        def tile(t, carry):
            m, l, a = carry
            k0 = pl.multiple_of(t * KTILE, KTILE)
            s = tile_scores(k0)
            m_new = jnp.maximum(m, jnp.max(s, axis=0, keepdims=True))
            alpha = jnp.exp2(m - m_new)
            p = jnp.exp2(s - m_new)
            l = alpha * l + jnp.sum(p, axis=0, keepdims=True)
            a = alpha * a + _dot(vst_ref[:, pl.ds(k0, KTILE)], p.astype(BF16))
            return m_new, l, a

        _, l, a = lax.fori_loop(0, n_tiles, tile, init)
        return a / l

    o_sel = lax.cond(jnp.min(l_fast) >= MIN_DENOM, lambda _: acc[:HEAD_DIM] / l_fast, exact, 0)

    def gate(branch):
        first = branch * N_KV * HPG + pl.program_id(1) * HPG
        return jnp.concatenate([gt_ref[pl.ds(first + hh, 1), :] for hh in range(HPG)], axis=1)

    o = gate(0) * o_cmp + gate(1) * o_sel + gate(2) * o_win
    o_ref[...] = jnp.concatenate([o[:, hh * QBLK:(hh + 1) * QBLK] for hh in range(HPG)], axis=0).astype(BF16)


def _attn_prompt(qn_t, qr_t, gt_t, k_norm, kc, vc_t, ks, blk_onehot, vs_t, kw_pad, vw_pad_t):
    nb, d, t = qn_t.shape
    ng = d // (HPG * HEAD_DIM)
    n_cmp = kc.shape[1]
    assert n_cmp <= LANES
    per_q = pl.BlockSpec((None, HPG * HEAD_DIM, QBLK), lambda b, g, i: (b, g, i))
    pair = lambda a: pl.BlockSpec((None, a.shape[1], LANES), lambda b, g, i: (b, 0, g // 2))
    group_rows = lambda a: pl.BlockSpec((None, HEAD_DIM, a.shape[2]), lambda b, g, i: (b, g, 0))
    return pl.pallas_call(
        _attn_prompt_body,
        grid=(nb, ng, t // QBLK),
        in_specs=[per_q, per_q, pl.BlockSpec((None, LANES, QBLK), lambda b, g, i: (b, 0, i)),
                  pl.BlockSpec((None, None, 8, LANES), lambda b, g, i: (b, g, 0, 0)),
                  pair(kc), group_rows(vc_t), pair(ks),
                  pl.BlockSpec(blk_onehot.shape, lambda b, g, i: (0, 0)), group_rows(vs_t),
                  pair(kw_pad), group_rows(vw_pad_t)],
        out_specs=per_q,
        out_shape=jax.ShapeDtypeStruct((nb, d, t), BF16),
        scratch_shapes=[pltpu.VMEM((2 * LANES, HPG * QBLK), BF16)],
        compiler_params=_cparams(3, 48),
        name="attn_prompt",
    )(qn_t, qr_t, gt_t, k_norm, kc, vc_t, ks, blk_onehot, vs_t, kw_pad, vw_pad_t)


def _attn_sample_body(pt_ref, qn_ref, qr_ref, gt_ref, kc_ref, vc_ref, kwin_ref, vwin_ref,
                      ksn_ref, vsn_ref, kwn_ref, vwn_ref, spread_ref, kpool_ref, vpool_ref, o_ref,
                      kbuf_ref, vbuf_ref, ksem, vsem, s_ref, *, n_pages, n_new, past_len):
    b = pl.program_id(0)
    nb = pl.num_programs(0)
    slot = b % 2

    @pl.when(b == 0)
    def _():
        _gather_start(pt_ref, kpool_ref, kbuf_ref, ksem, 0, 0, n_pages, 1, True)
        _gather_start(pt_ref, vpool_ref, vbuf_ref, vsem, 0, 0, n_pages, 1, True)

    @pl.when(b + 1 < nb)
    def _():
        _gather_start(pt_ref, kpool_ref, kbuf_ref, ksem, b + 1, 1 - slot, n_pages, 1, True)
        _gather_start(pt_ref, vpool_ref, vbuf_ref, vsem, b + 1, 1 - slot, n_pages, 1, True)

    qn = qn_ref[0]
    qr = qr_ref[0]
    n_col = qr.shape[0]
    n_past = n_pages * PAGE_SIZE
    n_blk = n_past // SEL_BLK
    n_gq = N_KV * n_new
    dot_nt = lambda a, bt: lax.dot_general(a, bt, (((1,), (1,)), ((), ())), preferred_element_type=F32)

    lane = lax.broadcasted_iota(jnp.int32, (1, LANES), 1)
    lane_pos = past_len + lane % n_new
    s = _dot(kc_ref[0], qn)
    cmp_end = lax.broadcasted_iota(jnp.int32, (n_blk, 1), 0) * CMP_BLK + (CMP_BLK - 1)
    s = jnp.where(cmp_end <= lane_pos, s, NEG_INF)
    e, l = _softmax_cols(s)
    p_cmp = e / jnp.maximum(l, 1e-30)
    o_cmp = _dot(p_cmp.T.astype(BF16), vc_ref[0])[:n_col]

    imp = p_cmp
    for hh in range(1, HPG):
        imp = imp + pltpu.roll(p_cmp, LANES - hh * n_gq, 1)
    sel = _select_bias(imp, lane_pos // SEL_BLK)
    sel01 = jnp.where((lane < n_gq) & (sel == 0.0), 1.0, 0.0)
    spread = sel01
    for hh in range(1, HPG):
        spread = spread + pltpu.roll(sel01, hh * n_gq, 1)
    sel_bias = jnp.where(spread.T[:n_col] > 0.5, 0.0, MASKED).astype(BF16)

    row = lax.broadcasted_iota(jnp.int32, (n_col, 1), 0)
    q_idx = row % n_new
    q_pos = past_len + q_idx
    n_pad = ksn_ref.shape[1]
    new_idx = lax.broadcasted_iota(jnp.int32, (1, n_pad), 1)
    new_ok = (new_idx <= q_idx) & (new_idx < n_new)

    _gather_wait(kpool_ref, kbuf_ref, ksem, slot, n_pages, 1, True)
    _gather_wait(vpool_ref, vbuf_ref, vsem, slot, n_pages, 1, True)
    pages_per_chunk = 8
    chunk = pages_per_chunk * PAGE_SIZE

    def page_rows(c, pp):
        return pl.ds(pl.multiple_of((c * pages_per_chunk + pp) * KV_W, KV_W), KV_W)

    def score_chunk(c, m):
        k0 = pl.multiple_of(c * chunk, chunk)
        s = jnp.concatenate([_dot(qr, kbuf_ref[slot, page_rows(c, pp), :].astype(BF16))
                             for pp in range(pages_per_chunk)], axis=1)
        s = s + _dot(sel_bias, spread_ref[:, pl.ds(k0, chunk)])
        s_ref[:, pl.ds(k0, chunk)] = s
        return jnp.maximum(m, jnp.max(s, axis=1, keepdims=True))

    m = lax.fori_loop(0, n_pages // pages_per_chunk, score_chunk, jnp.full((n_col, 1), NEG_INF, F32))
    s_new = jnp.where(new_ok, dot_nt(qr, ksn_ref[0].astype(BF16)), NEG_INF)
    m = jnp.maximum(m, jnp.max(s_new, axis=1, keepdims=True))
    m = jnp.where(m == NEG_INF, 0.0, m)

    def value_chunk(c, carry):
        l, acc = carry
        k0 = pl.multiple_of(c * chunk, chunk)
        p = jnp.exp2(s_ref[:, pl.ds(k0, chunk)] - m)
        l = l + jnp.sum(p, axis=1, keepdims=True)
        p = p.astype(BF16)
        for pp in range(pages_per_chunk):
            acc = acc + dot_nt(p[:, pp * PAGE_SIZE:(pp + 1) * PAGE_SIZE],
                               vbuf_ref[slot, page_rows(c, pp), :].astype(BF16))
        return l, acc

    p_new = jnp.exp2(s_new - m)
    l_sel, acc = lax.fori_loop(
        0, n_pages // pages_per_chunk, value_chunk,
        (jnp.sum(p_new, axis=1, keepdims=True), _dot(p_new.astype(BF16), vsn_ref[0].astype(BF16))))
    o_sel = acc / jnp.maximum(l_sel, 1e-30)

    n_buf = kwin_ref.shape[2]
    s_buf = _dot(qr, kwin_ref[0].astype(BF16))
    buf_pos = past_len - n_buf + lax.broadcasted_iota(jnp.int32, (1, n_buf), 1)
    diff = q_pos - buf_pos
    s_buf = jnp.where((diff >= 0) & (diff < WINDOW) & (buf_pos >= 0), s_buf, NEG_INF)
    s_nw = jnp.where(new_ok, dot_nt(qr, kwn_ref[0].astype(BF16)), NEG_INF)
    m = jnp.maximum(jnp.max(s_buf, axis=1, keepdims=True), jnp.max(s_nw, axis=1, keepdims=True))
    m = jnp.where(m == NEG_INF, 0.0, m)
    p_buf = jnp.exp2(s_buf - m)
    p_nw = jnp.exp2(s_nw - m)
    l_win = jnp.sum(p_buf, axis=1, keepdims=True) + jnp.sum(p_nw, axis=1, keepdims=True)
    o_win = (dot_nt(p_buf.astype(BF16), vwin_ref[0].astype(BF16))
             + _dot(p_nw.astype(BF16), vwn_ref[0].astype(BF16))) / jnp.maximum(l_win, 1e-30)

    gt = gt_ref[0]
    o = gt[:, 0:1] * o_cmp + gt[:, 1:2] * o_sel + gt[:, 2:3] * o_win
    row_g = (lax.broadcasted_iota(jnp.int32, (n_col, KV_W), 0) // n_new) % N_KV
    lane_g = lax.broadcasted_iota(jnp.int32, (n_col, KV_W), 1) // HEAD_DIM
    o = jnp.where(row_g == lane_g, o, 0.0)
    z = o[:, :LANES] + o[:, LANES:]
    o_ref[0] = z + pltpu.roll(z, HEAD_DIM, 1)


def _attn_sample(page_table, qn_bd, qr_bd_t, gt_col, kc, vc, kwin_t, vwin_t, ks_new, vs_new, kw_new, vw_new,
                 blk_spread, k_pool, v_pool, n_new, past_len):
    nb, n_pages = page_table.shape
    n_past = n_pages * PAGE_SIZE
    n_col = qr_bd_t.shape[1]
    per_b = lambda a: pl.BlockSpec((1,) + a.shape[1:], lambda b, pt: (b,) + (0,) * (a.ndim - 1))
    any_spec = pl.BlockSpec(memory_space=pl.ANY)
    ins = (qn_bd, qr_bd_t, gt_col, kc, vc, kwin_t, vwin_t, ks_new, vs_new, kw_new, vw_new)
    grid_spec = pltpu.PrefetchScalarGridSpec(
        num_scalar_prefetch=1,
        grid=(nb,),
        in_specs=[per_b(a) for a in ins]
        + [pl.BlockSpec(blk_spread.shape, lambda b, pt: (0, 0)), any_spec, any_spec],
        out_specs=pl.BlockSpec((1, n_col, LANES), lambda b, pt: (b, 0, 0)),
        scratch_shapes=[
            pltpu.VMEM((2, n_pages * KV_W, PAGE_SIZE), F32),
            pltpu.VMEM((2, n_pages * KV_W, PAGE_SIZE), F32),
            pltpu.SemaphoreType.DMA((2,)),
            pltpu.SemaphoreType.DMA((2,)),
            pltpu.VMEM((n_col, n_past), F32),
        ],
    )
    return pl.pallas_call(
        functools.partial(_attn_sample_body, n_pages=n_pages, n_new=n_new, past_len=past_len),
        grid_spec=grid_spec,
        out_shape=jax.ShapeDtypeStruct((nb, n_col, LANES), F32),
        compiler_params=_cparams(1, 56),
        name="attn_sample",
    )(page_table, *ins, blk_spread, k_pool, v_pool)


def _rope_tables(pos):
    half = HEAD_DIM // 2
    inv = ROPE_THETA ** (-jnp.arange(half, dtype=F32) / half)
    ang = pos.astype(F32)[:, None] * inv
    cos, sin = jnp.cos(ang), jnp.sin(ang)
    reps = LANES // HEAD_DIM
    return (jnp.tile(jnp.concatenate([cos, cos], axis=1), (1, reps)),
            jnp.tile(jnp.concatenate([-sin, sin], axis=1), (1, reps)))


def _group_sum_matrix():
    lane = jnp.arange(LANES)
    g = (lane[:, None] // HEAD_DIM == lane[None, :] // HEAD_DIM).astype(BF16)
    return jnp.concatenate([g, g], axis=0)


def _compress_weights(cmp_pos, cmp_w1, cmp_w2):
    n_pair = PAGE_SIZE // CMP_BLK
    eye = jnp.eye(n_pair, dtype=F32)
    hid = cmp_w1.shape[1]
    w1 = cmp_w1.reshape(CMP_BLK, HEAD_DIM, hid)
    wd = jnp.einsum("bk,rdh->dbrkh", eye, w1).reshape(HEAD_DIM, PAGE_SIZE, n_pair * hid).astype(BF16)
    bd2 = jnp.einsum("bk,hd->bhkd", eye, cmp_w2).reshape(n_pair * hid, n_pair * HEAD_DIM).astype(BF16)
    return jnp.tile(cmp_pos.T, (1, n_pair)), wd, bd2


def _to_rows_major(a_t):
    nb, _, rows = a_t.shape
    return a_t.reshape(nb, N_KV, HEAD_DIM, rows).transpose(0, 3, 1, 2)[None]


def _from_rows_major(a):
    n, rows = a.shape[:2]
    return a.transpose(0, 2, 3, 1).reshape(n, KV_W, rows)


def _layer_tail(x, p, norm_g, w_gu, w_down, w_ple_proj, w_ple_gate, mixer_out=None):
    return _ffn(x, norm_g[2], w_gu[1], w_down[1], mixer_out=mixer_out,
                ple=(norm_g[3], p, w_ple_gate, w_ple_proj))


def kernel(x_prompt, x_sample, state_conv, cache_k_cmp, cache_v_cmp, cache_k_sel, cache_v_sel,
           cache_k_win, cache_v_win, page_table, p_prompt, p_sample, norm_g, ffn_w_gu, ffn_w_down,
           ple_w_proj, ple_w_gate, conv_w_in, conv_w, conv_w_out, nsa_w_in, nsa_qk_g, nsa_cmp_pos,
           nsa_cmp_w1, nsa_cmp_w2, nsa_w_out):
    nb_p, t_p, d = x_prompt.shape
    nb_s, t_s, _ = x_sample.shape
    ple_dim = p_prompt.shape[-1]
    past_len = page_table.shape[1] * PAGE_SIZE
    assert t_p % KTILE == 0 and t_p % PAGE_SIZE == 0 and d % LANES == 0

    w_gu = ffn_w_gu.astype(BF16)
    w_down = ffn_w_down.astype(BF16)
    w_pp = ple_w_proj.astype(BF16)
    w_pg = ple_w_gate.astype(BF16)

    xp = x_prompt.reshape(nb_p * t_p, d)
    xs = x_sample.reshape(nb_s * t_s, d)
    pp = p_prompt.reshape(-1, nb_p * t_p, ple_dim)
    ps = p_sample.reshape(-1, nb_s * t_s, ple_dim)

    lw = (norm_g[0], w_gu[0], w_down[0], w_pp[0], w_pg[0])
    cw = (conv_w_in[0].astype(BF16), conv_w[0], conv_w_out[0].astype(BF16))
    xp = _ffn(xp, norm_g[0, 0], w_gu[0, 0], w_down[0, 0])
    xs = _ffn(xs, norm_g[0, 0], w_gu[0, 0], w_down[0, 0])
    xp3, conv_p = _conv_mixer(xp.reshape(nb_p, t_p, d), jnp.zeros((nb_p, CONV_W - 1, d), F32),
                              norm_g[0, 1], *cw, shift=1)
    xs_tb = xs.reshape(nb_s, t_s, d).transpose(1, 0, 2).reshape(1, t_s * nb_s, d)
    prev_tb = state_conv[0].transpose(1, 0, 2).reshape(1, (CONV_W - 1) * nb_s, d)
    xs3, conv_s = _conv_mixer(xs_tb, prev_tb, norm_g[0, 1], *cw, shift=nb_s)
    xs = xs3.reshape(t_s, nb_s, d).transpose(1, 0, 2).reshape(nb_s * t_s, d)
    conv_s = conv_s.reshape(CONV_W - 1, nb_s, d).transpose(1, 0, 2)
    xp = _layer_tail(xp3.reshape(nb_p * t_p, d), pp[0], *lw)
    xs = _layer_tail(xs, ps[0], *lw)

    lw = (norm_g[1], w_gu[1], w_down[1], w_pp[1], w_pg[1])
    xp = _ffn(xp, norm_g[1, 0], w_gu[1, 0], w_down[1, 0])
    xs = _ffn(xs, norm_g[1, 0], w_gu[1, 0], w_down[1, 0])

    n_in = nsa_w_in.shape[-1]
    n_in_pad = -(-n_in // LANES) * LANES
    n_gate = N_KV * HPG * 3
    gate_cols = nsa_w_in[0][:, n_in - n_gate:].reshape(d, N_KV * HPG, 3).transpose(0, 2, 1).reshape(d, n_gate)
    w_in = jnp.concatenate([nsa_w_in[0][:, :n_in - n_gate], gate_cols], axis=1)
    w_in = jnp.pad(w_in, ((0, 0), (0, n_in_pad - n_in))).astype(BF16)
    w_out = nsa_w_out[0].astype(BF16)
    qkg128 = jnp.tile(nsa_qk_g[0], (1, LANES // HEAD_DIM))
    gg = _group_sum_matrix()
    cos_p, sin_p = _rope_tables(jnp.arange(t_p))
    cos_s, sin_s = _rope_tables(past_len + jnp.arange(t_s))
    pos_k, wd_k, bd2_k = _compress_weights(nsa_cmp_pos[0, 0], nsa_cmp_w1[0, 0], nsa_cmp_w2[0, 0])
    pos_v, wd_v, bd2_v = _compress_weights(nsa_cmp_pos[0, 1], nsa_cmp_w1[0, 1], nsa_cmp_w2[0, 1])
    k_gain = qkg128[1:2]

    qn, qr, kc_t, vc_t, ks_t, vs_t, kw_t, vw_t, gt, ks_b, kw_b, vs_tb, vw_tb, k_norm2 = _nsa_proj(
        xp, norm_g[1, 1], w_in, qkg128, jnp.tile(cos_p, (nb_p, 1)), jnp.tile(sin_p, (nb_p, 1)), gg, n_batch=nb_p)
    pt_p = jnp.zeros((nb_p, t_p // PAGE_SIZE), jnp.int32)
    as_blocks = lambda c: c.reshape(c.shape[0], N_KV, -1, HEAD_DIM)
    by_block = lambda c: as_blocks(c).transpose(0, 2, 1, 3).reshape(c.shape[0], -1, KV_W)
    kc = _compress(pt_p, kc_t, pos_k, wd_k, bd2_k, k_gain, gg, True, False, nb_p)
    vc = _compress(pt_p, vc_t, pos_v, wd_v, bd2_v, k_gain, gg, False, False, nb_p)

    blk_onehot = (jnp.arange(t_p)[:, None] // SEL_BLK == jnp.arange(LANES)[None, :]).astype(BF16)
    k_norm = jnp.sqrt(k_norm2.reshape(nb_p, -1, N_KV, HEAD_DIM).max(axis=(1, 3)))
    k_norm = jnp.broadcast_to(k_norm[:, :, None, None], (nb_p, N_KV, 8, LANES))
    o_t = _attn_prompt(
        qn, qr, gt, k_norm,
        by_block(kc), as_blocks(vc).transpose(0, 1, 3, 2).reshape(nb_p, KV_W, -1),
        ks_b.reshape(nb_p, t_p, KV_W), blk_onehot, vs_tb,
        jnp.pad(kw_b.reshape(nb_p, t_p, KV_W), ((0, 0), (WINDOW, 0), (0, 0))),
        jnp.pad(vw_tb, ((0, 0), (0, 0), (WINDOW, 0))))
    keep = min(WINDOW, t_p)
    nsa_p = tuple(_to_rows_major(a) for a in (kc_t, vc_t, ks_t, vs_t, kw_t[:, :, -keep:], vw_t[:, :, -keep:]))

    qn, qr, kc_r, vc_r, ks, vs, kw, vw, gt = _nsa_proj(
        xs, norm_g[1, 1], w_in, qkg128, jnp.tile(cos_s, (nb_s, 1)), jnp.tile(sin_s, (nb_s, 1)), gg)
    pool_t = lambda c: _from_rows_major(c[0])
    n_seq = 2 if nb_s % 2 == 0 else 1
    kc = by_block(_compress(page_table, pool_t(cache_k_cmp), pos_k, wd_k, bd2_k, k_gain, gg, True, True, n_seq))
    vc = by_block(_compress(page_table, pool_t(cache_v_cmp), pos_v, wd_v, bd2_v, k_gain, gg, False, True, n_seq))

    n_used = HPG * N_KV * t_s

    def q_block_diag(a):
        a = a.reshape(nb_s, t_s, N_KV, HPG, HEAD_DIM).astype(F32)
        return jnp.einsum("bqghd,gk->bgdhkq", a, jnp.eye(N_KV, dtype=F32)).reshape(nb_s, KV_W, n_used)

    qn_bd = jnp.pad(q_block_diag(qn), ((0, 0), (0, 0), (0, LANES - n_used))).astype(BF16)
    qr_bd_t = q_block_diag(qr).transpose(0, 2, 1).astype(BF16)
    gt_col = gt[:, :n_gate].reshape(nb_s, t_s, 3, N_KV, HPG).transpose(0, 4, 3, 1, 2)
    gt_col = jnp.pad(gt_col.reshape(nb_s, n_used, 3), ((0, 0), (0, 0), (0, LANES - 3)))
    new_rows = lambda a: jnp.pad(a.reshape(nb_s, t_s, KV_W), ((0, 0), (0, 16 - t_s), (0, 0)))
    n_buf = cache_k_win.shape[2]
    kwin_t, vwin_t = _from_rows_major(cache_k_win[0]), _from_rows_major(cache_v_win[0])
    blk_spread = (jnp.arange(past_len // SEL_BLK)[:, None] == jnp.arange(past_len)[None, :] // SEL_BLK).astype(BF16)
    o_s = _attn_sample(
        page_table, qn_bd, qr_bd_t, gt_col, kc, vc, kwin_t, vwin_t,
        new_rows(ks), new_rows(vs), new_rows(kw), new_rows(vw), blk_spread,
        pool_t(cache_k_sel), pool_t(cache_v_sel), t_s, past_len)
    o = o_s[:, :, :HEAD_DIM].reshape(nb_s, HPG, N_KV, t_s, HEAD_DIM).transpose(0, 3, 2, 1, 4)
    o_s = o.reshape(nb_s * t_s, d).astype(BF16)
    st = lambda a: a.reshape(1, nb_s, t_s, N_KV, HEAD_DIM)
    new_t = lambda a: a.reshape(nb_s, t_s, KV_W).transpose(0, 2, 1)
    win = lambda buf_t, new: _to_rows_major(jnp.concatenate([buf_t, new_t(new)], axis=2)[:, :, -n_buf:])
    nsa_s = (st(kc_r), st(vc_r), st(ks), st(vs), win(kwin_t, kw), win(vwin_t, vw))

    xp = _layer_tail(xp, pp[1], *lw, mixer_out=(o_t, w_out))
    xs = _layer_tail(xs, ps[1], *lw, mixer_out=(o_s, w_out))

    return (xp.reshape(nb_p, t_p, d), xs.reshape(nb_s, t_s, d), conv_p[None], conv_s[None]) + nsa_p + nsa_s
```

```python
import functools

import jax
import jax.numpy as jnp
from jax import lax
from jax.experimental import pallas as pl
from jax.experimental.pallas import tpu as pltpu

F32 = jnp.float32
BF16 = jnp.bfloat16

HEAD_DIM = 64
N_KV = 4
HPG = 4
CMP_BLK = 64
SEL_BLK = 64
TOP_N = 16
N_FORCED = 3
WINDOW = 512
PAGE_SIZE = 128
CONV_W = 3
QBLK = 256
KTILE = 512
ROPE_THETA = 10000.0
RMS_EPS = 1e-6
NEG_INF = float("-inf")
MASKED = -1e30
LOG2E = 1.4426950408889634
BOUND_SLACK = 1.02
MIN_DENOM = 2.0 ** -100
ONES_ROWS = 16
KV_W = N_KV * HEAD_DIM
ROW_TILE = 512
LANES = 128
MIB = 1024 * 1024


def _cparams(n_grid, vmem_mib):
    return pltpu.CompilerParams(
        dimension_semantics=("arbitrary",) * n_grid,
        vmem_limit_bytes=vmem_mib * MIB,
    )


def _rms(x, g):
    return x * lax.rsqrt(jnp.mean(x * x, axis=-1, keepdims=True) + RMS_EPS) * g


def _dot(a, b):
    return jnp.dot(a, b, preferred_element_type=F32)


def _ffn_body(x_ref, g_ref, wgu_ref, wd_ref, *rest, f, mixer, with_ple):
    x = x_ref[...]
    if mixer is not None:
        o_ref, wo_ref, *rest = rest
        if mixer == "rows_last":
            x = x + lax.dot_general(o_ref[...], wo_ref[...], (((0,), (0,)), ((), ())),
                                    preferred_element_type=F32)
        else:
            x = x + _dot(o_ref[...], wo_ref[...])
    h = _rms(x, g_ref[...]).astype(BF16)
    gate_up = _dot(h, wgu_ref[...])
    act = (jax.nn.silu(gate_up[:, :f]) * gate_up[:, f:]).astype(BF16)
    y = x + 0.5 * _dot(act, wd_ref[...])
    if with_ple:
        g2_ref, p_ref, wg_ref, wp_ref, out_ref = rest
        gate = jax.nn.sigmoid(_dot(_rms(y, g2_ref[...]).astype(BF16), wg_ref[...]))
        y = y + gate * _dot(p_ref[...].astype(BF16), wp_ref[...])
    else:
        out_ref, = rest
    out_ref[...] = y


def _ffn(x, g, w_gu, w_down, at, mixer_out=None, ple=None):
    m, d = x.shape
    f = w_down.shape[-2]
    tm = min(m, ROW_TILE)
    row = lambda w: pl.BlockSpec((tm, w), lambda i: (i, 0))
    vec = pl.BlockSpec((1, d), lambda i: (0, 0))

    def resident(a, lead=()):
        zeros = (0,) * (a.ndim - len(lead))
        return pl.BlockSpec((None,) * len(lead) + a.shape[len(lead):], lambda i: lead + zeros,
                            pipeline_mode=pl.Buffered(1))

    args = [x, g.reshape(1, d), w_gu, w_down]
    in_specs = [row(d), vec, resident(w_gu, at), resident(w_down, at)]
    mixer = None
    if mixer_out is not None:
        o, w_o = mixer_out
        if o.ndim == 3:
            mixer = "rows_last"
            tiles = o.shape[2] // tm
            o_spec = pl.BlockSpec((None, o.shape[1], tm), lambda i: (i // tiles, 0, i % tiles))
        else:
            mixer = "rows_first"
            o_spec = row(o.shape[1])
        args += [o, w_o]
        in_specs += [o_spec, resident(w_o)]
    if ple is not None:
        g2, p, w_gate, w_proj = ple
        args += [g2.reshape(1, d), p, w_gate, w_proj]
        in_specs += [vec, row(p.shape[1]), resident(w_gate, at[:1]), resident(w_proj, at[:1])]
    return pl.pallas_call(
        functools.partial(_ffn_body, f=f, mixer=mixer, with_ple=ple is not None),
        grid=(m // tm,),
        in_specs=in_specs,
        out_specs=row(d),
        out_shape=jax.ShapeDtypeStruct((m, d), F32),
        compiler_params=_cparams(1, 56),
        name="ffn",
    )(*args)


def _conv_body(x_ref, prev_ref, g_ref, win_ref, wc_ref, wout_ref, y_ref, st_ref, u_ref,
               *, shift, halo, tm, d):
    t = pl.program_id(1)
    lo = halo - 2 * shift

    @pl.when(t == 0)
    def _():
        u_ref[lo:halo, :] = prev_ref[0]

    x = x_ref[0]
    h = _rms(x, g_ref[...]).astype(BF16)
    proj = _dot(h, win_ref[...])
    b_gate = proj[:, :d]
    u = proj[:, d:2 * d] * proj[:, 2 * d:]
    u_ref[halo:halo + tm, :] = u
    wc = wc_ref[...]
    y = (wc[0:1, :] * u_ref[lo:lo + tm, :]
         + wc[1:2, :] * u_ref[lo + shift:lo + shift + tm, :]
         + wc[2:3, :] * u)
    y_ref[0] = x + _dot((b_gate * y).astype(BF16), wout_ref[...])
    tail = u_ref[lo + tm:halo + tm, :]
    st_ref[0] = tail
    u_ref[lo:halo, :] = tail


def _conv_mixer(x, prev, g, w_in, w_conv, w_out, shift):
    nb, t, d = x.shape
    tm = min(t, ROW_TILE)
    halo = -(-2 * shift // 8) * 8
    return pl.pallas_call(
        functools.partial(_conv_body, shift=shift, halo=halo, tm=tm, d=d),
        grid=(nb, t // tm),
        in_specs=[
            pl.BlockSpec((1, tm, d), lambda b, i: (b, i, 0)),
            pl.BlockSpec((1, 2 * shift, d), lambda b, i: (b, 0, 0)),
            pl.BlockSpec((1, d), lambda b, i: (0, 0)),
            pl.BlockSpec((d, 3 * d), lambda b, i: (0, 0)),
            pl.BlockSpec((CONV_W, d), lambda b, i: (0, 0)),
            pl.BlockSpec((d, d), lambda b, i: (0, 0)),
        ],
        out_specs=[
            pl.BlockSpec((1, tm, d), lambda b, i: (b, i, 0)),
            pl.BlockSpec((1, 2 * shift, d), lambda b, i: (b, 0, 0)),
        ],
        out_shape=[
            jax.ShapeDtypeStruct((nb, t, d), F32),
            jax.ShapeDtypeStruct((nb, 2 * shift, d), F32),
        ],
        scratch_shapes=[pltpu.VMEM((halo + tm, d), F32)],
        compiler_params=_cparams(2, 48),
        name="conv_mixer",
    )(x, prev, g.reshape(1, d), w_in, w_conv, w_out)


def _head_norm(xc, gain, gg):
    sq = xc * xc
    hi = sq.astype(BF16)
    lo = (sq - hi.astype(F32)).astype(BF16)
    ss = _dot(jnp.concatenate([hi, lo], axis=1), gg)
    return xc * lax.rsqrt(ss * (1.0 / HEAD_DIM) + RMS_EPS) * gain


def _nsa_proj_body(x_ref, g_ref, w_ref, qkg_ref, cos_ref, sin_ref, gg_ref,
                   qn_ref, qr_ref, kc_ref, vc_ref, ks_ref, vs_ref, kw_ref, vw_ref, gt_ref, *extra_refs,
                   tm, d, rows_last):
    h = _rms(x_ref[...], g_ref[...]).astype(BF16)
    proj = _dot(h, w_ref[...])
    cos = cos_ref[...]
    sin = sin_ref[...]
    gg = gg_ref[...]
    lane = lax.broadcasted_iota(jnp.int32, (tm, LANES), 1)
    first_half = (lane % HEAD_DIM) < (HEAD_DIM // 2)
    scale = HEAD_DIM ** -0.5 * LOG2E

    def rope(xc):
        swapped = jnp.where(first_half, pltpu.roll(xc, LANES - HEAD_DIM // 2, 1),
                            pltpu.roll(xc, HEAD_DIM // 2, 1))
        return xc * cos + swapped * sin

    def slab(c):
        return proj[:, c * LANES:(c + 1) * LANES]

    n_q = d // LANES
    for c in range(n_q):
        qc = _head_norm(slab(c), qkg_ref[0:1, :], gg)
        if rows_last:
            qn_ref[c * LANES:(c + 1) * LANES, :] = (qc * scale).T.astype(BF16)
            qr_ref[c * LANES:(c + 1) * LANES, :] = (rope(qc) * scale).T.astype(BF16)
        else:
            qn_ref[:, c * LANES:(c + 1) * LANES] = (qc * scale).astype(BF16)
            qr_ref[:, c * LANES:(c + 1) * LANES] = (rope(qc) * scale).astype(BF16)
    def put(ref, c, val):
        if rows_last:
            ref[c * LANES:(c + 1) * LANES, :] = val
        else:
            ref[:, c * LANES:(c + 1) * LANES] = val

    for c in range(2):
        lanes = slice(c * LANES, (c + 1) * LANES)
        k_sel = rope(_head_norm(slab(n_q + 4 + c), qkg_ref[2:3, :], gg))
        k_win = rope(_head_norm(slab(n_q + 8 + c), qkg_ref[3:4, :], gg))
        vals = (slab(n_q + c), slab(n_q + 2 + c), k_sel, slab(n_q + 6 + c), k_win, slab(n_q + 10 + c))
        if rows_last:
            vals = tuple(v.T for v in vals)
            ksb_ref, kwb_ref, vsb_ref, vwb_ref, kn_ref = extra_refs
            k16 = k_sel.astype(BF16)
            ksb_ref[:, lanes] = k16
            k32 = k16.astype(F32)
            sq = k32 * k32
            hi = sq.astype(BF16)
            norm2 = _dot(jnp.concatenate([hi, (sq - hi.astype(F32)).astype(BF16)], axis=1), gg)
            kn_ref[:, lanes] = jnp.broadcast_to(jnp.max(norm2, axis=0, keepdims=True), (8, LANES))
            kwb_ref[:, lanes] = k_win.astype(BF16)
            vsb_ref[lanes, :] = vals[3].astype(BF16)
            vwb_ref[lanes, :] = vals[5].astype(BF16)
        for ref, v in zip((kc_ref, vc_ref, ks_ref, vs_ref, kw_ref, vw_ref), vals):
            put(ref, c, v)
    gates = jax.nn.sigmoid(slab(n_q + 12))
    gt_ref[...] = gates.T if rows_last else gates


def _nsa_proj(x, g, w_in_pad, qkg128, cos128, sin128, gg, n_batch=None):
    m, d = x.shape
    n = w_in_pad.shape[1]
    tm = min(m, ROW_TILE)
    row = lambda w: pl.BlockSpec((tm, w), lambda i: (i, 0))
    full = lambda a: pl.BlockSpec(a.shape, lambda i: (0,) * a.ndim)
    rows_last = n_batch is not None
    if rows_last:
        t = m // n_batch
        tiles = t // tm
        t_spec = lambda w: pl.BlockSpec((None, w, tm), lambda i: (i // tiles, 0, i % tiles))
        t_shape = lambda w, dt: jax.ShapeDtypeStruct((n_batch, w, t), dt)
        kv_spec, q_spec, gt_spec = t_spec(KV_W), t_spec(d), t_spec(LANES)
        kv_shape = lambda dt: t_shape(KV_W, dt)
        q_shape, gt_shape = t_shape(d, BF16), t_shape(LANES, F32)
        extra_specs = [row(KV_W), row(KV_W), kv_spec, kv_spec, pl.BlockSpec((8, KV_W), lambda i: (i, 0))]
        extra_shapes = ([jax.ShapeDtypeStruct((m, KV_W), BF16)] * 2 + [kv_shape(BF16)] * 2
                        + [jax.ShapeDtypeStruct((m // tm * 8, KV_W), F32)])
    else:
        kv_spec, kv_shape = row(KV_W), lambda dt: jax.ShapeDtypeStruct((m, KV_W), dt)
        q_spec, gt_spec = row(d), row(LANES)
        q_shape, gt_shape = jax.ShapeDtypeStruct((m, d), BF16), jax.ShapeDtypeStruct((m, LANES), F32)
        extra_specs, extra_shapes = [], []
    return pl.pallas_call(
        functools.partial(_nsa_proj_body, tm=tm, d=d, rows_last=rows_last),
        grid=(m // tm,),
        in_specs=[row(d), pl.BlockSpec((1, d), lambda i: (0, 0)), pl.BlockSpec((d, n), lambda i: (0, 0)),
                  full(qkg128), row(LANES), row(LANES), full(gg)],
        out_specs=[q_spec, q_spec] + [kv_spec] * 6 + [gt_spec] + extra_specs,
        out_shape=[q_shape] * 2 + [kv_shape(F32)] * 6 + [gt_shape] + extra_shapes,
        compiler_params=_cparams(1, 48),
        name="nsa_proj",
    )(x, g.reshape(1, d), w_in_pad, qkg128, cos128, sin128, gg)


def _page_copy(src_ref, buf_ref, sem, slot, paged, index, p, dst_page):
    src = src_ref.at[index] if paged else src_ref.at[index, :, pl.ds(p * PAGE_SIZE, PAGE_SIZE)]
    dst = buf_ref.at[slot, :, dst_page, :] if buf_ref.ndim == 4 else buf_ref.at[slot, pl.ds(dst_page * KV_W, KV_W)]
    return pltpu.make_async_copy(src, dst, sem.at[slot])


def _gather_start(pt_ref, src_ref, buf_ref, sem, step, slot, n_pages, n_seq, paged):
    for s in range(n_seq):
        seq = step * n_seq + s
        for p in range(n_pages):
            index = pt_ref[seq, p] if paged else seq
            _page_copy(src_ref, buf_ref, sem, slot, paged, index, p, s * n_pages + p).start()


def _gather_wait(src_ref, buf_ref, sem, slot, n_pages, n_seq, paged):
    for s in range(n_seq):
        for p in range(n_pages):
            _page_copy(src_ref, buf_ref, sem, slot, paged, 0, p, s * n_pages + p).wait()


def _compress_body(pt_ref, src_ref, pos_ref, wd_ref, bd2_ref, gain_ref, gg_ref, o_ref,
                   buf_ref, sem, acc_ref, *, n_pages, n_seq, paged, normalize):
    step = pl.program_id(0)
    n_steps = pl.num_programs(0)
    slot = step % 2
    m = n_seq * n_pages
    unroll = 8

    @pl.when(step == 0)
    def _():
        _gather_start(pt_ref, src_ref, buf_ref, sem, 0, 0, n_pages, n_seq, paged)

    @pl.when(step + 1 < n_steps)
    def _():
        _gather_start(pt_ref, src_ref, buf_ref, sem, step + 1, 1 - slot, n_pages, n_seq, paged)

    _gather_wait(src_ref, buf_ref, sem, slot, n_pages, n_seq, paged)
    acc_ref[...] = jnp.zeros_like(acc_ref)

    def chunk(c, carry):
        accs = [acc_ref[g] for g in range(N_KV)]
        for dd in range(unroll):
            d = c * unroll + dd
            w = wd_ref[d]
            pos = pos_ref[pl.ds(d, 1), :]
            for g in range(N_KV):
                rows = buf_ref[slot, g * HEAD_DIM + d] + pos
                accs[g] = accs[g] + _dot(rows.astype(BF16), w)
        for g in range(N_KV):
            acc_ref[g] = accs[g]
        return carry

    lax.fori_loop(0, HEAD_DIM // unroll, chunk, 0)

    for g in range(N_KV):
        y = _dot(jax.nn.gelu(acc_ref[g]).astype(BF16), bd2_ref[...])
        if normalize:
            y = _head_norm(y, gain_ref[...], gg_ref[...])
        for s in range(n_seq):
            o_ref[s, g] = y[s * n_pages:(s + 1) * n_pages].astype(BF16)


def _compress(page_table, src, pos_t, wd, bd2, gain128, gg, normalize, paged, n_seq):
    nb, n_pages = page_table.shape
    hid2 = wd.shape[2]
    full = lambda a: pl.BlockSpec(a.shape, lambda i, pt: (0,) * a.ndim)
    grid_spec = pltpu.PrefetchScalarGridSpec(
        num_scalar_prefetch=1,
        grid=(nb // n_seq,),
        in_specs=[pl.BlockSpec(memory_space=pl.ANY), full(pos_t), full(wd), full(bd2), full(gain128), full(gg)],
        out_specs=pl.BlockSpec((n_seq, N_KV, n_pages, LANES), lambda i, pt: (i, 0, 0, 0)),
        scratch_shapes=[
            pltpu.VMEM((2, KV_W, n_seq * n_pages, PAGE_SIZE), F32),
            pltpu.SemaphoreType.DMA((2,)),
            pltpu.VMEM((N_KV, n_seq * n_pages, hid2), F32),
        ],
    )
    return pl.pallas_call(
        functools.partial(_compress_body, n_pages=n_pages, n_seq=n_seq, paged=paged, normalize=normalize),
        grid_spec=grid_spec,
        out_shape=jax.ShapeDtypeStruct((nb, N_KV, n_pages, LANES), BF16),
        compiler_params=_cparams(1, 48),
        name="compress",
    )(page_table, src, pos_t, wd, bd2, gain128, gg)


def _softmax_cols(s):
    m = jnp.max(s, axis=0, keepdims=True)
    m = jnp.where(m == NEG_INF, 0.0, m)
    e = jnp.exp2(s - m)
    return e, jnp.sum(e, axis=0, keepdims=True)


def _select_bias(imp, cur):
    jj = lax.broadcasted_iota(jnp.int32, imp.shape, 0)
    forced = (jj == 0) | (jj == cur) | (jj == cur - 1)
    cand = (jj <= cur) & jnp.logical_not(forced)
    v = jnp.where(cand, imp, -1.0)
    for _ in range(TOP_N - N_FORCED):
        m = jnp.max(v, axis=0, keepdims=True)
        idx = jnp.min(jnp.where(v == m, jj, imp.shape[0]), axis=0, keepdims=True)
        idx = jnp.where(m >= 0.0, idx, -1)
        v = jnp.where(jj == idx, -1.0, v)
    picked = cand & (v < 0.0)
    return jnp.where(forced | picked, 0.0, NEG_INF)


def _attn_prompt_body(qn_ref, qr_ref, gt_ref, kn_ref, kc_ref, vct_ref, ks_ref, blk_ref, vst_ref, kw_ref, vwt_ref,
                      o_ref, qsel_ref):
    i = pl.program_id(2)
    q_start = i * QBLK
    n_col = HPG * QBLK
    q_pos = q_start + lax.broadcasted_iota(jnp.int32, (1, n_col), 1) % QBLK

    in_half = lax.broadcasted_iota(jnp.int32, (LANES, 1), 0) // HEAD_DIM == pl.program_id(1) % 2
    place = lambda q64: jnp.where(in_half, jnp.concatenate([q64, q64], axis=0), jnp.zeros((), BF16))
    heads_on_lanes = lambda ref: jnp.concatenate(
        [ref[hh * HEAD_DIM:(hh + 1) * HEAD_DIM, :] for hh in range(HPG)], axis=1)
    qn = place(heads_on_lanes(qn_ref))
    qr = place(heads_on_lanes(qr_ref))

    n_cmp = kc_ref.shape[0]
    s = _dot(kc_ref[...], qn)
    cmp_end = lax.broadcasted_iota(jnp.int32, (n_cmp, 1), 0) * CMP_BLK + (CMP_BLK - 1)
    s = jnp.where(cmp_end <= q_pos, s, NEG_INF)
    e, l = _softmax_cols(s)
    p_cmp = e / jnp.maximum(l, 1e-30)
    o_cmp = _dot(vct_ref[...], p_cmp.astype(BF16))

    imp = p_cmp[:, 0:QBLK]
    for hh in range(1, HPG):
        imp = imp + p_cmp[:, hh * QBLK:(hh + 1) * QBLK]
    bias = jnp.maximum(_select_bias(imp, q_pos[:, 0:QBLK] // SEL_BLK), MASKED)
    bias = jnp.concatenate([bias] * HPG, axis=1)
    if n_cmp < LANES:
        bias = jnp.concatenate([bias, jnp.zeros((LANES - n_cmp, n_col), F32)], axis=0)

    n_win = WINDOW + QBLK
    w0 = pl.multiple_of(q_start, QBLK)
    s = _dot(kw_ref[pl.ds(w0, n_win), :], qr)
    r = lax.broadcasted_iota(jnp.int32, (n_win, QBLK), 0)
    q = lax.broadcasted_iota(jnp.int32, (n_win, QBLK), 1)
    valid = (r > q) & (r <= q + WINDOW) & (r + q_start >= WINDOW)
    wbias = jnp.where(valid, 0.0, NEG_INF)
    s = s + jnp.concatenate([wbias] * HPG, axis=1)
    e, l = _softmax_cols(s)
    o_win = _dot(vwt_ref[:, pl.ds(w0, n_win)], e.astype(BF16)) / jnp.maximum(l, 1e-30)

    own = lax.broadcasted_iota(jnp.int32, (LANES, 1), 0) >= q_start // SEL_BLK
    masked = jnp.full((), MASKED, BF16)
    own_keys = jnp.concatenate([ks_ref[pl.ds(w0, QBLK), :], blk_ref[pl.ds(w0, QBLK), :]], axis=1)
    tri = jnp.where(lax.broadcasted_iota(jnp.int32, (QBLK, QBLK), 0)
                    <= lax.broadcasted_iota(jnp.int32, (QBLK, QBLK), 1), 0.0, NEG_INF)
    tri = jnp.concatenate([tri] * HPG, axis=1)
    n_tiles = (q_start + KTILE - 1) // KTILE

    def tile_scores(k0):
        keys = jnp.concatenate([ks_ref[pl.ds(k0, KTILE), :], blk_ref[pl.ds(k0, KTILE), :]], axis=1)
        return _dot(keys, qsel_ref[...])

    q32 = qr.astype(F32)
    bound = jnp.sqrt(jnp.sum(q32 * q32, axis=0, keepdims=True)) * (kn_ref[0:1, 0:1] * BOUND_SLACK)
    shifted = (bias - bound).astype(BF16)
    qsel_ref[0:LANES, :] = qr
    qsel_ref[LANES:, :] = jnp.where(own, masked, shifted)

    def values_and_ones(k0, n):
        return jnp.concatenate([vst_ref[:, pl.ds(k0, n)], jnp.ones((ONES_ROWS, n), BF16)], axis=0)

    def fast_tile(t, acc):
        k0 = pl.multiple_of(t * KTILE, KTILE)
        return acc + _dot(values_and_ones(k0, KTILE), jnp.exp2(tile_scores(k0)).astype(BF16))

    s = _dot(own_keys, jnp.concatenate([qr, shifted], axis=0)) + tri
    acc = _dot(values_and_ones(w0, QBLK), jnp.exp2(s).astype(BF16))
    acc = lax.fori_loop(0, n_tiles // 2, lambda u, a: fast_tile(2 * u + 1, fast_tile(2 * u, a)), acc)
    acc = lax.fori_loop(0, n_tiles % 2, lambda _, a: fast_tile(n_tiles - 1, a), acc)
    l_fast = acc[HEAD_DIM:HEAD_DIM + 1]

    def exact(_):
        bias16 = bias.astype(BF16)
        qsel_ref[LANES:, :] = jnp.where(own, masked, bias16)
        s = _dot(own_keys, jnp.concatenate([qr, bias16], axis=0)) + tri
        m0 = jnp.max(s, axis=0, keepdims=True)
        p = jnp.exp2(s - m0)
        init = (m0, jnp.sum(p, axis=0, keepdims=True), _dot(vst_ref[:, pl.ds(w0, QBLK)], p.astype(BF16)))

        def tile(t, carry):
            m, l, a = carry
            k0 = pl.multiple_of(t * KTILE, KTILE)
            s = tile_scores(k0)
            m_new = jnp.maximum(m, jnp.max(s, axis=0, keepdims=True))
            alpha = jnp.exp2(m - m_new)
            p = jnp.exp2(s - m_new)
            l = alpha * l + jnp.sum(p, axis=0, keepdims=True)
            a = alpha * a + _dot(vst_ref[:, pl.ds(k0, KTILE)], p.astype(BF16))
            return m_new, l, a

        _, l, a = lax.fori_loop(0, n_tiles, tile, init)
        return a / l

    o_sel = lax.cond(jnp.min(l_fast) >= MIN_DENOM, lambda _: acc[:HEAD_DIM] / l_fast, exact, 0)

    def gate(branch):
        first = branch * N_KV * HPG + pl.program_id(1) * HPG
        return jnp.concatenate([gt_ref[pl.ds(first + hh, 1), :] for hh in range(HPG)], axis=1)

    o = gate(0) * o_cmp + gate(1) * o_sel + gate(2) * o_win
    o_ref[...] = jnp.concatenate([o[:, hh * QBLK:(hh + 1) * QBLK] for hh in range(HPG)], axis=0).astype(BF16)


def _attn_prompt(qn_t, qr_t, gt_t, k_norm, kc, vc_t, ks, blk_onehot, vs_t, kw_pad, vw_pad_t):
    nb, d, t = qn_t.shape
    ng = d // (HPG * HEAD_DIM)
    n_cmp = kc.shape[1]
    assert n_cmp <= LANES
    per_q = pl.BlockSpec((None, HPG * HEAD_DIM, QBLK), lambda b, g, i: (b, g, i))
    pair = lambda a: pl.BlockSpec((None, a.shape[1], LANES), lambda b, g, i: (b, 0, g // 2))
    group_rows = lambda a: pl.BlockSpec((None, HEAD_DIM, a.shape[2]), lambda b, g, i: (b, g, 0))
    return pl.pallas_call(
        _attn_prompt_body,
        grid=(nb, ng, t // QBLK),
        in_specs=[per_q, per_q, pl.BlockSpec((None, LANES, QBLK), lambda b, g, i: (b, 0, i)),
                  pl.BlockSpec((None, None, 8, LANES), lambda b, g, i: (b, g, 0, 0)),
                  pair(kc), group_rows(vc_t), pair(ks),
                  pl.BlockSpec(blk_onehot.shape, lambda b, g, i: (0, 0)), group_rows(vs_t),
                  pair(kw_pad), group_rows(vw_pad_t)],
        out_specs=per_q,
        out_shape=jax.ShapeDtypeStruct((nb, d, t), BF16),
        scratch_shapes=[pltpu.VMEM((2 * LANES, HPG * QBLK), BF16)],
        compiler_params=_cparams(3, 48),
        name="attn_prompt",
    )(qn_t, qr_t, gt_t, k_norm, kc, vc_t, ks, blk_onehot, vs_t, kw_pad, vw_pad_t)


def _attn_sample_body(pt_ref, qn_ref, qr_ref, gt_ref, kc_ref, vc_ref, kwin_ref, vwin_ref,
                      ksn_ref, vsn_ref, kwn_ref, vwn_ref, spread_ref, kpool_ref, vpool_ref, o_ref,
                      kbuf_ref, vbuf_ref, ksem, vsem, s_ref, *, n_pages, n_new, past_len):
    b = pl.program_id(0)
    nb = pl.num_programs(0)
    slot = b % 2

    @pl.when(b == 0)
    def _():
        _gather_start(pt_ref, kpool_ref, kbuf_ref, ksem, 0, 0, n_pages, 1, True)
        _gather_start(pt_ref, vpool_ref, vbuf_ref, vsem, 0, 0, n_pages, 1, True)

    @pl.when(b + 1 < nb)
    def _():
        _gather_start(pt_ref, kpool_ref, kbuf_ref, ksem, b + 1, 1 - slot, n_pages, 1, True)
        _gather_start(pt_ref, vpool_ref, vbuf_ref, vsem, b + 1, 1 - slot, n_pages, 1, True)

    qn = qn_ref[0]
    qr = qr_ref[0]
    n_col = qr.shape[0]
    n_past = n_pages * PAGE_SIZE
    n_blk = n_past // SEL_BLK
    n_gq = N_KV * n_new
    dot_nt = lambda a, bt: lax.dot_general(a, bt, (((1,), (1,)), ((), ())), preferred_element_type=F32)

    lane = lax.broadcasted_iota(jnp.int32, (1, LANES), 1)
    lane_pos = past_len + lane % n_new
    s = _dot(kc_ref[0], qn)
    cmp_end = lax.broadcasted_iota(jnp.int32, (n_blk, 1), 0) * CMP_BLK + (CMP_BLK - 1)
    s = jnp.where(cmp_end <= lane_pos, s, NEG_INF)
    e, l = _softmax_cols(s)
    p_cmp = e / jnp.maximum(l, 1e-30)
    o_cmp = _dot(p_cmp.T.astype(BF16), vc_ref[0])[:n_col]

    imp = p_cmp
    for hh in range(1, HPG):
        imp = imp + pltpu.roll(p_cmp, LANES - hh * n_gq, 1)
    sel = _select_bias(imp, lane_pos // SEL_BLK)
    sel01 = jnp.where((lane < n_gq) & (sel == 0.0), 1.0, 0.0)
    spread = sel01
    for hh in range(1, HPG):
        spread = spread + pltpu.roll(sel01, hh * n_gq, 1)
    sel_bias = jnp.where(spread.T[:n_col] > 0.5, 0.0, MASKED).astype(BF16)

    row = lax.broadcasted_iota(jnp.int32, (n_col, 1), 0)
    q_idx = row % n_new
    q_pos = past_len + q_idx
    n_pad = ksn_ref.shape[1]
    new_idx = lax.broadcasted_iota(jnp.int32, (1, n_pad), 1)
    new_ok = (new_idx <= q_idx) & (new_idx < n_new)

    _gather_wait(kpool_ref, kbuf_ref, ksem, slot, n_pages, 1, True)
    _gather_wait(vpool_ref, vbuf_ref, vsem, slot, n_pages, 1, True)
    pages_per_chunk = 8
    chunk = pages_per_chunk * PAGE_SIZE

    def page_rows(c, pp):
        return pl.ds(pl.multiple_of((c * pages_per_chunk + pp) * KV_W, KV_W), KV_W)

    def score_chunk(c, m):
        k0 = pl.multiple_of(c * chunk, chunk)
        s = jnp.concatenate([_dot(qr, kbuf_ref[slot, page_rows(c, pp), :].astype(BF16))
                             for pp in range(pages_per_chunk)], axis=1)
        s = s + _dot(sel_bias, spread_ref[:, pl.ds(k0, chunk)])
        s_ref[:, pl.ds(k0, chunk)] = s
        return jnp.maximum(m, jnp.max(s, axis=1, keepdims=True))

    m = lax.fori_loop(0, n_pages // pages_per_chunk, score_chunk, jnp.full((n_col, 1), NEG_INF, F32))
    s_new = jnp.where(new_ok, dot_nt(qr, ksn_ref[0].astype(BF16)), NEG_INF)
    m = jnp.maximum(m, jnp.max(s_new, axis=1, keepdims=True))
    m = jnp.where(m == NEG_INF, 0.0, m)

    def value_chunk(c, carry):
        l, acc = carry
        k0 = pl.multiple_of(c * chunk, chunk)
        p = jnp.exp2(s_ref[:, pl.ds(k0, chunk)] - m)
        l = l + jnp.sum(p, axis=1, keepdims=True)
        p = p.astype(BF16)
        for pp in range(pages_per_chunk):
            acc = acc + dot_nt(p[:, pp * PAGE_SIZE:(pp + 1) * PAGE_SIZE],
                               vbuf_ref[slot, page_rows(c, pp), :].astype(BF16))
        return l, acc

    p_new = jnp.exp2(s_new - m)
    l_sel, acc = lax.fori_loop(
        0, n_pages // pages_per_chunk, value_chunk,
        (jnp.sum(p_new, axis=1, keepdims=True), _dot(p_new.astype(BF16), vsn_ref[0].astype(BF16))))
    o_sel = acc / jnp.maximum(l_sel, 1e-30)

    n_buf = kwin_ref.shape[2]
    s_buf = _dot(qr, kwin_ref[0].astype(BF16))
    buf_pos = past_len - n_buf + lax.broadcasted_iota(jnp.int32, (1, n_buf), 1)
    diff = q_pos - buf_pos
    s_buf = jnp.where((diff >= 0) & (diff < WINDOW) & (buf_pos >= 0), s_buf, NEG_INF)
    s_nw = jnp.where(new_ok, dot_nt(qr, kwn_ref[0].astype(BF16)), NEG_INF)
    m = jnp.maximum(jnp.max(s_buf, axis=1, keepdims=True), jnp.max(s_nw, axis=1, keepdims=True))
    m = jnp.where(m == NEG_INF, 0.0, m)
    p_buf = jnp.exp2(s_buf - m)
    p_nw = jnp.exp2(s_nw - m)
    l_win = jnp.sum(p_buf, axis=1, keepdims=True) + jnp.sum(p_nw, axis=1, keepdims=True)
    o_win = (dot_nt(p_buf.astype(BF16), vwin_ref[0].astype(BF16))
             + _dot(p_nw.astype(BF16), vwn_ref[0].astype(BF16))) / jnp.maximum(l_win, 1e-30)

    gt = gt_ref[0]
    o = gt[:, 0:1] * o_cmp + gt[:, 1:2] * o_sel + gt[:, 2:3] * o_win
    row_g = (lax.broadcasted_iota(jnp.int32, (n_col, KV_W), 0) // n_new) % N_KV
    lane_g = lax.broadcasted_iota(jnp.int32, (n_col, KV_W), 1) // HEAD_DIM
    o = jnp.where(row_g == lane_g, o, 0.0)
    z = o[:, :LANES] + o[:, LANES:]
    o_ref[0] = z + pltpu.roll(z, HEAD_DIM, 1)


def _attn_sample(page_table, qn_bd, qr_bd_t, gt_col, kc, vc, kwin_t, vwin_t, ks_new, vs_new, kw_new, vw_new,
                 blk_spread, k_pool, v_pool, n_new, past_len):
    nb, n_pages = page_table.shape
    n_past = n_pages * PAGE_SIZE
    n_col = qr_bd_t.shape[1]
    per_b = lambda a: pl.BlockSpec((1,) + a.shape[1:], lambda b, pt: (b,) + (0,) * (a.ndim - 1))
    any_spec = pl.BlockSpec(memory_space=pl.ANY)
    ins = (qn_bd, qr_bd_t, gt_col, kc, vc, kwin_t, vwin_t, ks_new, vs_new, kw_new, vw_new)
    grid_spec = pltpu.PrefetchScalarGridSpec(
        num_scalar_prefetch=1,
        grid=(nb,),
        in_specs=[per_b(a) for a in ins]
        + [pl.BlockSpec(blk_spread.shape, lambda b, pt: (0, 0)), any_spec, any_spec],
        out_specs=pl.BlockSpec((1, n_col, LANES), lambda b, pt: (b, 0, 0)),
        scratch_shapes=[
            pltpu.VMEM((2, n_pages * KV_W, PAGE_SIZE), F32),
            pltpu.VMEM((2, n_pages * KV_W, PAGE_SIZE), F32),
            pltpu.SemaphoreType.DMA((2,)),
            pltpu.SemaphoreType.DMA((2,)),
            pltpu.VMEM((n_col, n_past), F32),
        ],
    )
    return pl.pallas_call(
        functools.partial(_attn_sample_body, n_pages=n_pages, n_new=n_new, past_len=past_len),
        grid_spec=grid_spec,
        out_shape=jax.ShapeDtypeStruct((nb, n_col, LANES), F32),
        compiler_params=_cparams(1, 56),
        name="attn_sample",
    )(page_table, *ins, blk_spread, k_pool, v_pool)


def _rope_tables(pos):
    half = HEAD_DIM // 2
    inv = ROPE_THETA ** (-jnp.arange(half, dtype=F32) / half)
    ang = pos.astype(F32)[:, None] * inv
    cos, sin = jnp.cos(ang), jnp.sin(ang)
    reps = LANES // HEAD_DIM
    return (jnp.tile(jnp.concatenate([cos, cos], axis=1), (1, reps)),
            jnp.tile(jnp.concatenate([-sin, sin], axis=1), (1, reps)))


def _group_sum_matrix():
    lane = jnp.arange(LANES)
    g = (lane[:, None] // HEAD_DIM == lane[None, :] // HEAD_DIM).astype(BF16)
    return jnp.concatenate([g, g], axis=0)


def _compress_weights(cmp_pos, cmp_w1, cmp_w2):
    n_pair = PAGE_SIZE // CMP_BLK
    eye = jnp.eye(n_pair, dtype=F32)
    hid = cmp_w1.shape[1]
    w1 = cmp_w1.reshape(CMP_BLK, HEAD_DIM, hid)
    wd = jnp.einsum("bk,rdh->dbrkh", eye, w1).reshape(HEAD_DIM, PAGE_SIZE, n_pair * hid).astype(BF16)
    bd2 = jnp.einsum("bk,hd->bhkd", eye, cmp_w2).reshape(n_pair * hid, n_pair * HEAD_DIM).astype(BF16)
    return jnp.tile(cmp_pos.T, (1, n_pair)), wd, bd2


def _to_rows_major(a_t):
    nb, _, rows = a_t.shape
    return a_t.reshape(nb, N_KV, HEAD_DIM, rows).transpose(0, 3, 1, 2)[None]


def _from_rows_major(a):
    n, rows = a.shape[:2]
    return a.transpose(0, 2, 3, 1).reshape(n, KV_W, rows)


def _layer_tail(x, p, layer, norm_g, w_gu, w_down, w_ple_proj, w_ple_gate, mixer_out=None):
    return _ffn(x, norm_g[layer, 2], w_gu, w_down, (layer, 1), mixer_out=mixer_out,
                ple=(norm_g[layer, 3], p, w_ple_gate, w_ple_proj))


def kernel(x_prompt, x_sample, state_conv, cache_k_cmp, cache_v_cmp, cache_k_sel, cache_v_sel,
           cache_k_win, cache_v_win, page_table, p_prompt, p_sample, norm_g, ffn_w_gu, ffn_w_down,
           ple_w_proj, ple_w_gate, conv_w_in, conv_w, conv_w_out, nsa_w_in, nsa_qk_g, nsa_cmp_pos,
           nsa_cmp_w1, nsa_cmp_w2, nsa_w_out):
    nb_p, t_p, d = x_prompt.shape
    nb_s, t_s, _ = x_sample.shape
    ple_dim = p_prompt.shape[-1]
    past_len = page_table.shape[1] * PAGE_SIZE
    assert t_p % KTILE == 0 and t_p % PAGE_SIZE == 0 and d % LANES == 0

    w_gu = ffn_w_gu.astype(BF16)
    w_down = ffn_w_down.astype(BF16)
    w_pp = ple_w_proj.astype(BF16)
    w_pg = ple_w_gate.astype(BF16)

    xp = x_prompt.reshape(nb_p * t_p, d)
    xs = x_sample.reshape(nb_s * t_s, d)
    pp = p_prompt.reshape(-1, nb_p * t_p, ple_dim)
    ps = p_sample.reshape(-1, nb_s * t_s, ple_dim)

    lw = (norm_g, w_gu, w_down, w_pp, w_pg)
    cw = (conv_w_in[0].astype(BF16), conv_w[0], conv_w_out[0].astype(BF16))
    xp = _ffn(xp, norm_g[0, 0], w_gu, w_down, (0, 0))
    xs = _ffn(xs, norm_g[0, 0], w_gu, w_down, (0, 0))
    xp3, conv_p = _conv_mixer(xp.reshape(nb_p, t_p, d), jnp.zeros((nb_p, CONV_W - 1, d), F32),
                              norm_g[0, 1], *cw, shift=1)
    xs_tb = xs.reshape(nb_s, t_s, d).transpose(1, 0, 2).reshape(1, t_s * nb_s, d)
    prev_tb = state_conv[0].transpose(1, 0, 2).reshape(1, (CONV_W - 1) * nb_s, d)
    xs3, conv_s = _conv_mixer(xs_tb, prev_tb, norm_g[0, 1], *cw, shift=nb_s)
    xs = xs3.reshape(t_s, nb_s, d).transpose(1, 0, 2).reshape(nb_s * t_s, d)
    conv_s = conv_s.reshape(CONV_W - 1, nb_s, d).transpose(1, 0, 2)
    xp = _layer_tail(xp3.reshape(nb_p * t_p, d), pp[0], 0, *lw)
    xs = _layer_tail(xs, ps[0], 0, *lw)

    xp = _ffn(xp, norm_g[1, 0], w_gu, w_down, (1, 0))
    xs = _ffn(xs, norm_g[1, 0], w_gu, w_down, (1, 0))

    n_in = nsa_w_in.shape[-1]
    n_in_pad = -(-n_in // LANES) * LANES
    n_gate = N_KV * HPG * 3
    gate_cols = nsa_w_in[0][:, n_in - n_gate:].reshape(d, N_KV * HPG, 3).transpose(0, 2, 1).reshape(d, n_gate)
    w_in = jnp.concatenate([nsa_w_in[0][:, :n_in - n_gate], gate_cols], axis=1)
    w_in = jnp.pad(w_in, ((0, 0), (0, n_in_pad - n_in))).astype(BF16)
    w_out = nsa_w_out[0].astype(BF16)
    qkg128 = jnp.tile(nsa_qk_g[0], (1, LANES // HEAD_DIM))
    gg = _group_sum_matrix()
    cos_p, sin_p = _rope_tables(jnp.arange(t_p))
    cos_s, sin_s = _rope_tables(past_len + jnp.arange(t_s))
    pos_k, wd_k, bd2_k = _compress_weights(nsa_cmp_pos[0, 0], nsa_cmp_w1[0, 0], nsa_cmp_w2[0, 0])
    pos_v, wd_v, bd2_v = _compress_weights(nsa_cmp_pos[0, 1], nsa_cmp_w1[0, 1], nsa_cmp_w2[0, 1])
    k_gain = qkg128[1:2]

    qn, qr, kc_t, vc_t, ks_t, vs_t, kw_t, vw_t, gt, ks_b, kw_b, vs_tb, vw_tb, k_norm2 = _nsa_proj(
        xp, norm_g[1, 1], w_in, qkg128, jnp.tile(cos_p, (nb_p, 1)), jnp.tile(sin_p, (nb_p, 1)), gg, n_batch=nb_p)
    pt_p = jnp.zeros((nb_p, t_p // PAGE_SIZE), jnp.int32)
    as_blocks = lambda c: c.reshape(c.shape[0], N_KV, -1, HEAD_DIM)
    by_block = lambda c: as_blocks(c).transpose(0, 2, 1, 3).reshape(c.shape[0], -1, KV_W)
    kc = _compress(pt_p, kc_t, pos_k, wd_k, bd2_k, k_gain, gg, True, False, nb_p)
    vc = _compress(pt_p, vc_t, pos_v, wd_v, bd2_v, k_gain, gg, False, False, nb_p)

    blk_onehot = (jnp.arange(t_p)[:, None] // SEL_BLK == jnp.arange(LANES)[None, :]).astype(BF16)
    k_norm = jnp.sqrt(k_norm2.reshape(nb_p, -1, N_KV, HEAD_DIM).max(axis=(1, 3)))
    k_norm = jnp.broadcast_to(k_norm[:, :, None, None], (nb_p, N_KV, 8, LANES))
    o_t = _attn_prompt(
        qn, qr, gt, k_norm,
        by_block(kc), as_blocks(vc).transpose(0, 1, 3, 2).reshape(nb_p, KV_W, -1),
        ks_b.reshape(nb_p, t_p, KV_W), blk_onehot, vs_tb,
        jnp.pad(kw_b.reshape(nb_p, t_p, KV_W), ((0, 0), (WINDOW, 0), (0, 0))),
        jnp.pad(vw_tb, ((0, 0), (0, 0), (WINDOW, 0))))
    keep = min(WINDOW, t_p)
    nsa_p = tuple(_to_rows_major(a) for a in (kc_t, vc_t, ks_t, vs_t, kw_t[:, :, -keep:], vw_t[:, :, -keep:]))

    qn, qr, kc_r, vc_r, ks, vs, kw, vw, gt = _nsa_proj(
        xs, norm_g[1, 1], w_in, qkg128, jnp.tile(cos_s, (nb_s, 1)), jnp.tile(sin_s, (nb_s, 1)), gg)
    pool_t = lambda c: _from_rows_major(c[0])
    n_seq = 2 if nb_s % 2 == 0 else 1
    kc = by_block(_compress(page_table, pool_t(cache_k_cmp), pos_k, wd_k, bd2_k, k_gain, gg, True, True, n_seq))
    vc = by_block(_compress(page_table, pool_t(cache_v_cmp), pos_v, wd_v, bd2_v, k_gain, gg, False, True, n_seq))

    n_used = HPG * N_KV * t_s

    def q_block_diag(a):
        a = a.reshape(nb_s, t_s, N_KV, HPG, HEAD_DIM).astype(F32)
        return jnp.einsum("bqghd,gk->bgdhkq", a, jnp.eye(N_KV, dtype=F32)).reshape(nb_s, KV_W, n_used)

    qn_bd = jnp.pad(q_block_diag(qn), ((0, 0), (0, 0), (0, LANES - n_used))).astype(BF16)
    qr_bd_t = q_block_diag(qr).transpose(0, 2, 1).astype(BF16)
    gt_col = gt[:, :n_gate].reshape(nb_s, t_s, 3, N_KV, HPG).transpose(0, 4, 3, 1, 2)
    gt_col = jnp.pad(gt_col.reshape(nb_s, n_used, 3), ((0, 0), (0, 0), (0, LANES - 3)))
    new_rows = lambda a: jnp.pad(a.reshape(nb_s, t_s, KV_W), ((0, 0), (0, 16 - t_s), (0, 0)))
    n_buf = cache_k_win.shape[2]
    kwin_t, vwin_t = _from_rows_major(cache_k_win[0]), _from_rows_major(cache_v_win[0])
    blk_spread = (jnp.arange(past_len // SEL_BLK)[:, None] == jnp.arange(past_len)[None, :] // SEL_BLK).astype(BF16)
    o_s = _attn_sample(
        page_table, qn_bd, qr_bd_t, gt_col, kc, vc, kwin_t, vwin_t,
        new_rows(ks), new_rows(vs), new_rows(kw), new_rows(vw), blk_spread,
        pool_t(cache_k_sel), pool_t(cache_v_sel), t_s, past_len)
    o = o_s[:, :, :HEAD_DIM].reshape(nb_s, HPG, N_KV, t_s, HEAD_DIM).transpose(0, 3, 2, 1, 4)
    o_s = o.reshape(nb_s * t_s, d).astype(BF16)
    st = lambda a: a.reshape(1, nb_s, t_s, N_KV, HEAD_DIM)
    new_t = lambda a: a.reshape(nb_s, t_s, KV_W).transpose(0, 2, 1)
    win = lambda buf_t, new: _to_rows_major(jnp.concatenate([buf_t, new_t(new)], axis=2)[:, :, -n_buf:])
    nsa_s = (st(kc_r), st(vc_r), st(ks), st(vs), win(kwin_t, kw), win(vwin_t, vw))

    xp = _layer_tail(xp, pp[1], 1, *lw, mixer_out=(o_t, w_out))
    xs = _layer_tail(xs, ps[1], 1, *lw, mixer_out=(o_s, w_out))

    return (xp.reshape(nb_p, t_p, d), xs.reshape(nb_s, t_s, d), conv_p[None], conv_s[None]) + nsa_p + nsa_s
```

```python
import functools

import jax
import jax.numpy as jnp
from jax import lax
from jax.experimental import pallas as pl
from jax.experimental.pallas import tpu as pltpu

F32 = jnp.float32
BF16 = jnp.bfloat16

HEAD_DIM = 64
N_KV = 4
HPG = 4
CMP_BLK = 64
SEL_BLK = 64
TOP_N = 16
N_FORCED = 3
WINDOW = 512
PAGE_SIZE = 128
CONV_W = 3
QBLK = 256
KTILE = 1024
ROPE_THETA = 10000.0
RMS_EPS = 1e-6
NEG_INF = float("-inf")
MASKED = -1e30
LOG2E = 1.4426950408889634
BOUND_SLACK = 1.02
MIN_DENOM = 2.0 ** -100
ONES_ROWS = 16
KV_W = N_KV * HEAD_DIM
ROW_TILE = 512
LANES = 128
MIB = 1024 * 1024


def _cparams(n_grid, vmem_mib):
    return pltpu.CompilerParams(
        dimension_semantics=("arbitrary",) * n_grid,
        vmem_limit_bytes=vmem_mib * MIB,
    )


def _rms(x, g):
    return x * lax.rsqrt(jnp.mean(x * x, axis=-1, keepdims=True) + RMS_EPS) * g


def _dot(a, b):
    return jnp.dot(a, b, preferred_element_type=F32)


def _ffn_body(x_ref, g_ref, wgu_ref, wd_ref, *rest, f, mixer, with_ple):
    x = x_ref[...]
    if mixer is not None:
        o_ref, wo_ref, *rest = rest
        if mixer == "rows_last":
            x = x + lax.dot_general(o_ref[...], wo_ref[...], (((0,), (0,)), ((), ())),
                                    preferred_element_type=F32)
        else:
            x = x + _dot(o_ref[...], wo_ref[...])
    h = _rms(x, g_ref[...]).astype(BF16)
    gate_up = _dot(h, wgu_ref[...])
    act = (jax.nn.silu(gate_up[:, :f]) * gate_up[:, f:]).astype(BF16)
    y = x + 0.5 * _dot(act, wd_ref[...])
    if with_ple:
        g2_ref, p_ref, wg_ref, wp_ref, out_ref = rest
        gate = jax.nn.sigmoid(_dot(_rms(y, g2_ref[...]).astype(BF16), wg_ref[...]))
        y = y + gate * _dot(p_ref[...].astype(BF16), wp_ref[...])
    else:
        out_ref, = rest
    out_ref[...] = y


def _ffn(x, g, w_gu, w_down, at, mixer_out=None, ple=None):
    m, d = x.shape
    f = w_down.shape[-2]
    tm = min(m, ROW_TILE)
    row = lambda w: pl.BlockSpec((tm, w), lambda i: (i, 0))
    vec = pl.BlockSpec((1, d), lambda i: (0, 0))

    def resident(a, lead=()):
        zeros = (0,) * (a.ndim - len(lead))
        return pl.BlockSpec((None,) * len(lead) + a.shape[len(lead):], lambda i: lead + zeros,
                            pipeline_mode=pl.Buffered(1))

    args = [x, g.reshape(1, d), w_gu, w_down]
    in_specs = [row(d), vec, resident(w_gu, at), resident(w_down, at)]
    mixer = None
    if mixer_out is not None:
        o, w_o = mixer_out
        if o.ndim == 3:
            mixer = "rows_last"
            tiles = o.shape[2] // tm
            o_spec = pl.BlockSpec((None, o.shape[1], tm), lambda i: (i // tiles, 0, i % tiles))
        else:
            mixer = "rows_first"
            o_spec = row(o.shape[1])
        args += [o, w_o]
        in_specs += [o_spec, resident(w_o)]
    if ple is not None:
        g2, p, w_gate, w_proj = ple
        args += [g2.reshape(1, d), p, w_gate, w_proj]
        in_specs += [vec, row(p.shape[1]), resident(w_gate, at[:1]), resident(w_proj, at[:1])]
    return pl.pallas_call(
        functools.partial(_ffn_body, f=f, mixer=mixer, with_ple=ple is not None),
        grid=(m // tm,),
        in_specs=in_specs,
        out_specs=row(d),
        out_shape=jax.ShapeDtypeStruct((m, d), F32),
        compiler_params=_cparams(1, 56),
        name="ffn",
    )(*args)


def _conv_body(x_ref, prev_ref, g_ref, win_ref, wc_ref, wout_ref, y_ref, st_ref, u_ref,
               *, shift, halo, tm, d):
    t = pl.program_id(1)
    lo = halo - 2 * shift

    @pl.when(t == 0)
    def _():
        u_ref[lo:halo, :] = prev_ref[0]

    x = x_ref[0]
    h = _rms(x, g_ref[...]).astype(BF16)
    proj = _dot(h, win_ref[...])
    b_gate = proj[:, :d]
    u = proj[:, d:2 * d] * proj[:, 2 * d:]
    u_ref[halo:halo + tm, :] = u
    wc = wc_ref[...]
    y = (wc[0:1, :] * u_ref[lo:lo + tm, :]
         + wc[1:2, :] * u_ref[lo + shift:lo + shift + tm, :]
         + wc[2:3, :] * u)
    y_ref[0] = x + _dot((b_gate * y).astype(BF16), wout_ref[...])
    tail = u_ref[lo + tm:halo + tm, :]
    st_ref[0] = tail
    u_ref[lo:halo, :] = tail


def _conv_mixer(x, prev, g, w_in, w_conv, w_out, shift):
    nb, t, d = x.shape
    tm = min(t, ROW_TILE)
    halo = -(-2 * shift // 8) * 8
    return pl.pallas_call(
        functools.partial(_conv_body, shift=shift, halo=halo, tm=tm, d=d),
        grid=(nb, t // tm),
        in_specs=[
            pl.BlockSpec((1, tm, d), lambda b, i: (b, i, 0)),
            pl.BlockSpec((1, 2 * shift, d), lambda b, i: (b, 0, 0)),
            pl.BlockSpec((1, d), lambda b, i: (0, 0)),
            pl.BlockSpec((d, 3 * d), lambda b, i: (0, 0)),
            pl.BlockSpec((CONV_W, d), lambda b, i: (0, 0)),
            pl.BlockSpec((d, d), lambda b, i: (0, 0)),
        ],
        out_specs=[
            pl.BlockSpec((1, tm, d), lambda b, i: (b, i, 0)),
            pl.BlockSpec((1, 2 * shift, d), lambda b, i: (b, 0, 0)),
        ],
        out_shape=[
            jax.ShapeDtypeStruct((nb, t, d), F32),
            jax.ShapeDtypeStruct((nb, 2 * shift, d), F32),
        ],
        scratch_shapes=[pltpu.VMEM((halo + tm, d), F32)],
        compiler_params=_cparams(2, 48),
        name="conv_mixer",
    )(x, prev, g.reshape(1, d), w_in, w_conv, w_out)


def _head_norm(xc, gain, gg):
    sq = xc * xc
    hi = sq.astype(BF16)
    lo = (sq - hi.astype(F32)).astype(BF16)
    ss = _dot(jnp.concatenate([hi, lo], axis=1), gg)
    return xc * lax.rsqrt(ss * (1.0 / HEAD_DIM) + RMS_EPS) * gain


def _nsa_proj_body(x_ref, g_ref, w_ref, qkg_ref, cos_ref, sin_ref, gg_ref,
                   qn_ref, qr_ref, kc_ref, vc_ref, ks_ref, vs_ref, kw_ref, vw_ref, gt_ref, *extra_refs,
                   tm, d, rows_last):
    h = _rms(x_ref[...], g_ref[...]).astype(BF16)
    proj = _dot(h, w_ref[...])
    cos = cos_ref[...]
    sin = sin_ref[...]
    gg = gg_ref[...]
    lane = lax.broadcasted_iota(jnp.int32, (tm, LANES), 1)
    first_half = (lane % HEAD_DIM) < (HEAD_DIM // 2)
    scale = HEAD_DIM ** -0.5 * LOG2E

    def rope(xc):
        swapped = jnp.where(first_half, pltpu.roll(xc, LANES - HEAD_DIM // 2, 1),
                            pltpu.roll(xc, HEAD_DIM // 2, 1))
        return xc * cos + swapped * sin

    def slab(c):
        return proj[:, c * LANES:(c + 1) * LANES]

    n_q = d // LANES
    for c in range(n_q):
        qc = _head_norm(slab(c), qkg_ref[0:1, :], gg)
        if rows_last:
            qn_ref[c * LANES:(c + 1) * LANES, :] = (qc * scale).T.astype(BF16)
            qr_ref[c * LANES:(c + 1) * LANES, :] = (rope(qc) * scale).T.astype(BF16)
        else:
            qn_ref[:, c * LANES:(c + 1) * LANES] = (qc * scale).astype(BF16)
            qr_ref[:, c * LANES:(c + 1) * LANES] = (rope(qc) * scale).astype(BF16)
    def put(ref, c, val):
        if rows_last:
            ref[c * LANES:(c + 1) * LANES, :] = val
        else:
            ref[:, c * LANES:(c + 1) * LANES] = val

    for c in range(2):
        lanes = slice(c * LANES, (c + 1) * LANES)
        k_sel = rope(_head_norm(slab(n_q + 4 + c), qkg_ref[2:3, :], gg))
        k_win = rope(_head_norm(slab(n_q + 8 + c), qkg_ref[3:4, :], gg))
        vals = (slab(n_q + c), slab(n_q + 2 + c), k_sel, slab(n_q + 6 + c), k_win, slab(n_q + 10 + c))
        if rows_last:
            vals = tuple(v.T for v in vals)
            ksb_ref, kwb_ref, vsb_ref, vwb_ref, kn_ref = extra_refs
            k16 = k_sel.astype(BF16)
            ksb_ref[:, lanes] = k16
            k32 = k16.astype(F32)
            sq = k32 * k32
            hi = sq.astype(BF16)
            norm2 = _dot(jnp.concatenate([hi, (sq - hi.astype(F32)).astype(BF16)], axis=1), gg)
            kn_ref[:, lanes] = jnp.broadcast_to(jnp.max(norm2, axis=0, keepdims=True), (8, LANES))
            kwb_ref[:, lanes] = k_win.astype(BF16)
            vsb_ref[lanes, :] = vals[3].astype(BF16)
            vwb_ref[lanes, :] = vals[5].astype(BF16)
        for ref, v in zip((kc_ref, vc_ref, ks_ref, vs_ref, kw_ref, vw_ref), vals):
            put(ref, c, v)
    gates = jax.nn.sigmoid(slab(n_q + 12))
    gt_ref[...] = gates.T if rows_last else gates


def _nsa_proj(x, g, w_in_pad, qkg128, cos128, sin128, gg, n_batch=None):
    m, d = x.shape
    n = w_in_pad.shape[1]
    tm = min(m, ROW_TILE)
    row = lambda w: pl.BlockSpec((tm, w), lambda i: (i, 0))
    full = lambda a: pl.BlockSpec(a.shape, lambda i: (0,) * a.ndim)
    rows_last = n_batch is not None
    if rows_last:
        t = m // n_batch
        tiles = t // tm
        t_spec = lambda w: pl.BlockSpec((None, w, tm), lambda i: (i // tiles, 0, i % tiles))
        t_shape = lambda w, dt: jax.ShapeDtypeStruct((n_batch, w, t), dt)
        kv_spec, q_spec, gt_spec = t_spec(KV_W), t_spec(d), t_spec(LANES)
        kv_shape = lambda dt: t_shape(KV_W, dt)
        q_shape, gt_shape = t_shape(d, BF16), t_shape(LANES, F32)
        extra_specs = [row(KV_W), row(KV_W), kv_spec, kv_spec, pl.BlockSpec((8, KV_W), lambda i: (i, 0))]
        extra_shapes = ([jax.ShapeDtypeStruct((m, KV_W), BF16)] * 2 + [kv_shape(BF16)] * 2
                        + [jax.ShapeDtypeStruct((m // tm * 8, KV_W), F32)])
    else:
        kv_spec, kv_shape = row(KV_W), lambda dt: jax.ShapeDtypeStruct((m, KV_W), dt)
        q_spec, gt_spec = row(d), row(LANES)
        q_shape, gt_shape = jax.ShapeDtypeStruct((m, d), BF16), jax.ShapeDtypeStruct((m, LANES), F32)
        extra_specs, extra_shapes = [], []
    return pl.pallas_call(
        functools.partial(_nsa_proj_body, tm=tm, d=d, rows_last=rows_last),
        grid=(m // tm,),
        in_specs=[row(d), pl.BlockSpec((1, d), lambda i: (0, 0)), pl.BlockSpec((d, n), lambda i: (0, 0)),
                  full(qkg128), row(LANES), row(LANES), full(gg)],
        out_specs=[q_spec, q_spec] + [kv_spec] * 6 + [gt_spec] + extra_specs,
        out_shape=[q_shape] * 2 + [kv_shape(F32)] * 6 + [gt_shape] + extra_shapes,
        compiler_params=_cparams(1, 48),
        name="nsa_proj",
    )(x, g.reshape(1, d), w_in_pad, qkg128, cos128, sin128, gg)


def _page_copy(src_ref, buf_ref, sem, slot, paged, index, p, dst_page):
    src = src_ref.at[index] if paged else src_ref.at[index, :, pl.ds(p * PAGE_SIZE, PAGE_SIZE)]
    dst = buf_ref.at[slot, :, dst_page, :] if buf_ref.ndim == 4 else buf_ref.at[slot, pl.ds(dst_page * KV_W, KV_W)]
    return pltpu.make_async_copy(src, dst, sem.at[slot])


def _gather_start(pt_ref, src_ref, buf_ref, sem, step, slot, n_pages, n_seq, paged):
    for s in range(n_seq):
        seq = step * n_seq + s
        for p in range(n_pages):
            index = pt_ref[seq, p] if paged else seq
            _page_copy(src_ref, buf_ref, sem, slot, paged, index, p, s * n_pages + p).start()


def _gather_wait(src_ref, buf_ref, sem, slot, n_pages, n_seq, paged):
    for s in range(n_seq):
        for p in range(n_pages):
            _page_copy(src_ref, buf_ref, sem, slot, paged, 0, p, s * n_pages + p).wait()


def _compress_body(pt_ref, src_ref, pos_ref, wd_ref, bd2_ref, gain_ref, gg_ref, o_ref,
                   buf_ref, sem, acc_ref, *, n_pages, n_seq, paged, normalize):
    step = pl.program_id(0)
    n_steps = pl.num_programs(0)
    slot = step % 2
    m = n_seq * n_pages
    unroll = 8

    @pl.when(step == 0)
    def _():
        _gather_start(pt_ref, src_ref, buf_ref, sem, 0, 0, n_pages, n_seq, paged)

    @pl.when(step + 1 < n_steps)
    def _():
        _gather_start(pt_ref, src_ref, buf_ref, sem, step + 1, 1 - slot, n_pages, n_seq, paged)

    _gather_wait(src_ref, buf_ref, sem, slot, n_pages, n_seq, paged)
    acc_ref[...] = jnp.zeros_like(acc_ref)

    def chunk(c, carry):
        accs = [acc_ref[g] for g in range(N_KV)]
        for dd in range(unroll):
            d = c * unroll + dd
            w = wd_ref[d]
            pos = pos_ref[pl.ds(d, 1), :]
            for g in range(N_KV):
                rows = buf_ref[slot, g * HEAD_DIM + d] + pos
                accs[g] = accs[g] + _dot(rows.astype(BF16), w)
        for g in range(N_KV):
            acc_ref[g] = accs[g]
        return carry

    lax.fori_loop(0, HEAD_DIM // unroll, chunk, 0)

    for g in range(N_KV):
        y = _dot(jax.nn.gelu(acc_ref[g]).astype(BF16), bd2_ref[...])
        if normalize:
            y = _head_norm(y, gain_ref[...], gg_ref[...])
        for s in range(n_seq):
            o_ref[s, g] = y[s * n_pages:(s + 1) * n_pages].astype(BF16)


def _compress(page_table, src, pos_t, wd, bd2, gain128, gg, normalize, paged, n_seq):
    nb, n_pages = page_table.shape
    hid2 = wd.shape[2]
    full = lambda a: pl.BlockSpec(a.shape, lambda i, pt: (0,) * a.ndim)
    grid_spec = pltpu.PrefetchScalarGridSpec(
        num_scalar_prefetch=1,
        grid=(nb // n_seq,),
        in_specs=[pl.BlockSpec(memory_space=pl.ANY), full(pos_t), full(wd), full(bd2), full(gain128), full(gg)],
        out_specs=pl.BlockSpec((n_seq, N_KV, n_pages, LANES), lambda i, pt: (i, 0, 0, 0)),
        scratch_shapes=[
            pltpu.VMEM((2, KV_W, n_seq * n_pages, PAGE_SIZE), F32),
            pltpu.SemaphoreType.DMA((2,)),
            pltpu.VMEM((N_KV, n_seq * n_pages, hid2), F32),
        ],
    )
    return pl.pallas_call(
        functools.partial(_compress_body, n_pages=n_pages, n_seq=n_seq, paged=paged, normalize=normalize),
        grid_spec=grid_spec,
        out_shape=jax.ShapeDtypeStruct((nb, N_KV, n_pages, LANES), BF16),
        compiler_params=_cparams(1, 48),
        name="compress",
    )(page_table, src, pos_t, wd, bd2, gain128, gg)


def _softmax_cols(s):
    m = jnp.max(s, axis=0, keepdims=True)
    m = jnp.where(m == NEG_INF, 0.0, m)
    e = jnp.exp2(s - m)
    return e, jnp.sum(e, axis=0, keepdims=True)


def _select_bias(imp, cur):
    jj = lax.broadcasted_iota(jnp.int32, imp.shape, 0)
    forced = (jj == 0) | (jj == cur) | (jj == cur - 1)
    cand = (jj <= cur) & jnp.logical_not(forced)
    v = jnp.where(cand, imp, -1.0)
    for _ in range(TOP_N - N_FORCED):
        m = jnp.max(v, axis=0, keepdims=True)
        idx = jnp.min(jnp.where(v == m, jj, imp.shape[0]), axis=0, keepdims=True)
        idx = jnp.where(m >= 0.0, idx, -1)
        v = jnp.where(jj == idx, -1.0, v)
    picked = cand & (v < 0.0)
    return jnp.where(forced | picked, 0.0, NEG_INF)


def _attn_prompt_body(qn_ref, qr_ref, gt_ref, kn_ref, kc_ref, vct_ref, ks_ref, blk_ref, vst_ref, kw_ref, vwt_ref,
                      o_ref, qsel_ref):
    i = pl.program_id(2)
    q_start = i * QBLK
    n_col = HPG * QBLK
    q_pos = q_start + lax.broadcasted_iota(jnp.int32, (1, n_col), 1) % QBLK

    in_half = lax.broadcasted_iota(jnp.int32, (LANES, 1), 0) // HEAD_DIM == pl.program_id(1) % 2
    place = lambda q64: jnp.where(in_half, jnp.concatenate([q64, q64], axis=0), jnp.zeros((), BF16))
    heads_on_lanes = lambda ref: jnp.concatenate(
        [ref[hh * HEAD_DIM:(hh + 1) * HEAD_DIM, :] for hh in range(HPG)], axis=1)
    qn = place(heads_on_lanes(qn_ref))
    qr = place(heads_on_lanes(qr_ref))

    n_cmp = kc_ref.shape[0]
    s = _dot(kc_ref[...], qn)
    cmp_end = lax.broadcasted_iota(jnp.int32, (n_cmp, 1), 0) * CMP_BLK + (CMP_BLK - 1)
    s = jnp.where(cmp_end <= q_pos, s, NEG_INF)
    e, l = _softmax_cols(s)
    p_cmp = e / jnp.maximum(l, 1e-30)
    o_cmp = _dot(vct_ref[...], p_cmp.astype(BF16))

    imp = p_cmp[:, 0:QBLK]
    for hh in range(1, HPG):
        imp = imp + p_cmp[:, hh * QBLK:(hh + 1) * QBLK]
    bias = jnp.maximum(_select_bias(imp, q_pos[:, 0:QBLK] // SEL_BLK), MASKED)
    bias = jnp.concatenate([bias] * HPG, axis=1)
    if n_cmp < LANES:
        bias = jnp.concatenate([bias, jnp.zeros((LANES - n_cmp, n_col), F32)], axis=0)

    n_win = WINDOW + QBLK
    w0 = pl.multiple_of(q_start, QBLK)
    s = _dot(kw_ref[pl.ds(w0, n_win), :], qr)
    r = lax.broadcasted_iota(jnp.int32, (n_win, QBLK), 0)
    q = lax.broadcasted_iota(jnp.int32, (n_win, QBLK), 1)
    valid = (r > q) & (r <= q + WINDOW) & (r + q_start >= WINDOW)
    wbias = jnp.where(valid, 0.0, NEG_INF)
    s = s + jnp.concatenate([wbias] * HPG, axis=1)
    e, l = _softmax_cols(s)
    o_win = _dot(vwt_ref[:, pl.ds(w0, n_win)], e.astype(BF16)) / jnp.maximum(l, 1e-30)

    own = lax.broadcasted_iota(jnp.int32, (LANES, 1), 0) >= q_start // SEL_BLK
    masked = jnp.full((), MASKED, BF16)
    own_keys = jnp.concatenate([ks_ref[pl.ds(w0, QBLK), :], blk_ref[pl.ds(w0, QBLK), :]], axis=1)
    tri = jnp.where(lax.broadcasted_iota(jnp.int32, (QBLK, QBLK), 0)
                    <= lax.broadcasted_iota(jnp.int32, (QBLK, QBLK), 1), 0.0, NEG_INF)
    tri = jnp.concatenate([tri] * HPG, axis=1)
    n_tiles = (q_start + KTILE - 1) // KTILE

    def tile_scores(k0):
        keys = jnp.concatenate([ks_ref[pl.ds(k0, KTILE), :], blk_ref[pl.ds(k0, KTILE), :]], axis=1)
        return _dot(keys, qsel_ref[...])

    q32 = qr.astype(F32)
    bound = jnp.sqrt(jnp.sum(q32 * q32, axis=0, keepdims=True)) * (kn_ref[0:1, 0:1] * BOUND_SLACK)
    shifted = (bias - bound).astype(BF16)
    qsel_ref[0:LANES, :] = qr
    qsel_ref[LANES:, :] = jnp.where(own, masked, shifted)

    def values_and_ones(k0, n):
        return jnp.concatenate([vst_ref[:, pl.ds(k0, n)], jnp.ones((ONES_ROWS, n), BF16)], axis=0)

    def fast_tile(t, acc):
        k0 = pl.multiple_of(t * KTILE, KTILE)
        return acc + _dot(values_and_ones(k0, KTILE), jnp.exp2(tile_scores(k0)).astype(BF16))

    s = _dot(own_keys, jnp.concatenate([qr, shifted], axis=0)) + tri
    acc = _dot(values_and_ones(w0, QBLK), jnp.exp2(s).astype(BF16))
    acc = lax.fori_loop(0, n_tiles // 2, lambda u, a: fast_tile(2 * u + 1, fast_tile(2 * u, a)), acc)
    acc = lax.fori_loop(0, n_tiles % 2, lambda _, a: fast_tile(n_tiles - 1, a), acc)
    l_fast = acc[HEAD_DIM:HEAD_DIM + 1]

    def exact(_):
        bias16 = bias.astype(BF16)
        qsel_ref[LANES:, :] = jnp.where(own, masked, bias16)
        s = _dot(own_keys, jnp.concatenate([qr, bias16], axis=0)) + tri
        m0 = jnp.max(s, axis=0, keepdims=True)
        p = jnp.exp2(s - m0)
        init = (m0, jnp.sum(p, axis=0, keepdims=True), _dot(vst_ref[:, pl.ds(w0, QBLK)], p.astype(BF16)))

        def tile(t, carry):
            m, l, a = carry
            k0 = pl.multiple_of(t * KTILE, KTILE)
            s = tile_scores(k0)
            m_new = jnp.maximum(m, jnp.max(s, axis=0, keepdims=True))
            alpha = jnp.exp2(m - m_new)
            p = jnp.exp2(s - m_new)
            l = alpha * l + jnp.sum(p, axis=0, keepdims=True)
            a = alpha * a + _dot(vst_ref[:, pl.ds(k0, KTILE)], p.astype(BF16))
            return m_new, l, a

        _, l, a = lax.fori_loop(0, n_tiles, tile, init)
        return a / l

    o_sel = lax.cond(jnp.min(l_fast) >= MIN_DENOM, lambda _: acc[:HEAD_DIM] / l_fast, exact, 0)

    def gate(branch):
        first = branch * N_KV * HPG + pl.program_id(1) * HPG
        return jnp.concatenate([gt_ref[pl.ds(first + hh, 1), :] for hh in range(HPG)], axis=1)

    o = gate(0) * o_cmp + gate(1) * o_sel + gate(2) * o_win
    o_ref[...] = jnp.concatenate([o[:, hh * QBLK:(hh + 1) * QBLK] for hh in range(HPG)], axis=0).astype(BF16)


def _attn_prompt(qn_t, qr_t, gt_t, k_norm, kc, vc_t, ks, blk_onehot, vs_t, kw_pad, vw_pad_t):
    nb, d, t = qn_t.shape
    ng = d // (HPG * HEAD_DIM)
    n_cmp = kc.shape[1]
    assert n_cmp <= LANES
    per_q = pl.BlockSpec((None, HPG * HEAD_DIM, QBLK), lambda b, g, i: (b, g, i))
    pair = lambda a: pl.BlockSpec((None, a.shape[1], LANES), lambda b, g, i: (b, 0, g // 2))
    group_rows = lambda a: pl.BlockSpec((None, HEAD_DIM, a.shape[2]), lambda b, g, i: (b, g, 0))
    return pl.pallas_call(
        _attn_prompt_body,
        grid=(nb, ng, t // QBLK),
        in_specs=[per_q, per_q, pl.BlockSpec((None, LANES, QBLK), lambda b, g, i: (b, 0, i)),
                  pl.BlockSpec((None, None, 8, LANES), lambda b, g, i: (b, g, 0, 0)),
                  pair(kc), group_rows(vc_t), pair(ks),
                  pl.BlockSpec(blk_onehot.shape, lambda b, g, i: (0, 0)), group_rows(vs_t),
                  pair(kw_pad), group_rows(vw_pad_t)],
        out_specs=per_q,
        out_shape=jax.ShapeDtypeStruct((nb, d, t), BF16),
        scratch_shapes=[pltpu.VMEM((2 * LANES, HPG * QBLK), BF16)],
        compiler_params=_cparams(3, 48),
        name="attn_prompt",
    )(qn_t, qr_t, gt_t, k_norm, kc, vc_t, ks, blk_onehot, vs_t, kw_pad, vw_pad_t)


def _attn_sample_body(pt_ref, qn_ref, qr_ref, gt_ref, kc_ref, vc_ref, kwin_ref, vwin_ref,
                      ksn_ref, vsn_ref, kwn_ref, vwn_ref, spread_ref, kpool_ref, vpool_ref, o_ref,
                      kbuf_ref, vbuf_ref, ksem, vsem, s_ref, *, n_pages, n_new, past_len):
    b = pl.program_id(0)
    nb = pl.num_programs(0)
    slot = b % 2

    @pl.when(b == 0)
    def _():
        _gather_start(pt_ref, kpool_ref, kbuf_ref, ksem, 0, 0, n_pages, 1, True)
        _gather_start(pt_ref, vpool_ref, vbuf_ref, vsem, 0, 0, n_pages, 1, True)

    @pl.when(b + 1 < nb)
    def _():
        _gather_start(pt_ref, kpool_ref, kbuf_ref, ksem, b + 1, 1 - slot, n_pages, 1, True)
        _gather_start(pt_ref, vpool_ref, vbuf_ref, vsem, b + 1, 1 - slot, n_pages, 1, True)

    qn = qn_ref[0]
    qr = qr_ref[0]
    n_col = qr.shape[0]
    n_past = n_pages * PAGE_SIZE
    n_blk = n_past // SEL_BLK
    n_gq = N_KV * n_new
    dot_nt = lambda a, bt: lax.dot_general(a, bt, (((1,), (1,)), ((), ())), preferred_element_type=F32)

    lane = lax.broadcasted_iota(jnp.int32, (1, LANES), 1)
    lane_pos = past_len + lane % n_new
    s = _dot(kc_ref[0], qn)
    cmp_end = lax.broadcasted_iota(jnp.int32, (n_blk, 1), 0) * CMP_BLK + (CMP_BLK - 1)
    s = jnp.where(cmp_end <= lane_pos, s, NEG_INF)
    e, l = _softmax_cols(s)
    p_cmp = e / jnp.maximum(l, 1e-30)
    o_cmp = _dot(p_cmp.T.astype(BF16), vc_ref[0])[:n_col]

    imp = p_cmp
    for hh in range(1, HPG):
        imp = imp + pltpu.roll(p_cmp, LANES - hh * n_gq, 1)
    sel = _select_bias(imp, lane_pos // SEL_BLK)
    sel01 = jnp.where((lane < n_gq) & (sel == 0.0), 1.0, 0.0)
    spread = sel01
    for hh in range(1, HPG):
        spread = spread + pltpu.roll(sel01, hh * n_gq, 1)
    sel_bias = jnp.where(spread.T[:n_col] > 0.5, 0.0, MASKED).astype(BF16)

    row = lax.broadcasted_iota(jnp.int32, (n_col, 1), 0)
    q_idx = row % n_new
    q_pos = past_len + q_idx
    n_pad = ksn_ref.shape[1]
    new_idx = lax.broadcasted_iota(jnp.int32, (1, n_pad), 1)
    new_ok = (new_idx <= q_idx) & (new_idx < n_new)

    _gather_wait(kpool_ref, kbuf_ref, ksem, slot, n_pages, 1, True)
    _gather_wait(vpool_ref, vbuf_ref, vsem, slot, n_pages, 1, True)
    pages_per_chunk = 8
    chunk = pages_per_chunk * PAGE_SIZE

    def page_rows(c, pp):
        return pl.ds(pl.multiple_of((c * pages_per_chunk + pp) * KV_W, KV_W), KV_W)

    def score_chunk(c, m):
        k0 = pl.multiple_of(c * chunk, chunk)
        s = jnp.concatenate([_dot(qr, kbuf_ref[slot, page_rows(c, pp), :].astype(BF16))
                             for pp in range(pages_per_chunk)], axis=1)
        s = s + _dot(sel_bias, spread_ref[:, pl.ds(k0, chunk)])
        s_ref[:, pl.ds(k0, chunk)] = s
        return jnp.maximum(m, jnp.max(s, axis=1, keepdims=True))

    m = lax.fori_loop(0, n_pages // pages_per_chunk, score_chunk, jnp.full((n_col, 1), NEG_INF, F32))
    s_new = jnp.where(new_ok, dot_nt(qr, ksn_ref[0].astype(BF16)), NEG_INF)
    m = jnp.maximum(m, jnp.max(s_new, axis=1, keepdims=True))
    m = jnp.where(m == NEG_INF, 0.0, m)

    def value_chunk(c, carry):
        l, acc = carry
        k0 = pl.multiple_of(c * chunk, chunk)
        p = jnp.exp2(s_ref[:, pl.ds(k0, chunk)] - m)
        l = l + jnp.sum(p, axis=1, keepdims=True)
        p = p.astype(BF16)
        for pp in range(pages_per_chunk):
            acc = acc + dot_nt(p[:, pp * PAGE_SIZE:(pp + 1) * PAGE_SIZE],
                               vbuf_ref[slot, page_rows(c, pp), :].astype(BF16))
        return l, acc

    p_new = jnp.exp2(s_new - m)
    l_sel, acc = lax.fori_loop(
        0, n_pages // pages_per_chunk, value_chunk,
        (jnp.sum(p_new, axis=1, keepdims=True), _dot(p_new.astype(BF16), vsn_ref[0].astype(BF16))))
    o_sel = acc / jnp.maximum(l_sel, 1e-30)

    n_buf = kwin_ref.shape[2]
    s_buf = _dot(qr, kwin_ref[0].astype(BF16))
    buf_pos = past_len - n_buf + lax.broadcasted_iota(jnp.int32, (1, n_buf), 1)
    diff = q_pos - buf_pos
    s_buf = jnp.where((diff >= 0) & (diff < WINDOW) & (buf_pos >= 0), s_buf, NEG_INF)
    s_nw = jnp.where(new_ok, dot_nt(qr, kwn_ref[0].astype(BF16)), NEG_INF)
    m = jnp.maximum(jnp.max(s_buf, axis=1, keepdims=True), jnp.max(s_nw, axis=1, keepdims=True))
    m = jnp.where(m == NEG_INF, 0.0, m)
    p_buf = jnp.exp2(s_buf - m)
    p_nw = jnp.exp2(s_nw - m)
    l_win = jnp.sum(p_buf, axis=1, keepdims=True) + jnp.sum(p_nw, axis=1, keepdims=True)
    o_win = (dot_nt(p_buf.astype(BF16), vwin_ref[0].astype(BF16))
             + _dot(p_nw.astype(BF16), vwn_ref[0].astype(BF16))) / jnp.maximum(l_win, 1e-30)

    gt = gt_ref[0]
    o = gt[:, 0:1] * o_cmp + gt[:, 1:2] * o_sel + gt[:, 2:3] * o_win
    row_g = (lax.broadcasted_iota(jnp.int32, (n_col, KV_W), 0) // n_new) % N_KV
    lane_g = lax.broadcasted_iota(jnp.int32, (n_col, KV_W), 1) // HEAD_DIM
    o = jnp.where(row_g == lane_g, o, 0.0)
    z = o[:, :LANES] + o[:, LANES:]
    o_ref[0] = z + pltpu.roll(z, HEAD_DIM, 1)


def _attn_sample(page_table, qn_bd, qr_bd_t, gt_col, kc, vc, kwin_t, vwin_t, ks_new, vs_new, kw_new, vw_new,
                 blk_spread, k_pool, v_pool, n_new, past_len):
    nb, n_pages = page_table.shape
    n_past = n_pages * PAGE_SIZE
    n_col = qr_bd_t.shape[1]
    per_b = lambda a: pl.BlockSpec((1,) + a.shape[1:], lambda b, pt: (b,) + (0,) * (a.ndim - 1))
    any_spec = pl.BlockSpec(memory_space=pl.ANY)
    ins = (qn_bd, qr_bd_t, gt_col, kc, vc, kwin_t, vwin_t, ks_new, vs_new, kw_new, vw_new)
    grid_spec = pltpu.PrefetchScalarGridSpec(
        num_scalar_prefetch=1,
        grid=(nb,),
        in_specs=[per_b(a) for a in ins]
        + [pl.BlockSpec(blk_spread.shape, lambda b, pt: (0, 0)), any_spec, any_spec],
        out_specs=pl.BlockSpec((1, n_col, LANES), lambda b, pt: (b, 0, 0)),
        scratch_shapes=[
            pltpu.VMEM((2, n_pages * KV_W, PAGE_SIZE), F32),
            pltpu.VMEM((2, n_pages * KV_W, PAGE_SIZE), F32),
            pltpu.SemaphoreType.DMA((2,)),
            pltpu.SemaphoreType.DMA((2,)),
            pltpu.VMEM((n_col, n_past), F32),
        ],
    )
    return pl.pallas_call(
        functools.partial(_attn_sample_body, n_pages=n_pages, n_new=n_new, past_len=past_len),
        grid_spec=grid_spec,
        out_shape=jax.ShapeDtypeStruct((nb, n_col, LANES), F32),
        compiler_params=_cparams(1, 56),
        name="attn_sample",
    )(page_table, *ins, blk_spread, k_pool, v_pool)


def _rope_tables(pos):
    half = HEAD_DIM // 2
    inv = ROPE_THETA ** (-jnp.arange(half, dtype=F32) / half)
    ang = pos.astype(F32)[:, None] * inv
    cos, sin = jnp.cos(ang), jnp.sin(ang)
    reps = LANES // HEAD_DIM
    return (jnp.tile(jnp.concatenate([cos, cos], axis=1), (1, reps)),
            jnp.tile(jnp.concatenate([-sin, sin], axis=1), (1, reps)))


def _group_sum_matrix():
    lane = jnp.arange(LANES)
    g = (lane[:, None] // HEAD_DIM == lane[None, :] // HEAD_DIM).astype(BF16)
    return jnp.concatenate([g, g], axis=0)


def _compress_weights(cmp_pos, cmp_w1, cmp_w2):
    n_pair = PAGE_SIZE // CMP_BLK
    eye = jnp.eye(n_pair, dtype=F32)
    hid = cmp_w1.shape[1]
    w1 = cmp_w1.reshape(CMP_BLK, HEAD_DIM, hid)
    wd = jnp.einsum("bk,rdh->dbrkh", eye, w1).reshape(HEAD_DIM, PAGE_SIZE, n_pair * hid).astype(BF16)
    bd2 = jnp.einsum("bk,hd->bhkd", eye, cmp_w2).reshape(n_pair * hid, n_pair * HEAD_DIM).astype(BF16)
    return jnp.tile(cmp_pos.T, (1, n_pair)), wd, bd2


def _to_rows_major(a_t):
    nb, _, rows = a_t.shape
    return a_t.reshape(nb, N_KV, HEAD_DIM, rows).transpose(0, 3, 1, 2)[None]


def _from_rows_major(a):
    n, rows = a.shape[:2]
    return a.transpose(0, 2, 3, 1).reshape(n, KV_W, rows)


def _layer_tail(x, p, layer, norm_g, w_gu, w_down, w_ple_proj, w_ple_gate, mixer_out=None):
    return _ffn(x, norm_g[layer, 2], w_gu, w_down, (layer, 1), mixer_out=mixer_out,
                ple=(norm_g[layer, 3], p, w_ple_gate, w_ple_proj))


def kernel(x_prompt, x_sample, state_conv, cache_k_cmp, cache_v_cmp, cache_k_sel, cache_v_sel,
           cache_k_win, cache_v_win, page_table, p_prompt, p_sample, norm_g, ffn_w_gu, ffn_w_down,
           ple_w_proj, ple_w_gate, conv_w_in, conv_w, conv_w_out, nsa_w_in, nsa_qk_g, nsa_cmp_pos,
           nsa_cmp_w1, nsa_cmp_w2, nsa_w_out):
    nb_p, t_p, d = x_prompt.shape
    nb_s, t_s, _ = x_sample.shape
    ple_dim = p_prompt.shape[-1]
    past_len = page_table.shape[1] * PAGE_SIZE
    assert t_p % KTILE == 0 and t_p % PAGE_SIZE == 0 and d % LANES == 0

    w_gu = ffn_w_gu.astype(BF16)
    w_down = ffn_w_down.astype(BF16)
    w_pp = ple_w_proj.astype(BF16)
    w_pg = ple_w_gate.astype(BF16)

    xp = x_prompt.reshape(nb_p * t_p, d)
    xs = x_sample.reshape(nb_s * t_s, d)
    pp = p_prompt.reshape(-1, nb_p * t_p, ple_dim)
    ps = p_sample.reshape(-1, nb_s * t_s, ple_dim)

    lw = (norm_g, w_gu, w_down, w_pp, w_pg)
    cw = (conv_w_in[0].astype(BF16), conv_w[0], conv_w_out[0].astype(BF16))
    xp = _ffn(xp, norm_g[0, 0], w_gu, w_down, (0, 0))
    xs = _ffn(xs, norm_g[0, 0], w_gu, w_down, (0, 0))
    xp3, conv_p = _conv_mixer(xp.reshape(nb_p, t_p, d), jnp.zeros((nb_p, CONV_W - 1, d), F32),
                              norm_g[0, 1], *cw, shift=1)
    xs_tb = xs.reshape(nb_s, t_s, d).transpose(1, 0, 2).reshape(1, t_s * nb_s, d)
    prev_tb = state_conv[0].transpose(1, 0, 2).reshape(1, (CONV_W - 1) * nb_s, d)
    xs3, conv_s = _conv_mixer(xs_tb, prev_tb, norm_g[0, 1], *cw, shift=nb_s)
    xs = xs3.reshape(t_s, nb_s, d).transpose(1, 0, 2).reshape(nb_s * t_s, d)
    conv_s = conv_s.reshape(CONV_W - 1, nb_s, d).transpose(1, 0, 2)
    xp = _layer_tail(xp3.reshape(nb_p * t_p, d), pp[0], 0, *lw)
    xs = _layer_tail(xs, ps[0], 0, *lw)

    xp = _ffn(xp, norm_g[1, 0], w_gu, w_down, (1, 0))
    xs = _ffn(xs, norm_g[1, 0], w_gu, w_down, (1, 0))

    n_in = nsa_w_in.shape[-1]
    n_in_pad = -(-n_in // LANES) * LANES
    n_gate = N_KV * HPG * 3
    gate_cols = nsa_w_in[0][:, n_in - n_gate:].reshape(d, N_KV * HPG, 3).transpose(0, 2, 1).reshape(d, n_gate)
    w_in = jnp.concatenate([nsa_w_in[0][:, :n_in - n_gate], gate_cols], axis=1)
    w_in = jnp.pad(w_in, ((0, 0), (0, n_in_pad - n_in))).astype(BF16)
    w_out = nsa_w_out[0].astype(BF16)
    qkg128 = jnp.tile(nsa_qk_g[0], (1, LANES // HEAD_DIM))
    gg = _group_sum_matrix()
    cos_p, sin_p = _rope_tables(jnp.arange(t_p))
    cos_s, sin_s = _rope_tables(past_len + jnp.arange(t_s))
    pos_k, wd_k, bd2_k = _compress_weights(nsa_cmp_pos[0, 0], nsa_cmp_w1[0, 0], nsa_cmp_w2[0, 0])
    pos_v, wd_v, bd2_v = _compress_weights(nsa_cmp_pos[0, 1], nsa_cmp_w1[0, 1], nsa_cmp_w2[0, 1])
    k_gain = qkg128[1:2]

    qn, qr, kc_t, vc_t, ks_t, vs_t, kw_t, vw_t, gt, ks_b, kw_b, vs_tb, vw_tb, k_norm2 = _nsa_proj(
        xp, norm_g[1, 1], w_in, qkg128, jnp.tile(cos_p, (nb_p, 1)), jnp.tile(sin_p, (nb_p, 1)), gg, n_batch=nb_p)
    pt_p = jnp.zeros((nb_p, t_p // PAGE_SIZE), jnp.int32)
    as_blocks = lambda c: c.reshape(c.shape[0], N_KV, -1, HEAD_DIM)
    by_block = lambda c: as_blocks(c).transpose(0, 2, 1, 3).reshape(c.shape[0], -1, KV_W)
    kc = _compress(pt_p, kc_t, pos_k, wd_k, bd2_k, k_gain, gg, True, False, nb_p)
    vc = _compress(pt_p, vc_t, pos_v, wd_v, bd2_v, k_gain, gg, False, False, nb_p)

    blk_onehot = (jnp.arange(t_p)[:, None] // SEL_BLK == jnp.arange(LANES)[None, :]).astype(BF16)
    k_norm = jnp.sqrt(k_norm2.reshape(nb_p, -1, N_KV, HEAD_DIM).max(axis=(1, 3)))
    k_norm = jnp.broadcast_to(k_norm[:, :, None, None], (nb_p, N_KV, 8, LANES))
    o_t = _attn_prompt(
        qn, qr, gt, k_norm,
        by_block(kc), as_blocks(vc).transpose(0, 1, 3, 2).reshape(nb_p, KV_W, -1),
        ks_b.reshape(nb_p, t_p, KV_W), blk_onehot, vs_tb,
        jnp.pad(kw_b.reshape(nb_p, t_p, KV_W), ((0, 0), (WINDOW, 0), (0, 0))),
        jnp.pad(vw_tb, ((0, 0), (0, 0), (WINDOW, 0))))
    keep = min(WINDOW, t_p)
    nsa_p = tuple(_to_rows_major(a) for a in (kc_t, vc_t, ks_t, vs_t, kw_t[:, :, -keep:], vw_t[:, :, -keep:]))

    qn, qr, kc_r, vc_r, ks, vs, kw, vw, gt = _nsa_proj(
        xs, norm_g[1, 1], w_in, qkg128, jnp.tile(cos_s, (nb_s, 1)), jnp.tile(sin_s, (nb_s, 1)), gg)
    pool_t = lambda c: _from_rows_major(c[0])
    n_seq = 2 if nb_s % 2 == 0 else 1
    kc = by_block(_compress(page_table, pool_t(cache_k_cmp), pos_k, wd_k, bd2_k, k_gain, gg, True, True, n_seq))
    vc = by_block(_compress(page_table, pool_t(cache_v_cmp), pos_v, wd_v, bd2_v, k_gain, gg, False, True, n_seq))

    n_used = HPG * N_KV * t_s

    def q_block_diag(a):
        a = a.reshape(nb_s, t_s, N_KV, HPG, HEAD_DIM).astype(F32)
        return jnp.einsum("bqghd,gk->bgdhkq", a, jnp.eye(N_KV, dtype=F32)).reshape(nb_s, KV_W, n_used)

    qn_bd = jnp.pad(q_block_diag(qn), ((0, 0), (0, 0), (0, LANES - n_used))).astype(BF16)
    qr_bd_t = q_block_diag(qr).transpose(0, 2, 1).astype(BF16)
    gt_col = gt[:, :n_gate].reshape(nb_s, t_s, 3, N_KV, HPG).transpose(0, 4, 3, 1, 2)
    gt_col = jnp.pad(gt_col.reshape(nb_s, n_used, 3), ((0, 0), (0, 0), (0, LANES - 3)))
    new_rows = lambda a: jnp.pad(a.reshape(nb_s, t_s, KV_W), ((0, 0), (0, 16 - t_s), (0, 0)))
    n_buf = cache_k_win.shape[2]
    kwin_t, vwin_t = _from_rows_major(cache_k_win[0]), _from_rows_major(cache_v_win[0])
    blk_spread = (jnp.arange(past_len // SEL_BLK)[:, None] == jnp.arange(past_len)[None, :] // SEL_BLK).astype(BF16)
    o_s = _attn_sample(
        page_table, qn_bd, qr_bd_t, gt_col, kc, vc, kwin_t, vwin_t,
        new_rows(ks), new_rows(vs), new_rows(kw), new_rows(vw), blk_spread,
        pool_t(cache_k_sel), pool_t(cache_v_sel), t_s, past_len)
    o = o_s[:, :, :HEAD_DIM].reshape(nb_s, HPG, N_KV, t_s, HEAD_DIM).transpose(0, 3, 2, 1, 4)
    o_s = o.reshape(nb_s * t_s, d).astype(BF16)
    st = lambda a: a.reshape(1, nb_s, t_s, N_KV, HEAD_DIM)
    new_t = lambda a: a.reshape(nb_s, t_s, KV_W).transpose(0, 2, 1)
    win = lambda buf_t, new: _to_rows_major(jnp.concatenate([buf_t, new_t(new)], axis=2)[:, :, -n_buf:])
    nsa_s = (st(kc_r), st(vc_r), st(ks), st(vs), win(kwin_t, kw), win(vwin_t, vw))

    xp = _layer_tail(xp, pp[1], 1, *lw, mixer_out=(o_t, w_out))
    xs = _layer_tail(xs, ps[1], 1, *lw, mixer_out=(o_s, w_out))

    return (xp.reshape(nb_p, t_p, d), xs.reshape(nb_s, t_s, d), conv_p[None], conv_s[None]) + nsa_p + nsa_s
```
